```python
import math
import jax, jax.numpy as jnp
from jax import lax
import numpy as np

D_MODEL = 2048
BATCH = 2
SEQ = 8192
DEPTH = 1

D_CONV = 1024
CONV_GROUPS = 16
CONV_WIDTH = 3
N_HEADS = 8
QK_NOPE = 128
QK_ROPE = 64
QK_HEAD = QK_NOPE + QK_ROPE
V_HEAD = 128
D_ATTN = N_HEADS * V_HEAD
Q_LORA = 512
KV_LORA = 256
ROPE_BASE = 10000.0
Q_BLOCK = 128
D_MIX = D_CONV + D_ATTN
IN_COLS = 4 * D_CONV + Q_LORA + KV_LORA + QK_ROPE + D_ATTN
EPS = 1e-6

kernel_name = "hymba_conv_mla_adaln_layer"


def _rmsnorm(x, g):
    x32 = x.astype(jnp.float32)
    y = x32 * lax.rsqrt(jnp.mean(x32 * x32, axis=-1, keepdims=True) + EPS)
    return (y * g.astype(jnp.float32)).astype(x.dtype)


def _rope_tables(positions):
    inv_freq = ROPE_BASE ** (-jnp.arange(0, QK_ROPE, 2, dtype=jnp.float32) / QK_ROPE)
    ang = positions.astype(jnp.float32)[..., None] * inv_freq
    return jnp.cos(ang), jnp.sin(ang)


def _apply_rope(x, cos, sin):
    half = QK_ROPE // 2
    x32 = x.astype(jnp.float32)
    x1, x2 = x32[..., :half], x32[..., half:]
    out = jnp.concatenate([x1 * cos - x2 * sin, x1 * sin + x2 * cos], axis=-1)
    return out.astype(x.dtype)


def _short_conv_branch(x_c, b_c, c_c, z_c, conv_w):
    u = c_c * x_c
    seq = u.shape[1]
    u_pad = jnp.pad(u, ((0, 0), (CONV_WIDTH - 1, 0), (0, 0)))
    conv = sum(conv_w[k] * u_pad[:, k:k + seq, :] for k in range(CONV_WIDTH))
    y = b_c * conv
    return y * jax.nn.silu(z_c)


def _causal_blocked_attention(q, k, v):
    bsz, seq = q.shape[0], q.shape[1]
    n_blk = seq // Q_BLOCK
    scale = 1.0 / math.sqrt(QK_HEAD)
    q_blocks = q.reshape(bsz, n_blk, Q_BLOCK, N_HEADS, QK_HEAD).transpose(1, 0, 2, 3, 4)
    key_idx = jnp.arange(seq, dtype=jnp.int32)

    def one_block(args):
        qb, blk = args
        q_idx = blk * Q_BLOCK + jnp.arange(Q_BLOCK, dtype=jnp.int32)
        s = jnp.einsum('bqhd,bkhd->bhqk', qb, k).astype(jnp.float32) * scale
        mask = key_idx[None, :] <= q_idx[:, None]
        s = jnp.where(mask[None, None], s, -jnp.inf)
        p = jax.nn.softmax(s, axis=-1).astype(v.dtype)
        return jnp.einsum('bhqk,bkhd->bqhd', p, v)

    out = lax.map(one_block, (q_blocks, jnp.arange(n_blk, dtype=jnp.int32)))
    return out.transpose(1, 0, 2, 3, 4).reshape(bsz, seq, N_HEADS, V_HEAD)


def _mla_branch(c_q, c_kv, k_rope, z_a, cos, sin, q_a_g, w_q_b, kv_a_g, w_kv_b, q_g, k_g):
    bsz, seq = c_q.shape[0], c_q.shape[1]
    q = (_rmsnorm(c_q, q_a_g) @ w_q_b).reshape(bsz, seq, N_HEADS, QK_HEAD)
    kv = (_rmsnorm(c_kv, kv_a_g) @ w_kv_b).reshape(bsz, seq, N_HEADS, QK_NOPE + V_HEAD)
    k_nope, v = kv[..., :QK_NOPE], kv[..., QK_NOPE:]
    k = jnp.concatenate([k_nope, jnp.broadcast_to(k_rope[:, :, None, :], (bsz, seq, N_HEADS, QK_ROPE))], axis=-1)
    q = _rmsnorm(q, q_g)
    k = _rmsnorm(k, k_g)
    cos_h, sin_h = cos[:, :, None, :], sin[:, :, None, :]
    q = jnp.concatenate([q[..., :QK_NOPE], _apply_rope(q[..., QK_NOPE:], cos_h, sin_h)], axis=-1)
    k = jnp.concatenate([k[..., :QK_NOPE], _apply_rope(k[..., QK_NOPE:], cos_h, sin_h)], axis=-1)
    o = _causal_blocked_attention(q, k, v).reshape(bsz, seq, D_ATTN)
    return o * jax.nn.silu(z_a)


def _layer(x, c, cos, sin, ada_w, ada_b, norm_g, w_in, conv_w, q_a_g, w_q_b, kv_a_g, w_kv_b, q_g, k_g, w_out):
    mod = jax.nn.silu(c) @ ada_w + ada_b
    shift, scale, gate = jnp.split(mod, 3, axis=-1)
    h = _rmsnorm(x, norm_g) * (1.0 + scale[:, None, :]) + shift[:, None, :]
    u = h @ w_in
    splits = np.cumsum([D_CONV, D_CONV, D_CONV, D_CONV, Q_LORA, KV_LORA, QK_ROPE])
    x_c, b_c, c_c, z_c, c_q, c_kv, k_rope, z_a = jnp.split(u, splits.tolist(), axis=-1)
    y_conv = _short_conv_branch(x_c, b_c, c_c, z_c, conv_w)
    y_attn = _mla_branch(c_q, c_kv, k_rope, z_a, cos, sin, q_a_g, w_q_b, kv_a_g, w_kv_b, q_g, k_g)
    y = jnp.concatenate([y_conv, y_attn], axis=-1) @ w_out
    return x + gate[:, None, :] * y


def setup_inputs(seed: int = 0) -> dict:
    key = jax.random.key(seed)
    ks = jax.random.split(key, 20)
    f32 = jnp.float32

    def nrm(k, shape, fan_in, mult=1.0):
        return jax.random.normal(k, shape, f32) * (mult * fan_in ** -0.5)

    def gain(k, shape):
        return 1.0 + 0.02 * jax.random.normal(k, shape, f32)

    x = jax.random.normal(ks[0], (BATCH, SEQ, D_MODEL), f32)
    c = jax.random.normal(ks[1], (BATCH, D_MODEL), f32)
    positions = jnp.broadcast_to(jnp.arange(SEQ, dtype=jnp.int32), (BATCH, SEQ))
    return {
        "x": x,
        "c": c,
        "positions": positions,
        "ada_w": nrm(ks[2], (DEPTH, D_MODEL, 3 * D_MODEL), D_MODEL, 0.5),
        "ada_b": 0.01 * jax.random.normal(ks[3], (DEPTH, 3 * D_MODEL), f32),
        "norm_g": gain(ks[4], (DEPTH, D_MODEL)),
        "w_in": nrm(ks[5], (DEPTH, D_MODEL, IN_COLS), D_MODEL),
        "conv_w": nrm(ks[6], (DEPTH, CONV_WIDTH, D_CONV), CONV_WIDTH),
        "q_a_g": gain(ks[7], (DEPTH, Q_LORA)),
        "w_q_b": nrm(ks[8], (DEPTH, Q_LORA, N_HEADS * QK_HEAD), Q_LORA),
        "kv_a_g": gain(ks[9], (DEPTH, KV_LORA)),
        "w_kv_b": nrm(ks[10], (DEPTH, KV_LORA, N_HEADS * (QK_NOPE + V_HEAD)), KV_LORA),
        "q_g": gain(ks[11], (DEPTH, QK_HEAD)),
        "k_g": gain(ks[12], (DEPTH, QK_HEAD)),
        "w_out": nrm(ks[13], (DEPTH, D_MIX, D_MODEL), D_MIX),
    }


def reference(x, c, positions, ada_w, ada_b, norm_g, w_in, conv_w, q_a_g, w_q_b, kv_a_g, w_kv_b, q_g, k_g, w_out):
    cos, sin = _rope_tables(positions)
    for l in range(DEPTH):
        x = _layer(x, c, cos, sin, ada_w[l], ada_b[l], norm_g[l], w_in[l], conv_w[l],
                   q_a_g[l], w_q_b[l], kv_a_g[l], w_kv_b[l], q_g[l], k_g[l], w_out[l])
    return x
```

```python
import functools
import math

import jax
import jax.numpy as jnp
from jax import lax
from jax.experimental import pallas as pl
from jax.experimental.pallas import tpu as pltpu

D_MODEL = 2048
D_CONV = 1024
CONV_WIDTH = 3
N_HEADS = 8
QK_NOPE = 128
QK_ROPE = 64
QK_HEAD = QK_NOPE + QK_ROPE
V_HEAD = 128
D_ATTN = N_HEADS * V_HEAD
Q_LORA = 512
KV_LORA = 256
ROPE_BASE = 10000.0
D_MIX = D_CONV + D_ATTN
EPS = 1e-6

LANES = 128
SUBLANES = 8
ROPE_SLAB = LANES
HEAD_SLAB = QK_NOPE + ROPE_SLAB
HALF = QK_ROPE // 2

OFF_XC = 0
OFF_BC = D_CONV
OFF_CC = 2 * D_CONV
OFF_ZC = 3 * D_CONV
OFF_CQ = 4 * D_CONV
OFF_CKV = OFF_CQ + Q_LORA
OFF_KR = OFF_CKV + KV_LORA
OFF_ZA = OFF_KR + ROPE_SLAB
IN_COLS_PAD = OFF_ZA + D_ATTN

ADA_TN = 512
ROPE_TT = 1024
PROJ_TM = 256
CONV_CW = 256
ATTN_TQ = 512
ATTN_TK = 512
OUT_TM = 512
VMEM_LIMIT = 56 * 1024 * 1024

_BF16 = jnp.bfloat16
_F32 = jnp.float32


def _sigmoid(z):
    return 1.0 / (1.0 + jnp.exp(-z))


def _dot(a, b):
    return jnp.dot(a, b, preferred_element_type=_F32)


def _adaln_body(ct_ref, w_ref, b_ref, o_ref):
    ct = ct_ref[...]
    sc = ct * _sigmoid(ct)
    w = w_ref[...]
    rows = [jnp.sum(w * sc[:, b:b + 1], axis=0, keepdims=True) for b in range(ct.shape[1])]
    o_ref[...] = jnp.concatenate(rows, axis=0) + b_ref[...]


def _adaln_mod(c, ada_w, ada_b):
    bsz = c.shape[0]
    n = ada_w.shape[1]
    return pl.pallas_call(
        _adaln_body,
        grid=(n // ADA_TN,),
        in_specs=[
            pl.BlockSpec((D_MODEL, bsz), lambda j: (0, 0)),
            pl.BlockSpec((D_MODEL, ADA_TN), lambda j: (0, j)),
            pl.BlockSpec((1, ADA_TN), lambda j: (0, j)),
        ],
        out_specs=pl.BlockSpec((bsz, ADA_TN), lambda j: (0, j)),
        out_shape=jax.ShapeDtypeStruct((bsz, n), _F32),
        name="adaln_mod",
    )(c.T, ada_w, ada_b.reshape(1, n))


def _rope_body(pos_ref, f_ref, cm_ref, sm_ref, cos_ref, sin_ref):
    ang = f_ref[...] * pos_ref[...].astype(_F32)
    cos_ref[...] = (jnp.cos(ang) * cm_ref[...]).T
    sin_ref[...] = (jnp.sin(ang) * sm_ref[...]).T


def _rope_tables(positions):
    t = positions.size
    inv_freq = ROPE_BASE ** (-jnp.arange(0, QK_ROPE, 2, dtype=_F32) / QK_ROPE)
    zeros = jnp.zeros((HALF,), _F32)
    ones = jnp.ones((HALF,), _F32)
    f_col = jnp.concatenate([inv_freq, zeros, inv_freq, zeros]).reshape(ROPE_SLAB, 1)
    cmask = jnp.concatenate([ones, zeros, ones, zeros]).reshape(ROPE_SLAB, 1)
    smask = jnp.concatenate([-ones, zeros, ones, zeros]).reshape(ROPE_SLAB, 1)
    col = pl.BlockSpec((ROPE_SLAB, 1), lambda i: (0, 0))
    out = pl.BlockSpec((ROPE_TT, ROPE_SLAB), lambda i: (i, 0))
    return pl.pallas_call(
        _rope_body,
        grid=(t // ROPE_TT,),
        in_specs=[pl.BlockSpec((1, ROPE_TT), lambda i: (0, i)), col, col, col],
        out_specs=[out, out],
        out_shape=[jax.ShapeDtypeStruct((t, ROPE_SLAB), _F32)] * 2,
        name="rope_tables",
    )(positions.reshape(1, t), f_col, cmask, smask)


def _in_proj_body(tiles_per_batch,
                  x_ref, mod_ref, ng_ref, win_ref, convw_ref, qag_ref, wqb_ref, kvag_ref, wkvb_ref,
                  gq_ref, gk_ref, cos_ref, sin_ref,
                  yconv_ref, sz_ref, q_ref, k_ref, v_ref,
                  h_scr, carry_scr, vbuf_scr):
    tm = x_ref.shape[0]

    @pl.when(pl.program_id(0) % tiles_per_batch == 0)
    def _():
        carry_scr[...] = jnp.zeros_like(carry_scr)

    x = x_ref[...]
    xn = x * lax.rsqrt(jnp.mean(x * x, axis=-1, keepdims=True) + EPS)
    a = ng_ref[...] * (1.0 + mod_ref[0, 1:2, :])
    h_scr[...] = (xn * a + mod_ref[0, 0:1, :]).astype(_BF16)

    def proj(off, width):
        return _dot(h_scr[...], win_ref[:, off:off + width])

    for lo in range(0, D_CONV, CONV_CW):
        xc = proj(OFF_XC + lo, CONV_CW)
        bc = proj(OFF_BC + lo, CONV_CW)
        cc = proj(OFF_CC + lo, CONV_CW)
        zc = proj(OFF_ZC + lo, CONV_CW)
        v0 = cc * xc
        vbuf_scr[0:SUBLANES, :] = carry_scr[:, lo:lo + CONV_CW]
        vbuf_scr[SUBLANES:SUBLANES + tm, :] = v0
        carry_scr[:, lo:lo + CONV_CW] = v0[tm - SUBLANES:, :]
        v1 = vbuf_scr[SUBLANES - 1:SUBLANES - 1 + tm, :]
        v2 = vbuf_scr[SUBLANES - 2:SUBLANES - 2 + tm, :]
        w = convw_ref[:, lo:lo + CONV_CW]
        conv = w[0:1, :] * v2 + w[1:2, :] * v1 + w[2:3, :] * v0
        yconv_ref[:, lo:lo + CONV_CW] = (bc * conv * (zc * _sigmoid(zc))).astype(_BF16)

    za = proj(OFF_ZA, D_ATTN)
    sz_ref[...] = (za * _sigmoid(za)).astype(_BF16)

    cos_t = cos_ref[...]
    sin_t = sin_ref[...]

    def rope(slab):
        return slab * cos_t + pltpu.roll(slab, ROPE_SLAB // 2, axis=1) * sin_t

    cq = proj(OFF_CQ, Q_LORA)
    cqn = cq * lax.rsqrt(jnp.mean(cq * cq, axis=-1, keepdims=True) + EPS) * qag_ref[...]
    qf = _dot(cqn.astype(_BF16), wqb_ref[...])
    gq = gq_ref[...]
    for hd in range(N_HEADS):
        slab = qf[:, hd * HEAD_SLAB:(hd + 1) * HEAD_SLAB]
        r = lax.rsqrt(jnp.sum(slab * slab, axis=-1, keepdims=True) * (1.0 / QK_HEAD) + EPS)
        qn = slab * r * gq
        q_ref[0, hd, :, 0:QK_NOPE] = qn[:, 0:QK_NOPE].astype(_BF16)
        q_ref[0, hd, :, QK_NOPE:HEAD_SLAB] = rope(qn[:, QK_NOPE:HEAD_SLAB]).astype(_BF16)

    ckv = proj(OFF_CKV, KV_LORA)
    ckvn = ckv * lax.rsqrt(jnp.mean(ckv * ckv, axis=-1, keepdims=True) + EPS) * kvag_ref[...]
    kvf = _dot(ckvn.astype(_BF16), wkvb_ref[...])
    kr = proj(OFF_KR, ROPE_SLAB)
    ss_r = jnp.sum(kr * kr, axis=-1, keepdims=True)
    gk = gk_ref[...]
    kr_rot = rope(kr * gk[:, QK_NOPE:HEAD_SLAB])
    for hd in range(N_HEADS):
        base = hd * (QK_NOPE + V_HEAD)
        kn = kvf[:, base:base + QK_NOPE]
        r = lax.rsqrt((jnp.sum(kn * kn, axis=-1, keepdims=True) + ss_r) * (1.0 / QK_HEAD) + EPS)
        k_ref[0, hd, :, 0:QK_NOPE] = (kn * r * gk[:, 0:QK_NOPE]).astype(_BF16)
        k_ref[0, hd, :, QK_NOPE:HEAD_SLAB] = (kr_rot * r).astype(_BF16)
        v_ref[0, hd, :, :] = kvf[:, base + QK_NOPE:base + QK_NOPE + V_HEAD].astype(_BF16)


def _resident(shape):
    return pl.BlockSpec(shape, lambda i: (0,) * len(shape), pipeline_mode=pl.Buffered(1))


def _in_proj(x2d, mod3, norm_g, win, conv_w, q_a_g, wqb, kv_a_g, wkvb, gq, gk, cos_t, sin_t, bsz, seq):
    t = x2d.shape[0]
    tm = PROJ_TM
    tpb = seq // tm
    tok = lambda width: pl.BlockSpec((tm, width), lambda i: (i, 0))
    head = lambda width: pl.BlockSpec((1, N_HEADS, tm, width), lambda i: (i // tpb, 0, i % tpb, 0))
    return pl.pallas_call(
        functools.partial(_in_proj_body, tpb),
        grid=(t // tm,),
        in_specs=[
            tok(D_MODEL),
            pl.BlockSpec((1, 3, D_MODEL), lambda i: (i // tpb, 0, 0)),
            _resident((1, D_MODEL)),
            _resident((D_MODEL, IN_COLS_PAD)),
            _resident((CONV_WIDTH, D_CONV)),
            _resident((1, Q_LORA)),
            _resident((Q_LORA, N_HEADS * HEAD_SLAB)),
            _resident((1, KV_LORA)),
            _resident((KV_LORA, N_HEADS * (QK_NOPE + V_HEAD))),
            _resident((1, HEAD_SLAB)),
            _resident((1, HEAD_SLAB)),
            tok(ROPE_SLAB),
            tok(ROPE_SLAB),
        ],
        out_specs=[tok(D_CONV), tok(D_ATTN), head(HEAD_SLAB), head(HEAD_SLAB), head(V_HEAD)],
        out_shape=[
            jax.ShapeDtypeStruct((t, D_CONV), _BF16),
            jax.ShapeDtypeStruct((t, D_ATTN), _BF16),
            jax.ShapeDtypeStruct((bsz, N_HEADS, seq, HEAD_SLAB), _BF16),
            jax.ShapeDtypeStruct((bsz, N_HEADS, seq, HEAD_SLAB), _BF16),
            jax.ShapeDtypeStruct((bsz, N_HEADS, seq, V_HEAD), _BF16),
        ],
        scratch_shapes=[
            pltpu.VMEM((tm, D_MODEL), _BF16),
            pltpu.VMEM((SUBLANES, D_CONV), _F32),
            pltpu.VMEM((SUBLANES + tm, CONV_CW), _F32),
        ],
        compiler_params=pltpu.CompilerParams(
            dimension_semantics=("arbitrary",), vmem_limit_bytes=VMEM_LIMIT),
        name="in_proj",
    )(x2d, mod3, norm_g, win, conv_w, q_a_g, wqb, kv_a_g, wkvb, gq, gk, cos_t, sin_t)


def _attn_body(q_ref, k_ref, v_ref, sz_ref, o_ref, m_scr, l_scr, acc_scr):
    tq = q_ref.shape[2]
    tk = ATTN_TK
    qi = pl.program_id(2)
    q = q_ref[0, 0]

    m_scr[...] = jnp.full_like(m_scr, -jnp.inf)
    l_scr[...] = jnp.zeros_like(l_scr)
    acc_scr[...] = jnp.zeros_like(acc_scr)

    def step(j, diag_offset):
        start = pl.multiple_of(j * tk, tk)
        k = k_ref[0, 0, pl.ds(start, tk), :]
        v = v_ref[0, 0, pl.ds(start, tk), :]
        s = lax.dot_general(q, k, (((1,), (1,)), ((), ())), preferred_element_type=_F32)
        if diag_offset is not None:
            row = lax.broadcasted_iota(jnp.int32, (tq, tk), 0)
            col = lax.broadcasted_iota(jnp.int32, (tq, tk), 1) + diag_offset
            s = jnp.where(col <= row, s, -jnp.inf)
        m_prev = m_scr[...]
        m_new = jnp.maximum(m_prev, jnp.max(s, axis=-1, keepdims=True))
        alpha = jnp.exp2(m_prev - m_new)
        p = jnp.exp2(s - m_new)
        l_scr[...] = alpha * l_scr[...] + jnp.sum(p, axis=-1, keepdims=True)
        acc_scr[...] = alpha * acc_scr[...] + _dot(p.astype(_BF16), v)
        m_scr[...] = m_new

    n_diag = tq // tk
    n_full = qi * n_diag

    def full_step(j, carry):
        step(j, None)
        return carry

    lax.fori_loop(0, n_full, full_step, 0)
    for d in range(n_diag):
        step(n_full + d, d * tk)

    o = acc_scr[...] / l_scr[...]
    o_ref[...] = (o * sz_ref[...].astype(_F32)).astype(_BF16)


def _attention(q, k, v, sz):
    bsz, _, seq, _ = q.shape
    tq = ATTN_TQ
    nq = seq // tq
    return pl.pallas_call(
        _attn_body,
        grid=(bsz, N_HEADS, nq),
        in_specs=[
            pl.BlockSpec((1, 1, tq, HEAD_SLAB), lambda b, h, i: (b, h, i, 0)),
            pl.BlockSpec((1, 1, seq, HEAD_SLAB), lambda b, h, i: (b, h, 0, 0)),
            pl.BlockSpec((1, 1, seq, V_HEAD), lambda b, h, i: (b, h, 0, 0)),
            pl.BlockSpec((tq, V_HEAD), lambda b, h, i: (b * nq + i, h)),
        ],
        out_specs=pl.BlockSpec((tq, V_HEAD), lambda b, h, i: (b * nq + i, h)),
        out_shape=jax.ShapeDtypeStruct((bsz * seq, D_ATTN), _BF16),
        scratch_shapes=[
            pltpu.VMEM((tq, 1), _F32),
            pltpu.VMEM((tq, 1), _F32),
            pltpu.VMEM((tq, V_HEAD), _F32),
        ],
        compiler_params=pltpu.CompilerParams(
            dimension_semantics=("arbitrary", "arbitrary", "arbitrary"), vmem_limit_bytes=VMEM_LIMIT),
        name="attention",
    )(q, k, v, sz)


def _out_proj_body(yc_ref, ya_ref, x_ref, mod_ref, w_ref, o_ref):
    y = _dot(yc_ref[...], w_ref[0:D_CONV, :]) + _dot(ya_ref[...], w_ref[D_CONV:D_MIX, :])
    o_ref[...] = x_ref[...] + mod_ref[0, 2:3, :] * y


def _out_proj(yconv, yattn, x2d, mod3, wout, seq):
    t = x2d.shape[0]
    tm = OUT_TM
    tpb = seq // tm
    return pl.pallas_call(
        _out_proj_body,
        grid=(t // tm,),
        in_specs=[
            pl.BlockSpec((tm, D_CONV), lambda i: (i, 0)),
            pl.BlockSpec((tm, D_ATTN), lambda i: (i, 0)),
            pl.BlockSpec((tm, D_MODEL), lambda i: (i, 0)),
            pl.BlockSpec((1, 3, D_MODEL), lambda i: (i // tpb, 0, 0)),
            _resident((D_MIX, D_MODEL)),
        ],
        out_specs=pl.BlockSpec((tm, D_MODEL), lambda i: (i, 0)),
        out_shape=jax.ShapeDtypeStruct((t, D_MODEL), _F32),
        compiler_params=pltpu.CompilerParams(
            dimension_semantics=("arbitrary",), vmem_limit_bytes=VMEM_LIMIT),
        name="out_proj",
    )(yconv, yattn, x2d, mod3, wout)


def _rope_slab_cols(w):
    z = jnp.zeros(w.shape[:-1] + (HALF,), w.dtype)
    return jnp.concatenate([w[..., :HALF], z, w[..., HALF:], z], axis=-1)


def _layer(x2d, mod, cos_t, sin_t, norm_g, w_in, conv_w, q_a_g, w_q_b, kv_a_g, w_kv_b, q_g, k_g, w_out, bsz, seq):
    off_kr = 4 * D_CONV + Q_LORA + KV_LORA
    win = jnp.concatenate(
        [w_in[:, :off_kr], _rope_slab_cols(w_in[:, off_kr:off_kr + QK_ROPE]), w_in[:, off_kr + QK_ROPE:]],
        axis=-1).astype(_BF16)
    wq = w_q_b.reshape(Q_LORA, N_HEADS, QK_HEAD)
    wqb = jnp.concatenate([wq[..., :QK_NOPE], _rope_slab_cols(wq[..., QK_NOPE:])], axis=-1)
    wqb = wqb.reshape(Q_LORA, N_HEADS * HEAD_SLAB).astype(_BF16)
    wkvb = w_kv_b.astype(_BF16)
    qscale = math.log2(math.e) / math.sqrt(QK_HEAD)
    gq = jnp.concatenate([q_g[:QK_NOPE], _rope_slab_cols(q_g[QK_NOPE:])]).reshape(1, HEAD_SLAB) * qscale
    gk = jnp.concatenate([k_g[:QK_NOPE], _rope_slab_cols(k_g[QK_NOPE:])]).reshape(1, HEAD_SLAB)
    mod3 = mod.reshape(bsz, 3, D_MODEL)

    yconv, sz, q, k, v = _in_proj(
        x2d, mod3, norm_g.reshape(1, D_MODEL), win, conv_w, q_a_g.reshape(1, Q_LORA), wqb,
        kv_a_g.reshape(1, KV_LORA), wkvb, gq, gk, cos_t, sin_t, bsz, seq)
    yattn = _attention(q, k, v, sz)
    return _out_proj(yconv, yattn, x2d, mod3, w_out.astype(_BF16), seq)


def kernel(x, c, positions, ada_w, ada_b, norm_g, w_in, conv_w, q_a_g, w_q_b, kv_a_g, w_kv_b, q_g, k_g, w_out):
    bsz, seq, _ = x.shape
    depth = ada_w.shape[0]
    cos_t, sin_t = _rope_tables(positions)
    x2d = x.reshape(bsz * seq, D_MODEL)
    for l in range(depth):
        mod = _adaln_mod(c, ada_w[l], ada_b[l])
        x2d = _layer(x2d, mod, cos_t, sin_t, norm_g[l], w_in[l], conv_w[l], q_a_g[l], w_q_b[l],
                     kv_a_g[l], w_kv_b[l], q_g[l], k_g[l], w_out[l], bsz, seq)
    return x2d.reshape(bsz, seq, D_MODEL)
```

```python
import functools
import math

import jax
import jax.numpy as jnp
from jax import lax
from jax.experimental import pallas as pl
from jax.experimental.pallas import tpu as pltpu

D_MODEL = 2048
D_CONV = 1024
CONV_WIDTH = 3
N_HEADS = 8
QK_NOPE = 128
QK_ROPE = 64
QK_HEAD = QK_NOPE + QK_ROPE
V_HEAD = 128
D_ATTN = N_HEADS * V_HEAD
Q_LORA = 512
KV_LORA = 256
ROPE_BASE = 10000.0
D_MIX = D_CONV + D_ATTN
EPS = 1e-6

LANES = 128
SUBLANES = 8
ROPE_SLAB = LANES
HEAD_SLAB = QK_NOPE + ROPE_SLAB
HALF = QK_ROPE // 2

OFF_XC = 0
OFF_BC = D_CONV
OFF_CC = 2 * D_CONV
OFF_ZC = 3 * D_CONV
OFF_CQ = 4 * D_CONV
OFF_CKV = OFF_CQ + Q_LORA
OFF_KR = OFF_CKV + KV_LORA
OFF_ZA = OFF_KR + ROPE_SLAB
IN_COLS_PAD = OFF_ZA + D_ATTN

ADA_TN = 512
ROPE_TT = 1024
PROJ_TM = 256
CONV_CW = 256
ATTN_TQ = 1024
ATTN_TK = 512
OUT_TM = 512
VMEM_LIMIT = 56 * 1024 * 1024

_BF16 = jnp.bfloat16
_F32 = jnp.float32


def _sigmoid(z):
    return 1.0 / (1.0 + jnp.exp(-z))


def _dot(a, b):
    return jnp.dot(a, b, preferred_element_type=_F32)


def _adaln_body(ct_ref, w_ref, b_ref, o_ref):
    ct = ct_ref[...]
    sc = ct * _sigmoid(ct)
    w = w_ref[...]
    rows = [jnp.sum(w * sc[:, b:b + 1], axis=0, keepdims=True) for b in range(ct.shape[1])]
    o_ref[...] = jnp.concatenate(rows, axis=0) + b_ref[...]


def _adaln_mod(c, ada_w, ada_b):
    bsz = c.shape[0]
    n = ada_w.shape[1]
    return pl.pallas_call(
        _adaln_body,
        grid=(n // ADA_TN,),
        in_specs=[
            pl.BlockSpec((D_MODEL, bsz), lambda j: (0, 0)),
            pl.BlockSpec((D_MODEL, ADA_TN), lambda j: (0, j)),
            pl.BlockSpec((1, ADA_TN), lambda j: (0, j)),
        ],
        out_specs=pl.BlockSpec((bsz, ADA_TN), lambda j: (0, j)),
        out_shape=jax.ShapeDtypeStruct((bsz, n), _F32),
        name="adaln_mod",
    )(c.T, ada_w, ada_b.reshape(1, n))


def _rope_body(pos_ref, f_ref, cm_ref, sm_ref, cos_ref, sin_ref):
    ang = f_ref[...] * pos_ref[...].astype(_F32)
    cos_ref[...] = (jnp.cos(ang) * cm_ref[...]).T
    sin_ref[...] = (jnp.sin(ang) * sm_ref[...]).T


def _rope_tables(positions):
    t = positions.size
    inv_freq = ROPE_BASE ** (-jnp.arange(0, QK_ROPE, 2, dtype=_F32) / QK_ROPE)
    zeros = jnp.zeros((HALF,), _F32)
    ones = jnp.ones((HALF,), _F32)
    f_col = jnp.concatenate([inv_freq, zeros, inv_freq, zeros]).reshape(ROPE_SLAB, 1)
    cmask = jnp.concatenate([ones, zeros, ones, zeros]).reshape(ROPE_SLAB, 1)
    smask = jnp.concatenate([-ones, zeros, ones, zeros]).reshape(ROPE_SLAB, 1)
    col = pl.BlockSpec((ROPE_SLAB, 1), lambda i: (0, 0))
    out = pl.BlockSpec((ROPE_TT, ROPE_SLAB), lambda i: (i, 0))
    return pl.pallas_call(
        _rope_body,
        grid=(t // ROPE_TT,),
        in_specs=[pl.BlockSpec((1, ROPE_TT), lambda i: (0, i)), col, col, col],
        out_specs=[out, out],
        out_shape=[jax.ShapeDtypeStruct((t, ROPE_SLAB), _F32)] * 2,
        name="rope_tables",
    )(positions.reshape(1, t), f_col, cmask, smask)


def _in_proj_body(tiles_per_batch,
                  x_ref, mod_ref, ng_ref, win_ref, convw_ref, qag_ref, wqb_ref, kvag_ref, wkvb_ref,
                  gq_ref, gk_ref, cos_ref, sin_ref,
                  yconv_ref, sz_ref, q_ref, k_ref, v_ref,
                  h_scr, carry_scr, vbuf_scr):
    tm = x_ref.shape[0]

    @pl.when(pl.program_id(0) % tiles_per_batch == 0)
    def _():
        carry_scr[...] = jnp.zeros_like(carry_scr)

    x = x_ref[...]
    xn = x * lax.rsqrt(jnp.mean(x * x, axis=-1, keepdims=True) + EPS)
    a = ng_ref[...] * (1.0 + mod_ref[0, 1:2, :])
    h_scr[...] = (xn * a + mod_ref[0, 0:1, :]).astype(_BF16)

    def proj(off, width):
        return _dot(h_scr[...], win_ref[:, off:off + width])

    for lo in range(0, D_CONV, CONV_CW):
        xc = proj(OFF_XC + lo, CONV_CW)
        bc = proj(OFF_BC + lo, CONV_CW)
        cc = proj(OFF_CC + lo, CONV_CW)
        zc = proj(OFF_ZC + lo, CONV_CW)
        v0 = cc * xc
        vbuf_scr[0:SUBLANES, :] = carry_scr[:, lo:lo + CONV_CW]
        vbuf_scr[SUBLANES:SUBLANES + tm, :] = v0
        carry_scr[:, lo:lo + CONV_CW] = v0[tm - SUBLANES:, :]
        v1 = vbuf_scr[SUBLANES - 1:SUBLANES - 1 + tm, :]
        v2 = vbuf_scr[SUBLANES - 2:SUBLANES - 2 + tm, :]
        w = convw_ref[:, lo:lo + CONV_CW]
        conv = w[0:1, :] * v2 + w[1:2, :] * v1 + w[2:3, :] * v0
        yconv_ref[:, lo:lo + CONV_CW] = (bc * conv * (zc * _sigmoid(zc))).astype(_BF16)

    za = proj(OFF_ZA, D_ATTN)
    sz_ref[...] = (za * _sigmoid(za)).astype(_BF16)

    cos_t = cos_ref[...]
    sin_t = sin_ref[...]

    def rope(slab):
        return slab * cos_t + pltpu.roll(slab, ROPE_SLAB // 2, axis=1) * sin_t

    cq = proj(OFF_CQ, Q_LORA)
    cqn = cq * lax.rsqrt(jnp.mean(cq * cq, axis=-1, keepdims=True) + EPS) * qag_ref[...]
    qf = _dot(cqn.astype(_BF16), wqb_ref[...])
    gq = gq_ref[...]
    for hd in range(N_HEADS):
        slab = qf[:, hd * HEAD_SLAB:(hd + 1) * HEAD_SLAB]
        r = lax.rsqrt(jnp.sum(slab * slab, axis=-1, keepdims=True) * (1.0 / QK_HEAD) + EPS)
        qn = slab * r * gq
        q_ref[0, hd, :, 0:QK_NOPE] = qn[:, 0:QK_NOPE].astype(_BF16)
        q_ref[0, hd, :, QK_NOPE:HEAD_SLAB] = rope(qn[:, QK_NOPE:HEAD_SLAB]).astype(_BF16)

    ckv = proj(OFF_CKV, KV_LORA)
    ckvn = ckv * lax.rsqrt(jnp.mean(ckv * ckv, axis=-1, keepdims=True) + EPS) * kvag_ref[...]
    kvf = _dot(ckvn.astype(_BF16), wkvb_ref[...])
    kr = proj(OFF_KR, ROPE_SLAB)
    ss_r = jnp.sum(kr * kr, axis=-1, keepdims=True)
    gk = gk_ref[...]
    kr_rot = rope(kr * gk[:, QK_NOPE:HEAD_SLAB])
    for hd in range(N_HEADS):
        base = hd * (QK_NOPE + V_HEAD)
        kn = kvf[:, base:base + QK_NOPE]
        r = lax.rsqrt((jnp.sum(kn * kn, axis=-1, keepdims=True) + ss_r) * (1.0 / QK_HEAD) + EPS)
        k_ref[0, hd, :, 0:QK_NOPE] = (kn * r * gk[:, 0:QK_NOPE]).astype(_BF16)
        k_ref[0, hd, :, QK_NOPE:HEAD_SLAB] = (kr_rot * r).astype(_BF16)
        v_ref[0, hd, :, :] = kvf[:, base + QK_NOPE:base + QK_NOPE + V_HEAD].astype(_BF16)


def _resident(shape):
    return pl.BlockSpec(shape, lambda i: (0,) * len(shape), pipeline_mode=pl.Buffered(1))


def _in_proj(x2d, mod3, norm_g, win, conv_w, q_a_g, wqb, kv_a_g, wkvb, gq, gk, cos_t, sin_t, bsz, seq):
    t = x2d.shape[0]
    tm = PROJ_TM
    tpb = seq // tm
    tok = lambda width: pl.BlockSpec((tm, width), lambda i: (i, 0))
    head = lambda width: pl.BlockSpec((1, N_HEADS, tm, width), lambda i: (i // tpb, 0, i % tpb, 0))
    return pl.pallas_call(
        functools.partial(_in_proj_body, tpb),
        grid=(t // tm,),
        in_specs=[
            tok(D_MODEL),
            pl.BlockSpec((1, 3, D_MODEL), lambda i: (i // tpb, 0, 0)),
            _resident((1, D_MODEL)),
            _resident((D_MODEL, IN_COLS_PAD)),
            _resident((CONV_WIDTH, D_CONV)),
            _resident((1, Q_LORA)),
            _resident((Q_LORA, N_HEADS * HEAD_SLAB)),
            _resident((1, KV_LORA)),
            _resident((KV_LORA, N_HEADS * (QK_NOPE + V_HEAD))),
            _resident((1, HEAD_SLAB)),
            _resident((1, HEAD_SLAB)),
            tok(ROPE_SLAB),
            tok(ROPE_SLAB),
        ],
        out_specs=[tok(D_CONV), tok(D_ATTN), head(HEAD_SLAB), head(HEAD_SLAB), head(V_HEAD)],
        out_shape=[
            jax.ShapeDtypeStruct((t, D_CONV), _BF16),
            jax.ShapeDtypeStruct((t, D_ATTN), _BF16),
            jax.ShapeDtypeStruct((bsz, N_HEADS, seq, HEAD_SLAB), _BF16),
            jax.ShapeDtypeStruct((bsz, N_HEADS, seq, HEAD_SLAB), _BF16),
            jax.ShapeDtypeStruct((bsz, N_HEADS, seq, V_HEAD), _BF16),
        ],
        scratch_shapes=[
            pltpu.VMEM((tm, D_MODEL), _BF16),
            pltpu.VMEM((SUBLANES, D_CONV), _F32),
            pltpu.VMEM((SUBLANES + tm, CONV_CW), _F32),
        ],
        compiler_params=pltpu.CompilerParams(
            dimension_semantics=("arbitrary",), vmem_limit_bytes=VMEM_LIMIT),
        name="in_proj",
    )(x2d, mod3, norm_g, win, conv_w, q_a_g, wqb, kv_a_g, wkvb, gq, gk, cos_t, sin_t)


def _attn_body(q_ref, k_ref, v_ref, sz_ref, o_ref, sa_scr, sb_scr, m_scr, l_scr, acc_scr):
    tq = q_ref.shape[2]
    tk = ATTN_TK
    half = tq // 2
    assert tq == 2 * tk
    qi = pl.program_id(2)

    m_scr[...] = jnp.full_like(m_scr, -jnp.inf)
    l_scr[...] = jnp.zeros_like(l_scr)
    acc_scr[...] = jnp.zeros_like(acc_scr)

    def scores(j, r0, nr):
        start = pl.multiple_of(j * tk, tk)
        k = k_ref[0, 0, pl.ds(start, tk), :]
        return lax.dot_general(q_ref[0, 0, r0:r0 + nr, :], k, (((1,), (1,)), ((), ())),
                               preferred_element_type=_F32)

    def update(j, s, r0, nr, triangular):
        start = pl.multiple_of(j * tk, tk)
        v = v_ref[0, 0, pl.ds(start, tk), :]
        if triangular:
            row = lax.broadcasted_iota(jnp.int32, (nr, tk), 0)
            col = lax.broadcasted_iota(jnp.int32, (nr, tk), 1)
            s = jnp.where(col <= row, s, -jnp.inf)
        tiles = [s[:, t * LANES:(t + 1) * LANES] for t in range(tk // LANES)]
        mx = functools.reduce(jnp.maximum, tiles)
        m_prev = m_scr[r0:r0 + nr, :]
        m_new = jnp.maximum(m_prev, jnp.max(mx, axis=-1, keepdims=True))
        alpha = jnp.exp2(m_prev - m_new)
        ps = [jnp.exp2(t - m_new) for t in tiles]
        l_scr[r0:r0 + nr, :] = alpha * l_scr[r0:r0 + nr, :] + functools.reduce(jnp.add, ps)
        p = jnp.concatenate(ps, axis=-1).astype(_BF16)
        acc_scr[r0:r0 + nr, :] = alpha * acc_scr[r0:r0 + nr, :] + _dot(p, v)
        m_scr[r0:r0 + nr, :] = m_new

    n_pairs = qi
    sa_scr[...] = scores(0, 0, tq)

    def pair(jj, carry):
        j = 2 * jj
        sb_scr[...] = scores(j + 1, 0, tq)
        update(j, sa_scr[...], 0, tq, False)
        sa_scr[...] = scores(j + 2, 0, tq)
        update(j + 1, sb_scr[...], 0, tq, False)
        return carry

    lax.fori_loop(0, n_pairs, pair, 0)

    jd = 2 * n_pairs
    sb_scr[half:tq, :] = scores(jd + 1, half, half)
    update(jd, sa_scr[0:half, :], 0, half, True)
    update(jd, sa_scr[half:tq, :], half, half, False)
    update(jd + 1, sb_scr[half:tq, :], half, half, True)

    o = acc_scr[...] / jnp.sum(l_scr[...], axis=-1, keepdims=True)
    o_ref[...] = (o * sz_ref[...].astype(_F32)).astype(_BF16)


def _attention(q, k, v, sz):
    bsz, _, seq, _ = q.shape
    tq = ATTN_TQ
    nq = seq // tq
    return pl.pallas_call(
        _attn_body,
        grid=(bsz, N_HEADS, nq),
        in_specs=[
            pl.BlockSpec((1, 1, tq, HEAD_SLAB), lambda b, h, i: (b, h, i, 0)),
            pl.BlockSpec((1, 1, seq, HEAD_SLAB), lambda b, h, i: (b, h, 0, 0)),
            pl.BlockSpec((1, 1, seq, V_HEAD), lambda b, h, i: (b, h, 0, 0)),
            pl.BlockSpec((tq, V_HEAD), lambda b, h, i: (b * nq + i, h)),
        ],
        out_specs=pl.BlockSpec((tq, V_HEAD), lambda b, h, i: (b * nq + i, h)),
        out_shape=jax.ShapeDtypeStruct((bsz * seq, D_ATTN), _BF16),
        scratch_shapes=[
            pltpu.VMEM((tq, ATTN_TK), _F32),
            pltpu.VMEM((tq, ATTN_TK), _F32),
            pltpu.VMEM((tq, LANES), _F32),
            pltpu.VMEM((tq, LANES), _F32),
            pltpu.VMEM((tq, V_HEAD), _F32),
        ],
        compiler_params=pltpu.CompilerParams(
            dimension_semantics=("arbitrary", "arbitrary", "arbitrary"), vmem_limit_bytes=VMEM_LIMIT),
        name="attention",
    )(q, k, v, sz)


def _out_proj_body(yc_ref, ya_ref, x_ref, mod_ref, w_ref, o_ref):
    y = _dot(yc_ref[...], w_ref[0:D_CONV, :]) + _dot(ya_ref[...], w_ref[D_CONV:D_MIX, :])
    o_ref[...] = x_ref[...] + mod_ref[0, 2:3, :] * y


def _out_proj(yconv, yattn, x2d, mod3, wout, seq):
    t = x2d.shape[0]
    tm = OUT_TM
    tpb = seq // tm
    return pl.pallas_call(
        _out_proj_body,
        grid=(t // tm,),
        in_specs=[
            pl.BlockSpec((tm, D_CONV), lambda i: (i, 0)),
            pl.BlockSpec((tm, D_ATTN), lambda i: (i, 0)),
            pl.BlockSpec((tm, D_MODEL), lambda i: (i, 0)),
            pl.BlockSpec((1, 3, D_MODEL), lambda i: (i // tpb, 0, 0)),
            _resident((D_MIX, D_MODEL)),
        ],
        out_specs=pl.BlockSpec((tm, D_MODEL), lambda i: (i, 0)),
        out_shape=jax.ShapeDtypeStruct((t, D_MODEL), _F32),
        compiler_params=pltpu.CompilerParams(
            dimension_semantics=("arbitrary",), vmem_limit_bytes=VMEM_LIMIT),
        name="out_proj",
    )(yconv, yattn, x2d, mod3, wout)


def _rope_slab_cols(w):
    z = jnp.zeros(w.shape[:-1] + (HALF,), w.dtype)
    return jnp.concatenate([w[..., :HALF], z, w[..., HALF:], z], axis=-1)


def _layer(x2d, mod, cos_t, sin_t, norm_g, w_in, conv_w, q_a_g, w_q_b, kv_a_g, w_kv_b, q_g, k_g, w_out, bsz, seq):
    off_kr = 4 * D_CONV + Q_LORA + KV_LORA
    win = jnp.concatenate(
        [w_in[:, :off_kr], _rope_slab_cols(w_in[:, off_kr:off_kr + QK_ROPE]), w_in[:, off_kr + QK_ROPE:]],
        axis=-1).astype(_BF16)
    wq = w_q_b.reshape(Q_LORA, N_HEADS, QK_HEAD)
    wqb = jnp.concatenate([wq[..., :QK_NOPE], _rope_slab_cols(wq[..., QK_NOPE:])], axis=-1)
    wqb = wqb.reshape(Q_LORA, N_HEADS * HEAD_SLAB).astype(_BF16)
    wkvb = w_kv_b.astype(_BF16)
    qscale = math.log2(math.e) / math.sqrt(QK_HEAD)
    gq = jnp.concatenate([q_g[:QK_NOPE], _rope_slab_cols(q_g[QK_NOPE:])]).reshape(1, HEAD_SLAB) * qscale
    gk = jnp.concatenate([k_g[:QK_NOPE], _rope_slab_cols(k_g[QK_NOPE:])]).reshape(1, HEAD_SLAB)
    mod3 = mod.reshape(bsz, 3, D_MODEL)

    yconv, sz, q, k, v = _in_proj(
        x2d, mod3, norm_g.reshape(1, D_MODEL), win, conv_w, q_a_g.reshape(1, Q_LORA), wqb,
        kv_a_g.reshape(1, KV_LORA), wkvb, gq, gk, cos_t, sin_t, bsz, seq)
    yattn = _attention(q, k, v, sz)
    return _out_proj(yconv, yattn, x2d, mod3, w_out.astype(_BF16), seq)


def kernel(x, c, positions, ada_w, ada_b, norm_g, w_in, conv_w, q_a_g, w_q_b, kv_a_g, w_kv_b, q_g, k_g, w_out):
    bsz, seq, _ = x.shape
    depth = ada_w.shape[0]
    cos_t, sin_t = _rope_tables(positions)
    x2d = x.reshape(bsz * seq, D_MODEL)
    for l in range(depth):
        mod = _adaln_mod(c, ada_w[l], ada_b[l])
        x2d = _layer(x2d, mod, cos_t, sin_t, norm_g[l], w_in[l], conv_w[l], q_a_g[l], w_q_b[l],
                     kv_a_g[l], w_kv_b[l], q_g[l], k_g[l], w_out[l], bsz, seq)
    return x2d.reshape(bsz, seq, D_MODEL)
```

```python
import functools
import math

import jax
import jax.numpy as jnp
from jax import lax
from jax.experimental import pallas as pl
from jax.experimental.pallas import tpu as pltpu

D_MODEL = 2048
D_CONV = 1024
CONV_WIDTH = 3
N_HEADS = 8
QK_NOPE = 128
QK_ROPE = 64
QK_HEAD = QK_NOPE + QK_ROPE
V_HEAD = 128
D_ATTN = N_HEADS * V_HEAD
Q_LORA = 512
KV_LORA = 256
ROPE_BASE = 10000.0
D_MIX = D_CONV + D_ATTN
EPS = 1e-6

LANES = 128
SUBLANES = 8
ROPE_SLAB = LANES
HEAD_SLAB = QK_NOPE + ROPE_SLAB
HALF = QK_ROPE // 2
SHIFT_LANE = HALF
EXP2_NORMAL_RANGE = 120.0

OFF_XC = 0
OFF_BC = D_CONV
OFF_CC = 2 * D_CONV
OFF_ZC = 3 * D_CONV
OFF_CQ = 4 * D_CONV
OFF_CKV = OFF_CQ + Q_LORA
OFF_KR = OFF_CKV + KV_LORA
OFF_ZA = OFF_KR + ROPE_SLAB
IN_COLS_PAD = OFF_ZA + D_ATTN

ADA_TN = 512
ROPE_TT = 1024
PROJ_TM = 256
CONV_CW = 256
ATTN_TQ = 1024
ATTN_TK = 512
OUT_TM = 512
VMEM_LIMIT = 56 * 1024 * 1024

_BF16 = jnp.bfloat16
_F32 = jnp.float32


def _sigmoid(z):
    return 1.0 / (1.0 + jnp.exp(-z))


def _dot(a, b):
    return jnp.dot(a, b, preferred_element_type=_F32)


def _adaln_body(ct_ref, w_ref, b_ref, o_ref):
    ct = ct_ref[...]
    sc = ct * _sigmoid(ct)
    w = w_ref[...]
    rows = [jnp.sum(w * sc[:, b:b + 1], axis=0, keepdims=True) for b in range(ct.shape[1])]
    o_ref[...] = jnp.concatenate(rows, axis=0) + b_ref[...]


def _adaln_mod(c, ada_w, ada_b):
    bsz = c.shape[0]
    n = ada_w.shape[1]
    return pl.pallas_call(
        _adaln_body,
        grid=(n // ADA_TN,),
        in_specs=[
            pl.BlockSpec((D_MODEL, bsz), lambda j: (0, 0)),
            pl.BlockSpec((D_MODEL, ADA_TN), lambda j: (0, j)),
            pl.BlockSpec((1, ADA_TN), lambda j: (0, j)),
        ],
        out_specs=pl.BlockSpec((bsz, ADA_TN), lambda j: (0, j)),
        out_shape=jax.ShapeDtypeStruct((bsz, n), _F32),
        name="adaln_mod",
    )(c.T, ada_w, ada_b.reshape(1, n))


def _rope_body(pos_ref, f_ref, cm_ref, sm_ref, cos_ref, sin_ref):
    ang = f_ref[...] * pos_ref[...].astype(_F32)
    cos_ref[...] = (jnp.cos(ang) * cm_ref[...]).T
    sin_ref[...] = (jnp.sin(ang) * sm_ref[...]).T


def _rope_tables(positions):
    t = positions.size
    inv_freq = ROPE_BASE ** (-jnp.arange(0, QK_ROPE, 2, dtype=_F32) / QK_ROPE)
    zeros = jnp.zeros((HALF,), _F32)
    ones = jnp.ones((HALF,), _F32)
    f_col = jnp.concatenate([inv_freq, zeros, inv_freq, zeros]).reshape(ROPE_SLAB, 1)
    cmask = jnp.concatenate([ones, zeros, ones, zeros]).reshape(ROPE_SLAB, 1)
    smask = jnp.concatenate([-ones, zeros, ones, zeros]).reshape(ROPE_SLAB, 1)
    col = pl.BlockSpec((ROPE_SLAB, 1), lambda i: (0, 0))
    out = pl.BlockSpec((ROPE_TT, ROPE_SLAB), lambda i: (i, 0))
    return pl.pallas_call(
        _rope_body,
        grid=(t // ROPE_TT,),
        in_specs=[pl.BlockSpec((1, ROPE_TT), lambda i: (0, i)), col, col, col],
        out_specs=[out, out],
        out_shape=[jax.ShapeDtypeStruct((t, ROPE_SLAB), _F32)] * 2,
        name="rope_tables",
    )(positions.reshape(1, t), f_col, cmask, smask)


def _in_proj_body(tiles_per_batch,
                  x_ref, mod_ref, ng_ref, win_ref, convw_ref, qag_ref, wqb_ref, kvag_ref, wkvb_ref,
                  gq_ref, gk_ref, qb_ref, kb_ref, cos_ref, sin_ref,
                  yconv_ref, sz_ref, q_ref, k_ref, v_ref,
                  h_scr, carry_scr, vbuf_scr):
    tm = x_ref.shape[0]

    @pl.when(pl.program_id(0) % tiles_per_batch == 0)
    def _():
        carry_scr[...] = jnp.zeros_like(carry_scr)

    x = x_ref[...]
    xn = x * lax.rsqrt(jnp.mean(x * x, axis=-1, keepdims=True) + EPS)
    a = ng_ref[...] * (1.0 + mod_ref[0, 1:2, :])
    h_scr[...] = (xn * a + mod_ref[0, 0:1, :]).astype(_BF16)

    def proj(off, width):
        return _dot(h_scr[...], win_ref[:, off:off + width])

    for lo in range(0, D_CONV, CONV_CW):
        xc = proj(OFF_XC + lo, CONV_CW)
        bc = proj(OFF_BC + lo, CONV_CW)
        cc = proj(OFF_CC + lo, CONV_CW)
        zc = proj(OFF_ZC + lo, CONV_CW)
        v0 = cc * xc
        vbuf_scr[0:SUBLANES, :] = carry_scr[:, lo:lo + CONV_CW]
        vbuf_scr[SUBLANES:SUBLANES + tm, :] = v0
        carry_scr[:, lo:lo + CONV_CW] = v0[tm - SUBLANES:, :]
        v1 = vbuf_scr[SUBLANES - 1:SUBLANES - 1 + tm, :]
        v2 = vbuf_scr[SUBLANES - 2:SUBLANES - 2 + tm, :]
        w = convw_ref[:, lo:lo + CONV_CW]
        conv = w[0:1, :] * v2 + w[1:2, :] * v1 + w[2:3, :] * v0
        yconv_ref[:, lo:lo + CONV_CW] = (bc * conv * (zc * _sigmoid(zc))).astype(_BF16)

    za = proj(OFF_ZA, D_ATTN)
    sz_ref[...] = (za * _sigmoid(za)).astype(_BF16)

    cos_t = cos_ref[...]
    sin_t = sin_ref[...]

    def rope(slab):
        return slab * cos_t + pltpu.roll(slab, ROPE_SLAB // 2, axis=1) * sin_t

    cq = proj(OFF_CQ, Q_LORA)
    cqn = cq * lax.rsqrt(jnp.mean(cq * cq, axis=-1, keepdims=True) + EPS) * qag_ref[...]
    qf = _dot(cqn.astype(_BF16), wqb_ref[...])
    gq = gq_ref[...]
    for hd in range(N_HEADS):
        slab = qf[:, hd * HEAD_SLAB:(hd + 1) * HEAD_SLAB]
        r = lax.rsqrt(jnp.sum(slab * slab, axis=-1, keepdims=True) * (1.0 / QK_HEAD) + EPS)
        qn = slab * r * gq
        q_ref[0, hd, :, 0:QK_NOPE] = qn[:, 0:QK_NOPE].astype(_BF16)
        q_ref[0, hd, :, QK_NOPE:HEAD_SLAB] = (rope(qn[:, QK_NOPE:HEAD_SLAB]) + qb_ref[...]).astype(_BF16)

    ckv = proj(OFF_CKV, KV_LORA)
    ckvn = ckv * lax.rsqrt(jnp.mean(ckv * ckv, axis=-1, keepdims=True) + EPS) * kvag_ref[...]
    kvf = _dot(ckvn.astype(_BF16), wkvb_ref[...])
    kr = proj(OFF_KR, ROPE_SLAB)
    ss_r = jnp.sum(kr * kr, axis=-1, keepdims=True)
    gk = gk_ref[...]
    kr_rot = rope(kr * gk[:, QK_NOPE:HEAD_SLAB])
    for hd in range(N_HEADS):
        base = hd * (QK_NOPE + V_HEAD)
        kn = kvf[:, base:base + QK_NOPE]
        r = lax.rsqrt((jnp.sum(kn * kn, axis=-1, keepdims=True) + ss_r) * (1.0 / QK_HEAD) + EPS)
        k_ref[0, hd, :, 0:QK_NOPE] = (kn * r * gk[:, 0:QK_NOPE]).astype(_BF16)
        k_ref[0, hd, :, QK_NOPE:HEAD_SLAB] = (kr_rot * r + kb_ref[...]).astype(_BF16)
        v_ref[0, hd, :, :] = kvf[:, base + QK_NOPE:base + QK_NOPE + V_HEAD].astype(_BF16)


def _resident(shape):
    return pl.BlockSpec(shape, lambda i: (0,) * len(shape), pipeline_mode=pl.Buffered(1))


def _in_proj(x2d, mod3, norm_g, win, conv_w, q_a_g, wqb, kv_a_g, wkvb, gq, gk, qb, kb, cos_t, sin_t, bsz, seq):
    t = x2d.shape[0]
    tm = PROJ_TM
    tpb = seq // tm
    tok = lambda width: pl.BlockSpec((tm, width), lambda i: (i, 0))
    head = lambda width: pl.BlockSpec((1, N_HEADS, tm, width), lambda i: (i // tpb, 0, i % tpb, 0))
    return pl.pallas_call(
        functools.partial(_in_proj_body, tpb),
        grid=(t // tm,),
        in_specs=[
            tok(D_MODEL),
            pl.BlockSpec((1, 3, D_MODEL), lambda i: (i // tpb, 0, 0)),
            _resident((1, D_MODEL)),
            _resident((D_MODEL, IN_COLS_PAD)),
            _resident((CONV_WIDTH, D_CONV)),
            _resident((1, Q_LORA)),
            _resident((Q_LORA, N_HEADS * HEAD_SLAB)),
            _resident((1, KV_LORA)),
            _resident((KV_LORA, N_HEADS * (QK_NOPE + V_HEAD))),
            _resident((1, HEAD_SLAB)),
            _resident((1, HEAD_SLAB)),
            _resident((1, ROPE_SLAB)),
            _resident((1, ROPE_SLAB)),
            tok(ROPE_SLAB),
            tok(ROPE_SLAB),
        ],
        out_specs=[tok(D_CONV), tok(D_ATTN), head(HEAD_SLAB), head(HEAD_SLAB), head(V_HEAD)],
        out_shape=[
            jax.ShapeDtypeStruct((t, D_CONV), _BF16),
            jax.ShapeDtypeStruct((t, D_ATTN), _BF16),
            jax.ShapeDtypeStruct((bsz, N_HEADS, seq, HEAD_SLAB), _BF16),
            jax.ShapeDtypeStruct((bsz, N_HEADS, seq, HEAD_SLAB), _BF16),
            jax.ShapeDtypeStruct((bsz, N_HEADS, seq, V_HEAD), _BF16),
        ],
        scratch_shapes=[
            pltpu.VMEM((tm, D_MODEL), _BF16),
            pltpu.VMEM((SUBLANES, D_CONV), _F32),
            pltpu.VMEM((SUBLANES + tm, CONV_CW), _F32),
        ],
        compiler_params=pltpu.CompilerParams(
            dimension_semantics=("arbitrary",), vmem_limit_bytes=VMEM_LIMIT),
        name="in_proj",
    )(x2d, mod3, norm_g, win, conv_w, q_a_g, wqb, kv_a_g, wkvb, gq, gk, qb, kb, cos_t, sin_t)


def _attn_body(online_max, q_ref, k_ref, v_ref, sz_ref, o_ref, sa_scr, sb_scr, m_scr, l_scr, acc_scr):
    tq = q_ref.shape[2]
    tk = ATTN_TK
    half = tq // 2
    assert tq == 2 * tk
    qi = pl.program_id(2)

    if online_max:
        m_scr[...] = jnp.full_like(m_scr, -jnp.inf)
    l_scr[...] = jnp.zeros_like(l_scr)
    acc_scr[...] = jnp.zeros_like(acc_scr)

    def scores(j, r0, nr):
        start = pl.multiple_of(j * tk, tk)
        k = k_ref[0, 0, pl.ds(start, tk), :]
        return lax.dot_general(q_ref[0, 0, r0:r0 + nr, :], k, (((1,), (1,)), ((), ())),
                               preferred_element_type=_F32)

    def update(j, s, r0, nr, triangular):
        start = pl.multiple_of(j * tk, tk)
        v = v_ref[0, 0, pl.ds(start, tk), :]
        if triangular:
            row = lax.broadcasted_iota(jnp.int32, (nr, tk), 0)
            col = lax.broadcasted_iota(jnp.int32, (nr, tk), 1)
            s = jnp.where(col <= row, s, -jnp.inf)
        tiles = [s[:, t * LANES:(t + 1) * LANES] for t in range(tk // LANES)]
        if not online_max:
            ps = [jnp.exp2(t) for t in tiles]
            l_scr[r0:r0 + nr, :] += functools.reduce(jnp.add, ps)
            p = jnp.concatenate(ps, axis=-1).astype(_BF16)
            acc_scr[r0:r0 + nr, :] += _dot(p, v)
            return
        mx = functools.reduce(jnp.maximum, tiles)
        m_prev = m_scr[r0:r0 + nr, :]
        m_new = jnp.maximum(m_prev, jnp.max(mx, axis=-1, keepdims=True))
        alpha = jnp.exp2(m_prev - m_new)
        ps = [jnp.exp2(t - m_new) for t in tiles]
        l_scr[r0:r0 + nr, :] = alpha * l_scr[r0:r0 + nr, :] + functools.reduce(jnp.add, ps)
        p = jnp.concatenate(ps, axis=-1).astype(_BF16)
        acc_scr[r0:r0 + nr, :] = alpha * acc_scr[r0:r0 + nr, :] + _dot(p, v)
        m_scr[r0:r0 + nr, :] = m_new

    n_pairs = qi
    sa_scr[...] = scores(0, 0, tq)

    def pair(jj, carry):
        j = 2 * jj
        sb_scr[...] = scores(j + 1, 0, tq)
        update(j, sa_scr[...], 0, tq, False)
        sa_scr[...] = scores(j + 2, 0, tq)
        update(j + 1, sb_scr[...], 0, tq, False)
        return carry

    lax.fori_loop(0, n_pairs, pair, 0)

    jd = 2 * n_pairs
    sb_scr[half:tq, :] = scores(jd + 1, half, half)
    update(jd, sa_scr[0:half, :], 0, half, True)
    update(jd, sa_scr[half:tq, :], half, half, False)
    update(jd + 1, sb_scr[half:tq, :], half, half, True)

    o = acc_scr[...] / jnp.sum(l_scr[...], axis=-1, keepdims=True)
    o_ref[...] = (o * sz_ref[...].astype(_F32)).astype(_BF16)


def _attention(online_max, q, k, v, sz):
    bsz, _, seq, _ = q.shape
    tq = ATTN_TQ
    nq = seq // tq
    return pl.pallas_call(
        functools.partial(_attn_body, online_max),
        grid=(bsz, N_HEADS, nq),
        in_specs=[
            pl.BlockSpec((1, 1, tq, HEAD_SLAB), lambda b, h, i: (b, h, i, 0)),
            pl.BlockSpec((1, 1, seq, HEAD_SLAB), lambda b, h, i: (b, h, 0, 0)),
            pl.BlockSpec((1, 1, seq, V_HEAD), lambda b, h, i: (b, h, 0, 0)),
            pl.BlockSpec((tq, V_HEAD), lambda b, h, i: (b * nq + i, h)),
        ],
        out_specs=pl.BlockSpec((tq, V_HEAD), lambda b, h, i: (b * nq + i, h)),
        out_shape=jax.ShapeDtypeStruct((bsz * seq, D_ATTN), _BF16),
        scratch_shapes=[
            pltpu.VMEM((tq, ATTN_TK), _F32),
            pltpu.VMEM((tq, ATTN_TK), _F32),
            pltpu.VMEM((tq, LANES), _F32),
            pltpu.VMEM((tq, LANES), _F32),
            pltpu.VMEM((tq, V_HEAD), _F32),
        ],
        compiler_params=pltpu.CompilerParams(
            dimension_semantics=("arbitrary", "arbitrary", "arbitrary"), vmem_limit_bytes=VMEM_LIMIT),
        name="attention_online_max" if online_max else "attention_shifted",
    )(q, k, v, sz)


def _out_proj_body(yc_ref, ya_ref, x_ref, mod_ref, w_ref, o_ref):
    y = _dot(yc_ref[...], w_ref[0:D_CONV, :]) + _dot(ya_ref[...], w_ref[D_CONV:D_MIX, :])
    o_ref[...] = x_ref[...] + mod_ref[0, 2:3, :] * y


def _out_proj(yconv, yattn, x2d, mod3, wout, seq):
    t = x2d.shape[0]
    tm = OUT_TM
    tpb = seq // tm
    return pl.pallas_call(
        _out_proj_body,
        grid=(t // tm,),
        in_specs=[
            pl.BlockSpec((tm, D_CONV), lambda i: (i, 0)),
            pl.BlockSpec((tm, D_ATTN), lambda i: (i, 0)),
            pl.BlockSpec((tm, D_MODEL), lambda i: (i, 0)),
            pl.BlockSpec((1, 3, D_MODEL), lambda i: (i // tpb, 0, 0)),
            _resident((D_MIX, D_MODEL)),
        ],
        out_specs=pl.BlockSpec((tm, D_MODEL), lambda i: (i, 0)),
        out_shape=jax.ShapeDtypeStruct((t, D_MODEL), _F32),
        compiler_params=pltpu.CompilerParams(
            dimension_semantics=("arbitrary",), vmem_limit_bytes=VMEM_LIMIT),
        name="out_proj",
    )(yconv, yattn, x2d, mod3, wout)


def _score_shift(q_g, k_g, qscale):
    bound = QK_HEAD * qscale * jnp.max(jnp.abs(q_g)) * jnp.max(jnp.abs(k_g))
    shift = (bound * 1.03 + 1.0).astype(_BF16).astype(_F32)
    use_shift = bound + shift <= EXP2_NORMAL_RANGE
    return jnp.where(use_shift, shift, 0.0), use_shift


def _rope_slab_cols(w):
    z = jnp.zeros(w.shape[:-1] + (HALF,), w.dtype)
    return jnp.concatenate([w[..., :HALF], z, w[..., HALF:], z], axis=-1)


def _layer(x2d, mod, cos_t, sin_t, norm_g, w_in, conv_w, q_a_g, w_q_b, kv_a_g, w_kv_b, q_g, k_g, w_out, bsz, seq):
    off_kr = 4 * D_CONV + Q_LORA + KV_LORA
    win = jnp.concatenate(
        [w_in[:, :off_kr], _rope_slab_cols(w_in[:, off_kr:off_kr + QK_ROPE]), w_in[:, off_kr + QK_ROPE:]],
        axis=-1).astype(_BF16)
    wq = w_q_b.reshape(Q_LORA, N_HEADS, QK_HEAD)
    wqb = jnp.concatenate([wq[..., :QK_NOPE], _rope_slab_cols(wq[..., QK_NOPE:])], axis=-1)
    wqb = wqb.reshape(Q_LORA, N_HEADS * HEAD_SLAB).astype(_BF16)
    wkvb = w_kv_b.astype(_BF16)
    qscale = math.log2(math.e) / math.sqrt(QK_HEAD)
    gq = jnp.concatenate([q_g[:QK_NOPE], _rope_slab_cols(q_g[QK_NOPE:])]).reshape(1, HEAD_SLAB) * qscale
    gk = jnp.concatenate([k_g[:QK_NOPE], _rope_slab_cols(k_g[QK_NOPE:])]).reshape(1, HEAD_SLAB)
    mod3 = mod.reshape(bsz, 3, D_MODEL)
    shift, use_shift = _score_shift(q_g, k_g, qscale)
    pad_lane = jnp.arange(ROPE_SLAB) == SHIFT_LANE
    qb = jnp.where(pad_lane, -shift, 0.0).reshape(1, ROPE_SLAB)
    kb = jnp.where(pad_lane, 1.0, 0.0).reshape(1, ROPE_SLAB)

    yconv, sz, q, k, v = _in_proj(
        x2d, mod3, norm_g.reshape(1, D_MODEL), win, conv_w, q_a_g.reshape(1, Q_LORA), wqb,
        kv_a_g.reshape(1, KV_LORA), wkvb, gq, gk, qb, kb, cos_t, sin_t, bsz, seq)
    yattn = lax.cond(use_shift, functools.partial(_attention, False), functools.partial(_attention, True),
                     q, k, v, sz)
    return _out_proj(yconv, yattn, x2d, mod3, w_out.astype(_BF16), seq)


def kernel(x, c, positions, ada_w, ada_b, norm_g, w_in, conv_w, q_a_g, w_q_b, kv_a_g, w_kv_b, q_g, k_g, w_out):
    bsz, seq, _ = x.shape
    depth = ada_w.shape[0]
    cos_t, sin_t = _rope_tables(positions)
    x2d = x.reshape(bsz * seq, D_MODEL)
    for l in range(depth):
        mod = _adaln_mod(c, ada_w[l], ada_b[l])
        x2d = _layer(x2d, mod, cos_t, sin_t, norm_g[l], w_in[l], conv_w[l], q_a_g[l], w_q_b[l],
                     kv_a_g[l], w_kv_b[l], q_g[l], k_g[l], w_out[l], bsz, seq)
    return x2d.reshape(bsz, seq, D_MODEL)
```

```python
import functools
import math

import jax
import jax.numpy as jnp
from jax import lax
from jax.experimental import pallas as pl
from jax.experimental.pallas import tpu as pltpu

D_MODEL = 2048
D_CONV = 1024
CONV_WIDTH = 3
N_HEADS = 8
QK_NOPE = 128
QK_ROPE = 64
QK_HEAD = QK_NOPE + QK_ROPE
V_HEAD = 128
D_ATTN = N_HEADS * V_HEAD
Q_LORA = 512
KV_LORA = 256
ROPE_BASE = 10000.0
D_MIX = D_CONV + D_ATTN
EPS = 1e-6

KV_HEAD = QK_NOPE + V_HEAD
MXU_N = 256
LANES = 128
SUBLANES = 8
ROPE_SLAB = LANES
HEAD_SLAB = QK_NOPE + ROPE_SLAB
HALF = QK_ROPE // 2
SHIFT_LANE = HALF
EXP2_NORMAL_RANGE = 120.0

OFF_XC = 0
OFF_BC = D_CONV
OFF_CC = 2 * D_CONV
OFF_ZC = 3 * D_CONV
OFF_CQ = 4 * D_CONV
OFF_CKV = OFF_CQ + Q_LORA
OFF_KR = OFF_CKV + KV_LORA
OFF_ZA = OFF_KR + QK_ROPE
TAIL_COLS = ROPE_SLAB + D_ATTN

ADA_TN = 512
ROPE_TT = 1024
CONV_TM = 512
MLA_TM = 512
CONV_CW = 256
ATTN_TQ = 1024
ATTN_TK = 512
OUT_TM = 512
VMEM_LIMIT = 56 * 1024 * 1024

_BF16 = jnp.bfloat16
_F32 = jnp.float32


def _sigmoid(z):
    return 1.0 / (1.0 + jnp.exp(-z))


def _dot(a, b):
    return jnp.dot(a, b, preferred_element_type=_F32)


def _adaln_body(ct_ref, w_ref, b_ref, o_ref):
    ct = ct_ref[...]
    sc = ct * _sigmoid(ct)
    w = w_ref[...]
    rows = [jnp.sum(w * sc[:, b:b + 1], axis=0, keepdims=True) for b in range(ct.shape[1])]
    o_ref[...] = jnp.concatenate(rows, axis=0) + b_ref[...]


def _adaln_mod(c, ada_w, ada_b):
    bsz = c.shape[0]
    n = ada_w.shape[1]
    return pl.pallas_call(
        _adaln_body,
        grid=(n // ADA_TN,),
        in_specs=[
            pl.BlockSpec((D_MODEL, bsz), lambda j: (0, 0)),
            pl.BlockSpec((D_MODEL, ADA_TN), lambda j: (0, j)),
            pl.BlockSpec((1, ADA_TN), lambda j: (0, j)),
        ],
        out_specs=pl.BlockSpec((bsz, ADA_TN), lambda j: (0, j)),
        out_shape=jax.ShapeDtypeStruct((bsz, n), _F32),
        name="adaln_mod",
    )(c.T, ada_w, ada_b.reshape(1, n))


def _rope_body(pos_ref, f_ref, cm_ref, sm_ref, cos_ref, sin_ref):
    ang = f_ref[...] * pos_ref[...].astype(_F32)
    cos_ref[...] = (jnp.cos(ang) * cm_ref[...]).T
    sin_ref[...] = (jnp.sin(ang) * sm_ref[...]).T


def _rope_tables(positions):
    t = positions.size
    inv_freq = ROPE_BASE ** (-jnp.arange(0, QK_ROPE, 2, dtype=_F32) / QK_ROPE)
    zeros = jnp.zeros((HALF,), _F32)
    ones = jnp.ones((HALF,), _F32)
    f_col = jnp.concatenate([inv_freq, zeros, inv_freq, zeros]).reshape(ROPE_SLAB, 1)
    cmask = jnp.concatenate([ones, zeros, ones, zeros]).reshape(ROPE_SLAB, 1)
    smask = jnp.concatenate([-ones, zeros, ones, zeros]).reshape(ROPE_SLAB, 1)
    col = pl.BlockSpec((ROPE_SLAB, 1), lambda i: (0, 0))
    out = pl.BlockSpec((ROPE_TT, ROPE_SLAB), lambda i: (i, 0))
    return pl.pallas_call(
        _rope_body,
        grid=(t // ROPE_TT,),
        in_specs=[pl.BlockSpec((1, ROPE_TT), lambda i: (0, i)), col, col, col],
        out_specs=[out, out],
        out_shape=[jax.ShapeDtypeStruct((t, ROPE_SLAB), _F32)] * 2,
        name="rope_tables",
    )(positions.reshape(1, t), f_col, cmask, smask)


def _modulated_norm(x_ref, mod_ref, ng_ref):
    x = x_ref[...]
    xn = x * lax.rsqrt(jnp.mean(x * x, axis=-1, keepdims=True) + EPS)
    a = ng_ref[...] * (1.0 + mod_ref[0, 1:2, :])
    return (xn * a + mod_ref[0, 0:1, :]).astype(_BF16)


def _conv_proj_body(tiles_per_batch, x_ref, mod_ref, ng_ref, win_ref, convw_ref, yconv_ref,
                    h_scr, carry_scr, vbuf_scr):
    tm = x_ref.shape[0]

    @pl.when(pl.program_id(0) % tiles_per_batch == 0)
    def _():
        carry_scr[...] = jnp.zeros_like(carry_scr)

    h_scr[...] = _modulated_norm(x_ref, mod_ref, ng_ref)

    def proj(off, width):
        return _dot(h_scr[...], win_ref[:, off:off + width])

    for lo in range(0, D_CONV, CONV_CW):
        xc = proj(OFF_XC + lo, CONV_CW)
        bc = proj(OFF_BC + lo, CONV_CW)
        cc = proj(OFF_CC + lo, CONV_CW)
        zc = proj(OFF_ZC + lo, CONV_CW)
        v0 = cc * xc
        vbuf_scr[0:SUBLANES, :] = carry_scr[:, lo:lo + CONV_CW]
        vbuf_scr[SUBLANES:SUBLANES + tm, :] = v0
        carry_scr[:, lo:lo + CONV_CW] = v0[tm - SUBLANES:, :]
        v1 = vbuf_scr[SUBLANES - 1:SUBLANES - 1 + tm, :]
        v2 = vbuf_scr[SUBLANES - 2:SUBLANES - 2 + tm, :]
        w = convw_ref[:, lo:lo + CONV_CW]
        conv = w[0:1, :] * v2 + w[1:2, :] * v1 + w[2:3, :] * v0
        yconv_ref[:, lo:lo + CONV_CW] = (bc * conv * (zc * _sigmoid(zc))).astype(_BF16)


def _resident(shape, index=None):
    index = (0,) * len(shape) if index is None else index
    return pl.BlockSpec(shape, lambda i: index, pipeline_mode=pl.Buffered(1))


def _conv_proj(layer, x2d, mod3, norm_g, win_all, conv_w, seq):
    t = x2d.shape[0]
    tm = CONV_TM
    tpb = seq // tm
    return pl.pallas_call(
        functools.partial(_conv_proj_body, tpb),
        grid=(t // tm,),
        in_specs=[
            pl.BlockSpec((tm, D_MODEL), lambda i: (i, 0)),
            pl.BlockSpec((1, 3, D_MODEL), lambda i: (i // tpb, 0, 0)),
            _resident((1, D_MODEL)),
            _resident((None, D_MODEL, 4 * D_CONV), (layer, 0, 0)),
            _resident((CONV_WIDTH, D_CONV)),
        ],
        out_specs=pl.BlockSpec((tm, D_CONV), lambda i: (i, 0)),
        out_shape=jax.ShapeDtypeStruct((t, D_CONV), _BF16),
        scratch_shapes=[
            pltpu.VMEM((tm, D_MODEL), _BF16),
            pltpu.VMEM((SUBLANES, D_CONV), _F32),
            pltpu.VMEM((SUBLANES + tm, CONV_CW), _F32),
        ],
        compiler_params=pltpu.CompilerParams(
            dimension_semantics=("arbitrary",), vmem_limit_bytes=VMEM_LIMIT),
        name="conv_proj",
    )(x2d, mod3, norm_g, win_all, conv_w)


def _mla_proj_body(x_ref, mod_ref, ng_ref, wcq_ref, wckv_ref, wtail_ref, qag_ref, wqb_ref, kvag_ref, wkvb_ref,
                   gq_ref, gk_ref, qb_ref, kb_ref, cos_ref, sin_ref,
                   sz_ref, q_ref, k_ref, v_ref, h_scr, cqn_scr, ckvn_scr):
    h_scr[...] = _modulated_norm(x_ref, mod_ref, ng_ref)
    cos_t = cos_ref[...]
    sin_t = sin_ref[...]

    def rope(slab):
        return slab * cos_t + pltpu.roll(slab, ROPE_SLAB // 2, axis=1) * sin_t

    def rms(parts, width):
        ss = sum(jnp.sum(p * p, axis=-1, keepdims=True) for p in parts)
        return lax.rsqrt(ss * (1.0 / width) + EPS)

    cq = [_dot(h_scr[...], wcq_ref[:, lo:lo + MXU_N]) for lo in range(0, Q_LORA, MXU_N)]
    r = rms(cq, Q_LORA)
    for i, part in enumerate(cq):
        lo = i * MXU_N
        cqn_scr[:, lo:lo + MXU_N] = (part * r * qag_ref[:, lo:lo + MXU_N]).astype(_BF16)
    ckv = _dot(h_scr[...], wckv_ref[...])
    ckvn_scr[...] = (ckv * rms([ckv], KV_LORA) * kvag_ref[...]).astype(_BF16)

    kr = _dot(h_scr[...], wtail_ref[:, 0:ROPE_SLAB])
    gk = gk_ref[...]
    gq = gq_ref[...]
    kr_rot = rope(kr * gk[:, QK_NOPE:HEAD_SLAB])

    def lane_sum(sq):
        ones = jnp.ones((sq.shape[1], LANES), _BF16)
        return _dot(sq.astype(_BF16), ones)

    ss_r = lane_sum(kr * kr)
    for hd in range(N_HEADS):
        lo = hd * HEAD_SLAB
        slab = _dot(cqn_scr[...], wqb_ref[:, lo:lo + HEAD_SLAB])
        r = lax.rsqrt(lane_sum(slab * slab) * (1.0 / QK_HEAD) + EPS)
        q_ref[0, hd, :, 0:QK_NOPE] = (slab[:, 0:QK_NOPE] * r * gq[:, 0:QK_NOPE]).astype(_BF16)
        q_rope = rope(slab[:, QK_NOPE:HEAD_SLAB] * r * gq[:, QK_NOPE:HEAD_SLAB])
        q_ref[0, hd, :, QK_NOPE:HEAD_SLAB] = (q_rope + qb_ref[...]).astype(_BF16)

        kv = _dot(ckvn_scr[...], wkvb_ref[:, hd * KV_HEAD:(hd + 1) * KV_HEAD])
        kn = kv[:, 0:QK_NOPE]
        r = lax.rsqrt((lane_sum(kn * kn) + ss_r) * (1.0 / QK_HEAD) + EPS)
        k_ref[0, hd, :, 0:QK_NOPE] = (kn * r * gk[:, 0:QK_NOPE]).astype(_BF16)
        k_ref[0, hd, :, QK_NOPE:HEAD_SLAB] = (kr_rot * r + kb_ref[...]).astype(_BF16)
        v_ref[0, hd, :, :] = kv[:, QK_NOPE:KV_HEAD].astype(_BF16)

    for lo in range(0, D_ATTN, MXU_N):
        za = _dot(h_scr[...], wtail_ref[:, ROPE_SLAB + lo:ROPE_SLAB + lo + MXU_N])
        sz_ref[:, lo:lo + MXU_N] = (za * _sigmoid(za)).astype(_BF16)


def _mla_proj(layer, x2d, mod3, norm_g, win_all, wtail, q_a_g, wqb, kv_a_g, wkvb, gq, gk, qb, kb, cos_t, sin_t,
              bsz, seq):
    t = x2d.shape[0]
    tm = MLA_TM
    tpb = seq // tm
    tok = lambda width: pl.BlockSpec((tm, width), lambda i: (i, 0))
    head = lambda width: pl.BlockSpec((1, N_HEADS, tm, width), lambda i: (i // tpb, 0, i % tpb, 0))
    return pl.pallas_call(
        _mla_proj_body,
        grid=(t // tm,),
        in_specs=[
            tok(D_MODEL),
            pl.BlockSpec((1, 3, D_MODEL), lambda i: (i // tpb, 0, 0)),
            _resident((1, D_MODEL)),
            _resident((None, D_MODEL, Q_LORA), (layer, 0, OFF_CQ // Q_LORA)),
            _resident((None, D_MODEL, KV_LORA), (layer, 0, OFF_CKV // KV_LORA)),
            _resident((D_MODEL, TAIL_COLS)),
            _resident((1, Q_LORA)),
            _resident((Q_LORA, N_HEADS * HEAD_SLAB)),
            _resident((1, KV_LORA)),
            _resident((KV_LORA, N_HEADS * (QK_NOPE + V_HEAD))),
            _resident((1, HEAD_SLAB)),
            _resident((1, HEAD_SLAB)),
            _resident((1, ROPE_SLAB)),
            _resident((1, ROPE_SLAB)),
            tok(ROPE_SLAB),
            tok(ROPE_SLAB),
        ],
        out_specs=[tok(D_ATTN), head(HEAD_SLAB), head(HEAD_SLAB), head(V_HEAD)],
        out_shape=[
            jax.ShapeDtypeStruct((t, D_ATTN), _BF16),
            jax.ShapeDtypeStruct((bsz, N_HEADS, seq, HEAD_SLAB), _BF16),
            jax.ShapeDtypeStruct((bsz, N_HEADS, seq, HEAD_SLAB), _BF16),
            jax.ShapeDtypeStruct((bsz, N_HEADS, seq, V_HEAD), _BF16),
        ],
        scratch_shapes=[
            pltpu.VMEM((tm, D_MODEL), _BF16),
            pltpu.VMEM((tm, Q_LORA), _BF16),
            pltpu.VMEM((tm, KV_LORA), _BF16),
        ],
        compiler_params=pltpu.CompilerParams(
            dimension_semantics=("arbitrary",), vmem_limit_bytes=VMEM_LIMIT),
        name="mla_proj",
    )(x2d, mod3, norm_g, win_all, win_all, wtail, q_a_g, wqb, kv_a_g, wkvb, gq, gk, qb, kb, cos_t, sin_t)


def _attn_body(online_max, q_ref, k_ref, v_ref, sz_ref, o_ref, sa_scr, sb_scr, m_scr, l_scr, acc_scr):
    tq = q_ref.shape[2]
    tk = ATTN_TK
    half = tq // 2
    assert tq == 2 * tk
    qi = pl.program_id(2)

    if online_max:
        m_scr[...] = jnp.full_like(m_scr, -jnp.inf)
    l_scr[...] = jnp.zeros_like(l_scr)
    acc_scr[...] = jnp.zeros_like(acc_scr)

    def scores(j, r0, nr):
        start = pl.multiple_of(j * tk, tk)
        k = k_ref[0, 0, pl.ds(start, tk), :]
        return lax.dot_general(q_ref[0, 0, r0:r0 + nr, :], k, (((1,), (1,)), ((), ())),
                               preferred_element_type=_F32)

    def update(j, s, r0, nr, triangular):
        start = pl.multiple_of(j * tk, tk)
        v = v_ref[0, 0, pl.ds(start, tk), :]
        if triangular:
            row = lax.broadcasted_iota(jnp.int32, (nr, tk), 0)
            col = lax.broadcasted_iota(jnp.int32, (nr, tk), 1)
            s = jnp.where(col <= row, s, -jnp.inf)
        tiles = [s[:, t * LANES:(t + 1) * LANES] for t in range(tk // LANES)]
        if not online_max:
            ps = [jnp.exp2(t) for t in tiles]
            l_scr[r0:r0 + nr, :] += functools.reduce(jnp.add, ps)
            p = jnp.concatenate(ps, axis=-1).astype(_BF16)
            acc_scr[r0:r0 + nr, :] += _dot(p, v)
            return
        mx = functools.reduce(jnp.maximum, tiles)
        m_prev = m_scr[r0:r0 + nr, :]
        m_new = jnp.maximum(m_prev, jnp.max(mx, axis=-1, keepdims=True))
        alpha = jnp.exp2(m_prev - m_new)
        ps = [jnp.exp2(t - m_new) for t in tiles]
        l_scr[r0:r0 + nr, :] = alpha * l_scr[r0:r0 + nr, :] + functools.reduce(jnp.add, ps)
        p = jnp.concatenate(ps, axis=-1).astype(_BF16)
        acc_scr[r0:r0 + nr, :] = alpha * acc_scr[r0:r0 + nr, :] + _dot(p, v)
        m_scr[r0:r0 + nr, :] = m_new

    n_pairs = qi
    sa_scr[...] = scores(0, 0, tq)

    def pair(jj, carry):
        j = 2 * jj
        sb_scr[...] = scores(j + 1, 0, tq)
        update(j, sa_scr[...], 0, tq, False)
        sa_scr[...] = scores(j + 2, 0, tq)
        update(j + 1, sb_scr[...], 0, tq, False)
        return carry

    lax.fori_loop(0, n_pairs, pair, 0)

    jd = 2 * n_pairs
    sb_scr[half:tq, :] = scores(jd + 1, half, half)
    update(jd, sa_scr[0:half, :], 0, half, True)
    update(jd, sa_scr[half:tq, :], half, half, False)
    update(jd + 1, sb_scr[half:tq, :], half, half, True)

    o = acc_scr[...] / jnp.sum(l_scr[...], axis=-1, keepdims=True)
    o_ref[...] = (o * sz_ref[...].astype(_F32)).astype(_BF16)


def _attention(online_max, q, k, v, sz):
    bsz, _, seq, _ = q.shape
    tq = ATTN_TQ
    nq = seq // tq
    return pl.pallas_call(
        functools.partial(_attn_body, online_max),
        grid=(bsz, N_HEADS, nq),
        in_specs=[
            pl.BlockSpec((1, 1, tq, HEAD_SLAB), lambda b, h, i: (b, h, i, 0)),
            pl.BlockSpec((1, 1, seq, HEAD_SLAB), lambda b, h, i: (b, h, 0, 0)),
            pl.BlockSpec((1, 1, seq, V_HEAD), lambda b, h, i: (b, h, 0, 0)),
            pl.BlockSpec((tq, V_HEAD), lambda b, h, i: (b * nq + i, h)),
        ],
        out_specs=pl.BlockSpec((tq, V_HEAD), lambda b, h, i: (b * nq + i, h)),
        out_shape=jax.ShapeDtypeStruct((bsz * seq, D_ATTN), _BF16),
        scratch_shapes=[
            pltpu.VMEM((tq, ATTN_TK), _F32),
            pltpu.VMEM((tq, ATTN_TK), _F32),
            pltpu.VMEM((tq, LANES), _F32),
            pltpu.VMEM((tq, LANES), _F32),
            pltpu.VMEM((tq, V_HEAD), _F32),
        ],
        compiler_params=pltpu.CompilerParams(
            dimension_semantics=("arbitrary", "arbitrary", "arbitrary"), vmem_limit_bytes=VMEM_LIMIT),
        name="attention_online_max" if online_max else "attention_shifted",
    )(q, k, v, sz)


def _out_proj_body(yc_ref, ya_ref, x_ref, mod_ref, w_ref, o_ref):
    y = _dot(yc_ref[...], w_ref[0:D_CONV, :]) + _dot(ya_ref[...], w_ref[D_CONV:D_MIX, :])
    o_ref[...] = x_ref[...] + mod_ref[0, 2:3, :] * y


def _out_proj(yconv, yattn, x2d, mod3, wout, seq):
    t = x2d.shape[0]
    tm = OUT_TM
    tpb = seq // tm
    return pl.pallas_call(
        _out_proj_body,
        grid=(t // tm,),
        in_specs=[
            pl.BlockSpec((tm, D_CONV), lambda i: (i, 0)),
            pl.BlockSpec((tm, D_ATTN), lambda i: (i, 0)),
            pl.BlockSpec((tm, D_MODEL), lambda i: (i, 0)),
            pl.BlockSpec((1, 3, D_MODEL), lambda i: (i // tpb, 0, 0)),
            _resident((D_MIX, D_MODEL)),
        ],
        out_specs=pl.BlockSpec((tm, D_MODEL), lambda i: (i, 0)),
        out_shape=jax.ShapeDtypeStruct((t, D_MODEL), _F32),
        compiler_params=pltpu.CompilerParams(
            dimension_semantics=("arbitrary",), vmem_limit_bytes=VMEM_LIMIT),
        name="out_proj",
    )(yconv, yattn, x2d, mod3, wout)


def _score_shift(q_g, k_g, qscale):
    bound = QK_HEAD * qscale * jnp.max(jnp.abs(q_g)) * jnp.max(jnp.abs(k_g))
    shift = (bound * 1.03 + 1.0).astype(_BF16).astype(_F32)
    use_shift = bound + shift <= EXP2_NORMAL_RANGE
    return jnp.where(use_shift, shift, 0.0), use_shift


def _rope_slab_cols(w):
    z = jnp.zeros(w.shape[:-1] + (HALF,), w.dtype)
    return jnp.concatenate([w[..., :HALF], z, w[..., HALF:], z], axis=-1)


def _layer(layer, x2d, mod, cos_t, sin_t, norm_g, w_in, win_all, conv_w, q_a_g, w_q_b, kv_a_g, w_kv_b, q_g, k_g,
           w_out, bsz, seq):
    wtail = jnp.concatenate([_rope_slab_cols(w_in[:, OFF_KR:OFF_ZA]), w_in[:, OFF_ZA:]], axis=-1).astype(_BF16)
    wq = w_q_b.reshape(Q_LORA, N_HEADS, QK_HEAD)
    wqb = jnp.concatenate([wq[..., :QK_NOPE], _rope_slab_cols(wq[..., QK_NOPE:])], axis=-1)
    wqb = wqb.reshape(Q_LORA, N_HEADS * HEAD_SLAB).astype(_BF16)
    wkvb = w_kv_b.astype(_BF16)
    qscale = math.log2(math.e) / math.sqrt(QK_HEAD)
    gq = jnp.concatenate([q_g[:QK_NOPE], _rope_slab_cols(q_g[QK_NOPE:])]).reshape(1, HEAD_SLAB) * qscale
    gk = jnp.concatenate([k_g[:QK_NOPE], _rope_slab_cols(k_g[QK_NOPE:])]).reshape(1, HEAD_SLAB)
    mod3 = mod.reshape(bsz, 3, D_MODEL)
    shift, use_shift = _score_shift(q_g, k_g, qscale)
    pad_lane = jnp.arange(ROPE_SLAB) == SHIFT_LANE
    qb = jnp.where(pad_lane, -shift, 0.0).reshape(1, ROPE_SLAB)
    kb = jnp.where(pad_lane, 1.0, 0.0).reshape(1, ROPE_SLAB)

    ng = norm_g.reshape(1, D_MODEL)
    yconv = _conv_proj(layer, x2d, mod3, ng, win_all, conv_w, seq)
    sz, q, k, v = _mla_proj(
        layer, x2d, mod3, ng, win_all, wtail, q_a_g.reshape(1, Q_LORA), wqb,
        kv_a_g.reshape(1, KV_LORA), wkvb, gq, gk, qb, kb, cos_t, sin_t, bsz, seq)
    yattn = lax.cond(use_shift, functools.partial(_attention, False), functools.partial(_attention, True),
                     q, k, v, sz)
    return _out_proj(yconv, yattn, x2d, mod3, w_out.astype(_BF16), seq)


def kernel(x, c, positions, ada_w, ada_b, norm_g, w_in, conv_w, q_a_g, w_q_b, kv_a_g, w_kv_b, q_g, k_g, w_out):
    bsz, seq, _ = x.shape
    depth = ada_w.shape[0]
    cos_t, sin_t = _rope_tables(positions)
    x2d = x.reshape(bsz * seq, D_MODEL)
    win_all = w_in.astype(_BF16)
    for l in range(depth):
        mod = _adaln_mod(c, ada_w[l], ada_b[l])
        x2d = _layer(l, x2d, mod, cos_t, sin_t, norm_g[l], w_in[l], win_all, conv_w[l], q_a_g[l], w_q_b[l],
                     kv_a_g[l], w_kv_b[l], q_g[l], k_g[l], w_out[l], bsz, seq)
    return x2d.reshape(bsz, seq, D_MODEL)
```

```python
import functools
import math

import jax
import jax.numpy as jnp
from jax import lax
from jax.experimental import pallas as pl
from jax.experimental.pallas import tpu as pltpu

D_MODEL = 2048
D_CONV = 1024
CONV_WIDTH = 3
N_HEADS = 8
QK_NOPE = 128
QK_ROPE = 64
QK_HEAD = QK_NOPE + QK_ROPE
V_HEAD = 128
D_ATTN = N_HEADS * V_HEAD
Q_LORA = 512
KV_LORA = 256
ROPE_BASE = 10000.0
D_MIX = D_CONV + D_ATTN
EPS = 1e-6

KV_HEAD = QK_NOPE + V_HEAD
MXU_N = 256
LANES = 128
SUBLANES = 8
ROPE_SLAB = LANES
HEAD_SLAB = QK_NOPE + ROPE_SLAB
HALF = QK_ROPE // 2
SHIFT_LANE = HALF
EXP2_NORMAL_RANGE = 120.0

OFF_XC = 0
OFF_BC = D_CONV
OFF_CC = 2 * D_CONV
OFF_ZC = 3 * D_CONV
OFF_CQ = 4 * D_CONV
OFF_CKV = OFF_CQ + Q_LORA
OFF_KR = OFF_CKV + KV_LORA
OFF_ZA = OFF_KR + QK_ROPE
TAIL_COLS = ROPE_SLAB + D_ATTN

ADA_TN = 512
ROPE_TT = 1024
CONV_TM = 512
MLA_TM = 512
CONV_CW = 256
ATTN_TQ = 1024
ATTN_TK = 512
OUT_TM = 512
VMEM_LIMIT = 56 * 1024 * 1024

_BF16 = jnp.bfloat16
_F32 = jnp.float32


def _sigmoid(z):
    return 1.0 / (1.0 + jnp.exp(-z))


def _dot(a, b):
    return jnp.dot(a, b, preferred_element_type=_F32)


def _adaln_body(ct_ref, w_ref, b_ref, o_ref):
    ct = ct_ref[...]
    sc = ct * _sigmoid(ct)
    w = w_ref[...]
    rows = [jnp.sum(w * sc[:, b:b + 1], axis=0, keepdims=True) for b in range(ct.shape[1])]
    o_ref[...] = jnp.concatenate(rows, axis=0) + b_ref[...]


def _adaln_mod(c, ada_w, ada_b):
    bsz = c.shape[0]
    n = ada_w.shape[1]
    return pl.pallas_call(
        _adaln_body,
        grid=(n // ADA_TN,),
        in_specs=[
            pl.BlockSpec((D_MODEL, bsz), lambda j: (0, 0)),
            pl.BlockSpec((D_MODEL, ADA_TN), lambda j: (0, j)),
            pl.BlockSpec((1, ADA_TN), lambda j: (0, j)),
        ],
        out_specs=pl.BlockSpec((bsz, ADA_TN), lambda j: (0, j)),
        out_shape=jax.ShapeDtypeStruct((bsz, n), _F32),
        name="adaln_mod",
    )(c.T, ada_w, ada_b.reshape(1, n))


def _rope_body(pos_ref, f_ref, cos_ref, sin_ref):
    ang = f_ref[...] * pos_ref[...].astype(_F32)
    c = jnp.cos(ang)
    s = jnp.sin(ang)
    z = jnp.zeros_like(c)
    cos_ref[...] = jnp.concatenate([c, z, c, z], axis=0).T
    sin_ref[...] = jnp.concatenate([-s, z, s, z], axis=0).T


def _rope_tables(positions):
    t = positions.size
    inv_freq = ROPE_BASE ** (-jnp.arange(0, QK_ROPE, 2, dtype=_F32) / QK_ROPE)
    out = pl.BlockSpec((ROPE_TT, ROPE_SLAB), lambda i: (i, 0))
    return pl.pallas_call(
        _rope_body,
        grid=(t // ROPE_TT,),
        in_specs=[pl.BlockSpec((1, ROPE_TT), lambda i: (0, i)), pl.BlockSpec((HALF, 1), lambda i: (0, 0))],
        out_specs=[out, out],
        out_shape=[jax.ShapeDtypeStruct((t, ROPE_SLAB), _F32)] * 2,
        name="rope_tables",
    )(positions.reshape(1, t), inv_freq.reshape(HALF, 1))


def _modulated_norm(x_ref, mod_ref, ng_ref):
    x = x_ref[...]
    xn = x * lax.rsqrt(jnp.mean(x * x, axis=-1, keepdims=True) + EPS)
    a = ng_ref[...] * (1.0 + mod_ref[0, 1:2, :])
    return (xn * a + mod_ref[0, 0:1, :]).astype(_BF16)


def _conv_proj_body(tiles_per_batch, x_ref, mod_ref, ng_ref, win_ref, convw_ref, yconv_ref,
                    h_scr, carry_scr, vbuf_scr):
    tm = x_ref.shape[0]

    @pl.when(pl.program_id(0) % tiles_per_batch == 0)
    def _():
        carry_scr[...] = jnp.zeros_like(carry_scr)

    h_scr[...] = _modulated_norm(x_ref, mod_ref, ng_ref)

    def proj(off, width):
        return _dot(h_scr[...], win_ref[:, off:off + width])

    for lo in range(0, D_CONV, CONV_CW):
        xc = proj(OFF_XC + lo, CONV_CW)
        bc = proj(OFF_BC + lo, CONV_CW)
        cc = proj(OFF_CC + lo, CONV_CW)
        zc = proj(OFF_ZC + lo, CONV_CW)
        v0 = cc * xc
        vbuf_scr[0:SUBLANES, :] = carry_scr[:, lo:lo + CONV_CW]
        vbuf_scr[SUBLANES:SUBLANES + tm, :] = v0
        carry_scr[:, lo:lo + CONV_CW] = v0[tm - SUBLANES:, :]
        v1 = vbuf_scr[SUBLANES - 1:SUBLANES - 1 + tm, :]
        v2 = vbuf_scr[SUBLANES - 2:SUBLANES - 2 + tm, :]
        w = convw_ref[:, lo:lo + CONV_CW]
        conv = w[0:1, :] * v2 + w[1:2, :] * v1 + w[2:3, :] * v0
        yconv_ref[:, lo:lo + CONV_CW] = (bc * conv * (zc * _sigmoid(zc))).astype(_BF16)


def _resident(shape, index=None):
    index = (0,) * len(shape) if index is None else index
    return pl.BlockSpec(shape, lambda i: index, pipeline_mode=pl.Buffered(1))


def _conv_proj(layer, x2d, mod3, norm_g, win_all, conv_w, seq):
    t = x2d.shape[0]
    tm = CONV_TM
    tpb = seq // tm
    return pl.pallas_call(
        functools.partial(_conv_proj_body, tpb),
        grid=(t // tm,),
        in_specs=[
            pl.BlockSpec((tm, D_MODEL), lambda i: (i, 0)),
            pl.BlockSpec((1, 3, D_MODEL), lambda i: (i // tpb, 0, 0)),
            _resident((1, D_MODEL)),
            _resident((None, D_MODEL, 4 * D_CONV), (layer, 0, 0)),
            _resident((CONV_WIDTH, D_CONV)),
        ],
        out_specs=pl.BlockSpec((tm, D_CONV), lambda i: (i, 0)),
        out_shape=jax.ShapeDtypeStruct((t, D_CONV), _BF16),
        scratch_shapes=[
            pltpu.VMEM((tm, D_MODEL), _BF16),
            pltpu.VMEM((SUBLANES, D_CONV), _F32),
            pltpu.VMEM((SUBLANES + tm, CONV_CW), _F32),
        ],
        compiler_params=pltpu.CompilerParams(
            dimension_semantics=("arbitrary",), vmem_limit_bytes=VMEM_LIMIT),
        name="conv_proj",
    )(x2d, mod3, norm_g, win_all, conv_w)


def _mla_proj_body(x_ref, mod_ref, ng_ref, wcq_ref, wckv_ref, wtail_ref, qag_ref, wqb_ref, kvag_ref, wkvb_ref,
                   gq_ref, gk_ref, qb_ref, kb_ref, cos_ref, sin_ref,
                   sz_ref, q_ref, k_ref, v_ref, h_scr, cqn_scr, ckvn_scr):
    h_scr[...] = _modulated_norm(x_ref, mod_ref, ng_ref)
    cos_t = cos_ref[...]
    sin_t = sin_ref[...]

    def rope(slab):
        return slab * cos_t + pltpu.roll(slab, ROPE_SLAB // 2, axis=1) * sin_t

    def rms(parts, width):
        ss = sum(jnp.sum(p * p, axis=-1, keepdims=True) for p in parts)
        return lax.rsqrt(ss * (1.0 / width) + EPS)

    cq = [_dot(h_scr[...], wcq_ref[:, lo:lo + MXU_N]) for lo in range(0, Q_LORA, MXU_N)]
    r = rms(cq, Q_LORA)
    for i, part in enumerate(cq):
        lo = i * MXU_N
        cqn_scr[:, lo:lo + MXU_N] = (part * r * qag_ref[:, lo:lo + MXU_N]).astype(_BF16)
    ckv = _dot(h_scr[...], wckv_ref[...])
    ckvn_scr[...] = (ckv * rms([ckv], KV_LORA) * kvag_ref[...]).astype(_BF16)

    kr = _dot(h_scr[...], wtail_ref[:, 0:ROPE_SLAB])
    gk = gk_ref[...]
    gq = gq_ref[...]
    kr_rot = rope(kr * gk[:, QK_NOPE:HEAD_SLAB])

    def lane_sum(sq):
        ones = jnp.ones((sq.shape[1], LANES), _BF16)
        return _dot(sq.astype(_BF16), ones)

    ss_r = lane_sum(kr * kr)
    for hd in range(N_HEADS):
        lo = hd * HEAD_SLAB
        slab = _dot(cqn_scr[...], wqb_ref[:, lo:lo + HEAD_SLAB])
        r = lax.rsqrt(lane_sum(slab * slab) * (1.0 / QK_HEAD) + EPS)
        q_ref[0, hd, :, 0:QK_NOPE] = (slab[:, 0:QK_NOPE] * r * gq[:, 0:QK_NOPE]).astype(_BF16)
        q_rope = rope(slab[:, QK_NOPE:HEAD_SLAB] * r * gq[:, QK_NOPE:HEAD_SLAB])
        q_ref[0, hd, :, QK_NOPE:HEAD_SLAB] = (q_rope + qb_ref[...]).astype(_BF16)

        kv = _dot(ckvn_scr[...], wkvb_ref[:, hd * KV_HEAD:(hd + 1) * KV_HEAD])
        kn = kv[:, 0:QK_NOPE]
        r = lax.rsqrt((lane_sum(kn * kn) + ss_r) * (1.0 / QK_HEAD) + EPS)
        k_ref[0, hd, :, 0:QK_NOPE] = (kn * r * gk[:, 0:QK_NOPE]).astype(_BF16)
        k_ref[0, hd, :, QK_NOPE:HEAD_SLAB] = (kr_rot * r + kb_ref[...]).astype(_BF16)
        v_ref[0, hd, :, :] = kv[:, QK_NOPE:KV_HEAD].astype(_BF16)

    for lo in range(0, D_ATTN, MXU_N):
        za = _dot(h_scr[...], wtail_ref[:, ROPE_SLAB + lo:ROPE_SLAB + lo + MXU_N])
        sz_ref[:, lo:lo + MXU_N] = (za * _sigmoid(za)).astype(_BF16)


def _mla_proj(layer, x2d, mod3, norm_g, win_all, wtail, q_a_g, wqb, kv_a_g, wkvb, gq, gk, qb, kb, cos_t, sin_t,
              bsz, seq):
    t = x2d.shape[0]
    tm = MLA_TM
    tpb = seq // tm
    tok = lambda width: pl.BlockSpec((tm, width), lambda i: (i, 0))
    head = lambda width: pl.BlockSpec((1, N_HEADS, tm, width), lambda i: (i // tpb, 0, i % tpb, 0))
    return pl.pallas_call(
        _mla_proj_body,
        grid=(t // tm,),
        in_specs=[
            tok(D_MODEL),
            pl.BlockSpec((1, 3, D_MODEL), lambda i: (i // tpb, 0, 0)),
            _resident((1, D_MODEL)),
            _resident((None, D_MODEL, Q_LORA), (layer, 0, OFF_CQ // Q_LORA)),
            _resident((None, D_MODEL, KV_LORA), (layer, 0, OFF_CKV // KV_LORA)),
            _resident((D_MODEL, TAIL_COLS)),
            _resident((1, Q_LORA)),
            _resident((Q_LORA, N_HEADS * HEAD_SLAB)),
            _resident((1, KV_LORA)),
            _resident((KV_LORA, N_HEADS * (QK_NOPE + V_HEAD))),
            _resident((1, HEAD_SLAB)),
            _resident((1, HEAD_SLAB)),
            _resident((1, ROPE_SLAB)),
            _resident((1, ROPE_SLAB)),
            tok(ROPE_SLAB),
            tok(ROPE_SLAB),
        ],
        out_specs=[tok(D_ATTN), head(HEAD_SLAB), head(HEAD_SLAB), head(V_HEAD)],
        out_shape=[
            jax.ShapeDtypeStruct((t, D_ATTN), _BF16),
            jax.ShapeDtypeStruct((bsz, N_HEADS, seq, HEAD_SLAB), _BF16),
            jax.ShapeDtypeStruct((bsz, N_HEADS, seq, HEAD_SLAB), _BF16),
            jax.ShapeDtypeStruct((bsz, N_HEADS, seq, V_HEAD), _BF16),
        ],
        scratch_shapes=[
            pltpu.VMEM((tm, D_MODEL), _BF16),
            pltpu.VMEM((tm, Q_LORA), _BF16),
            pltpu.VMEM((tm, KV_LORA), _BF16),
        ],
        compiler_params=pltpu.CompilerParams(
            dimension_semantics=("arbitrary",), vmem_limit_bytes=VMEM_LIMIT),
        name="mla_proj",
    )(x2d, mod3, norm_g, win_all, win_all, wtail, q_a_g, wqb, kv_a_g, wkvb, gq, gk, qb, kb, cos_t, sin_t)


def _attn_body(online_max, q_ref, k_ref, v_ref, sz_ref, o_ref,
               sa0_scr, sa1_scr, sb_scr, m_scr, l_scr, acc_scr):
    tq = ATTN_TQ
    tk = ATTN_TK
    half = tq // 2
    assert tq == 2 * tk
    nq = q_ref.shape[2] // tq
    assert nq % 2 == 0

    def scores(qi, j, r0, nr):
        qstart = pl.multiple_of(qi * tq + r0, half)
        kstart = pl.multiple_of(j * tk, tk)
        return lax.dot_general(q_ref[0, 0, pl.ds(qstart, nr), :], k_ref[0, 0, pl.ds(kstart, tk), :],
                               (((1,), (1,)), ((), ())), preferred_element_type=_F32)

    def update(j, s, r0, nr, triangular):
        start = pl.multiple_of(j * tk, tk)
        v = v_ref[0, 0, pl.ds(start, tk), :]
        if triangular:
            row = lax.broadcasted_iota(jnp.int32, (nr, tk), 0)
            col = lax.broadcasted_iota(jnp.int32, (nr, tk), 1)
            s = jnp.where(col <= row, s, -jnp.inf)
        tiles = [s[:, t * LANES:(t + 1) * LANES] for t in range(tk // LANES)]
        if not online_max:
            ps = [jnp.exp2(t) for t in tiles]
            l_scr[r0:r0 + nr, :] += functools.reduce(jnp.add, ps)
            p = jnp.concatenate(ps, axis=-1).astype(_BF16)
            acc_scr[r0:r0 + nr, :] += _dot(p, v)
            return
        mx = functools.reduce(jnp.maximum, tiles)
        m_prev = m_scr[r0:r0 + nr, :]
        m_new = jnp.maximum(m_prev, jnp.max(mx, axis=-1, keepdims=True))
        alpha = jnp.exp2(m_prev - m_new)
        ps = [jnp.exp2(t - m_new) for t in tiles]
        l_scr[r0:r0 + nr, :] = alpha * l_scr[r0:r0 + nr, :] + functools.reduce(jnp.add, ps)
        p = jnp.concatenate(ps, axis=-1).astype(_BF16)
        acc_scr[r0:r0 + nr, :] = alpha * acc_scr[r0:r0 + nr, :] + _dot(p, v)
        m_scr[r0:r0 + nr, :] = m_new

    def query_block(qi, sa_scr, sa_next_scr):
        if online_max:
            m_scr[...] = jnp.full_like(m_scr, -jnp.inf)
        l_scr[...] = jnp.zeros_like(l_scr)
        acc_scr[...] = jnp.zeros_like(acc_scr)

        def pair(jj, carry):
            j = 2 * jj
            sb_scr[...] = scores(qi, j + 1, 0, tq)
            update(j, sa_scr[...], 0, tq, False)
            sa_scr[...] = scores(qi, j + 2, 0, tq)
            update(j + 1, sb_scr[...], 0, tq, False)
            return carry

        lax.fori_loop(0, qi, pair, 0)

        jd = 2 * qi
        sb_scr[half:tq, :] = scores(qi, jd + 1, half, half)
        update(jd, sa_scr[0:half, :], 0, half, True)
        sa_next_scr[...] = scores(jnp.minimum(qi + 1, nq - 1), 0, 0, tq)
        update(jd, sa_scr[half:tq, :], half, half, False)
        update(jd + 1, sb_scr[half:tq, :], half, half, True)

        rows = pl.ds(pl.multiple_of(qi * tq, tq), tq)
        o = acc_scr[...] / jnp.sum(l_scr[...], axis=-1, keepdims=True)
        o_ref[rows, :] = (o * sz_ref[rows, :].astype(_F32)).astype(_BF16)

    sa0_scr[...] = scores(0, 0, 0, tq)

    def two_blocks(ib, carry):
        query_block(2 * ib, sa0_scr, sa1_scr)
        query_block(2 * ib + 1, sa1_scr, sa0_scr)
        return carry

    lax.fori_loop(0, nq // 2, two_blocks, 0)


def _attention(online_max, q, k, v, sz):
    bsz, _, seq, _ = q.shape
    tq = ATTN_TQ
    per_head = lambda width: pl.BlockSpec((1, 1, seq, width), lambda b, h: (b, h, 0, 0))
    return pl.pallas_call(
        functools.partial(_attn_body, online_max),
        grid=(bsz, N_HEADS),
        in_specs=[
            per_head(HEAD_SLAB),
            per_head(HEAD_SLAB),
            per_head(V_HEAD),
            pl.BlockSpec((seq, V_HEAD), lambda b, h: (b, h)),
        ],
        out_specs=pl.BlockSpec((seq, V_HEAD), lambda b, h: (b, h)),
        out_shape=jax.ShapeDtypeStruct((bsz * seq, D_ATTN), _BF16),
        scratch_shapes=[
            pltpu.VMEM((tq, ATTN_TK), _F32),
            pltpu.VMEM((tq, ATTN_TK), _F32),
            pltpu.VMEM((tq, ATTN_TK), _F32),
            pltpu.VMEM((tq, LANES), _F32),
            pltpu.VMEM((tq, LANES), _F32),
            pltpu.VMEM((tq, V_HEAD), _F32),
        ],
        compiler_params=pltpu.CompilerParams(
            dimension_semantics=("arbitrary", "arbitrary"), vmem_limit_bytes=VMEM_LIMIT),
        name="attention_online_max" if online_max else "attention_shifted",
    )(q, k, v, sz)


def _out_proj_body(yc_ref, ya_ref, x_ref, mod_ref, w_ref, o_ref):
    y = _dot(yc_ref[...], w_ref[0:D_CONV, :]) + _dot(ya_ref[...], w_ref[D_CONV:D_MIX, :])
    o_ref[...] = x_ref[...] + mod_ref[0, 2:3, :] * y


def _out_proj(yconv, yattn, x2d, mod3, wout, seq):
    t = x2d.shape[0]
    tm = OUT_TM
    tpb = seq // tm
    return pl.pallas_call(
        _out_proj_body,
        grid=(t // tm,),
        in_specs=[
            pl.BlockSpec((tm, D_CONV), lambda i: (i, 0)),
            pl.BlockSpec((tm, D_ATTN), lambda i: (i, 0)),
            pl.BlockSpec((tm, D_MODEL), lambda i: (i, 0)),
            pl.BlockSpec((1, 3, D_MODEL), lambda i: (i // tpb, 0, 0)),
            _resident((D_MIX, D_MODEL)),
        ],
        out_specs=pl.BlockSpec((tm, D_MODEL), lambda i: (i, 0)),
        out_shape=jax.ShapeDtypeStruct((t, D_MODEL), _F32),
        compiler_params=pltpu.CompilerParams(
            dimension_semantics=("arbitrary",), vmem_limit_bytes=VMEM_LIMIT),
        name="out_proj",
    )(yconv, yattn, x2d, mod3, wout)


def _score_shift(q_g, k_g, qscale):
    bound = QK_HEAD * qscale * jnp.max(jnp.abs(q_g)) * jnp.max(jnp.abs(k_g))
    shift = (bound * 1.03 + 1.0).astype(_BF16).astype(_F32)
    use_shift = bound + shift <= EXP2_NORMAL_RANGE
    return jnp.where(use_shift, shift, 0.0), use_shift


def _rope_slab_cols(w):
    z = jnp.zeros(w.shape[:-1] + (HALF,), w.dtype)
    return jnp.concatenate([w[..., :HALF], z, w[..., HALF:], z], axis=-1)


def _layer(layer, x2d, mod, cos_t, sin_t, norm_g, w_in, win_all, conv_w, q_a_g, w_q_b, kv_a_g, w_kv_b, q_g, k_g,
           w_out, bsz, seq):
    wtail = jnp.concatenate([_rope_slab_cols(w_in[:, OFF_KR:OFF_ZA]), w_in[:, OFF_ZA:]], axis=-1).astype(_BF16)
    wq = w_q_b.reshape(Q_LORA, N_HEADS, QK_HEAD)
    wqb = jnp.concatenate([wq[..., :QK_NOPE], _rope_slab_cols(wq[..., QK_NOPE:])], axis=-1)
    wqb = wqb.reshape(Q_LORA, N_HEADS * HEAD_SLAB).astype(_BF16)
    wkvb = w_kv_b.astype(_BF16)
    qscale = math.log2(math.e) / math.sqrt(QK_HEAD)
    gq = jnp.concatenate([q_g[:QK_NOPE], _rope_slab_cols(q_g[QK_NOPE:])]).reshape(1, HEAD_SLAB) * qscale
    gk = jnp.concatenate([k_g[:QK_NOPE], _rope_slab_cols(k_g[QK_NOPE:])]).reshape(1, HEAD_SLAB)
    mod3 = mod.reshape(bsz, 3, D_MODEL)
    shift, use_shift = _score_shift(q_g, k_g, qscale)
    pad_lane = jnp.arange(ROPE_SLAB) == SHIFT_LANE
    qb = jnp.where(pad_lane, -shift, 0.0).reshape(1, ROPE_SLAB)
    kb = jnp.where(pad_lane, 1.0, 0.0).reshape(1, ROPE_SLAB)

    ng = norm_g.reshape(1, D_MODEL)
    yconv = _conv_proj(layer, x2d, mod3, ng, win_all, conv_w, seq)
    sz, q, k, v = _mla_proj(
        layer, x2d, mod3, ng, win_all, wtail, q_a_g.reshape(1, Q_LORA), wqb,
        kv_a_g.reshape(1, KV_LORA), wkvb, gq, gk, qb, kb, cos_t, sin_t, bsz, seq)
    yattn = lax.cond(use_shift, functools.partial(_attention, False), functools.partial(_attention, True),
                     q, k, v, sz)
    return _out_proj(yconv, yattn, x2d, mod3, w_out.astype(_BF16), seq)


def kernel(x, c, positions, ada_w, ada_b, norm_g, w_in, conv_w, q_a_g, w_q_b, kv_a_g, w_kv_b, q_g, k_g, w_out):
    bsz, seq, _ = x.shape
    depth = ada_w.shape[0]
    cos_t, sin_t = _rope_tables(positions)
    x2d = x.reshape(bsz * seq, D_MODEL)
    win_all = w_in.astype(_BF16)
    for l in range(depth):
        mod = _adaln_mod(c, ada_w[l], ada_b[l])
        x2d = _layer(l, x2d, mod, cos_t, sin_t, norm_g[l], w_in[l], win_all, conv_w[l], q_a_g[l], w_q_b[l],
                     kv_a_g[l], w_kv_b[l], q_g[l], k_g[l], w_out[l], bsz, seq)
    return x2d.reshape(bsz, seq, D_MODEL)
```

```python
import functools
import math

import jax
import jax.numpy as jnp
from jax import lax
from jax.experimental import pallas as pl
from jax.experimental.pallas import tpu as pltpu

D_MODEL = 2048
D_CONV = 1024
CONV_WIDTH = 3
N_HEADS = 8
QK_NOPE = 128
QK_ROPE = 64
QK_HEAD = QK_NOPE + QK_ROPE
V_HEAD = 128
D_ATTN = N_HEADS * V_HEAD
Q_LORA = 512
KV_LORA = 256
ROPE_BASE = 10000.0
D_MIX = D_CONV + D_ATTN
EPS = 1e-6

KV_HEAD = QK_NOPE + V_HEAD
MXU_N = 256
LANES = 128
SUBLANES = 8
ROPE_SLAB = LANES
HEAD_SLAB = QK_NOPE + ROPE_SLAB
HALF = QK_ROPE // 2
SHIFT_LANE = HALF
EXP2_NORMAL_RANGE = 120.0

OFF_XC = 0
OFF_BC = D_CONV
OFF_CC = 2 * D_CONV
OFF_ZC = 3 * D_CONV
OFF_CQ = 4 * D_CONV
OFF_CKV = OFF_CQ + Q_LORA
OFF_KR = OFF_CKV + KV_LORA
OFF_ZA = OFF_KR + QK_ROPE
TAIL_COLS = ROPE_SLAB + D_ATTN

ADA_TN = 512
ROPE_TT = 1024
CONV_TM = 512
MLA_TM = 512
CONV_CW = 256
ATTN_TQ = 1024
ATTN_TK = 512
OUT_TM = 512
VMEM_LIMIT = 56 * 1024 * 1024

_BF16 = jnp.bfloat16
_F32 = jnp.float32


def _sigmoid(z):
    return 1.0 / (1.0 + jnp.exp(-z))


def _dot(a, b):
    return jnp.dot(a, b, preferred_element_type=_F32)


def _adaln_body(ct_ref, w_ref, b_ref, o_ref):
    ct = ct_ref[...]
    sc = ct * _sigmoid(ct)
    w = w_ref[...]
    rows = [jnp.sum(w * sc[:, b:b + 1], axis=0, keepdims=True) for b in range(ct.shape[1])]
    o_ref[...] = jnp.concatenate(rows, axis=0) + b_ref[...]


def _adaln_mod(c, ada_w, ada_b):
    bsz = c.shape[0]
    n = ada_w.shape[1]
    return pl.pallas_call(
        _adaln_body,
        grid=(n // ADA_TN,),
        in_specs=[
            pl.BlockSpec((D_MODEL, bsz), lambda j: (0, 0)),
            pl.BlockSpec((D_MODEL, ADA_TN), lambda j: (0, j)),
            pl.BlockSpec((1, ADA_TN), lambda j: (0, j)),
        ],
        out_specs=pl.BlockSpec((bsz, ADA_TN), lambda j: (0, j)),
        out_shape=jax.ShapeDtypeStruct((bsz, n), _F32),
        name="adaln_mod",
    )(c.T, ada_w, ada_b.reshape(1, n))


def _rope_body(pos_ref, f_ref, cos_ref, sin_ref):
    ang = f_ref[...] * pos_ref[...].astype(_F32)
    c = jnp.cos(ang)
    s = jnp.sin(ang)
    z = jnp.zeros_like(c)
    cos_ref[...] = jnp.concatenate([c, z, c, z], axis=0).T
    sin_ref[...] = jnp.concatenate([-s, z, s, z], axis=0).T


def _rope_tables(positions):
    t = positions.size
    inv_freq = ROPE_BASE ** (-jnp.arange(0, QK_ROPE, 2, dtype=_F32) / QK_ROPE)
    out = pl.BlockSpec((ROPE_TT, ROPE_SLAB), lambda i: (i, 0))
    return pl.pallas_call(
        _rope_body,
        grid=(t // ROPE_TT,),
        in_specs=[pl.BlockSpec((1, ROPE_TT), lambda i: (0, i)), pl.BlockSpec((HALF, 1), lambda i: (0, 0))],
        out_specs=[out, out],
        out_shape=[jax.ShapeDtypeStruct((t, ROPE_SLAB), _F32)] * 2,
        name="rope_tables",
    )(positions.reshape(1, t), inv_freq.reshape(HALF, 1))


def _modulated_norm(x_ref, mod_ref, ng_ref):
    x = x_ref[...]
    xn = x * lax.rsqrt(jnp.mean(x * x, axis=-1, keepdims=True) + EPS)
    a = ng_ref[...] * (1.0 + mod_ref[0, 1:2, :])
    return (xn * a + mod_ref[0, 0:1, :]).astype(_BF16)


def _conv_proj_body(tiles_per_batch, x_ref, mod_ref, ng_ref, win_ref, convw_ref, yconv_ref,
                    h_scr, carry_scr, vbuf_scr):
    tm = x_ref.shape[0]

    @pl.when(pl.program_id(0) % tiles_per_batch == 0)
    def _():
        carry_scr[...] = jnp.zeros_like(carry_scr)

    h_scr[...] = _modulated_norm(x_ref, mod_ref, ng_ref)

    def proj(off, width):
        return _dot(h_scr[...], win_ref[:, off:off + width])

    for lo in range(0, D_CONV, CONV_CW):
        xc = proj(OFF_XC + lo, CONV_CW)
        bc = proj(OFF_BC + lo, CONV_CW)
        cc = proj(OFF_CC + lo, CONV_CW)
        zc = proj(OFF_ZC + lo, CONV_CW)
        v0 = cc * xc
        vbuf_scr[0:SUBLANES, :] = carry_scr[:, lo:lo + CONV_CW]
        vbuf_scr[SUBLANES:SUBLANES + tm, :] = v0
        carry_scr[:, lo:lo + CONV_CW] = v0[tm - SUBLANES:, :]
        v1 = vbuf_scr[SUBLANES - 1:SUBLANES - 1 + tm, :]
        v2 = vbuf_scr[SUBLANES - 2:SUBLANES - 2 + tm, :]
        w = convw_ref[:, lo:lo + CONV_CW]
        conv = w[0:1, :] * v2 + w[1:2, :] * v1 + w[2:3, :] * v0
        yconv_ref[:, lo:lo + CONV_CW] = (bc * conv * (zc * _sigmoid(zc))).astype(_BF16)


def _resident(shape, index=None):
    index = (0,) * len(shape) if index is None else index
    return pl.BlockSpec(shape, lambda i: index, pipeline_mode=pl.Buffered(1))


def _conv_proj(layer, x2d, mod3, norm_g, win_all, conv_w, seq):
    t = x2d.shape[0]
    tm = CONV_TM
    tpb = seq // tm
    return pl.pallas_call(
        functools.partial(_conv_proj_body, tpb),
        grid=(t // tm,),
        in_specs=[
            pl.BlockSpec((tm, D_MODEL), lambda i: (i, 0)),
            pl.BlockSpec((1, 3, D_MODEL), lambda i: (i // tpb, 0, 0)),
            _resident((1, D_MODEL)),
            _resident((None, D_MODEL, 4 * D_CONV), (layer, 0, 0)),
            _resident((CONV_WIDTH, D_CONV)),
        ],
        out_specs=pl.BlockSpec((tm, D_CONV), lambda i: (i, 0)),
        out_shape=jax.ShapeDtypeStruct((t, D_CONV), _BF16),
        scratch_shapes=[
            pltpu.VMEM((tm, D_MODEL), _BF16),
            pltpu.VMEM((SUBLANES, D_CONV), _F32),
            pltpu.VMEM((SUBLANES + tm, CONV_CW), _F32),
        ],
        compiler_params=pltpu.CompilerParams(
            dimension_semantics=("arbitrary",), vmem_limit_bytes=VMEM_LIMIT),
        name="conv_proj",
    )(x2d, mod3, norm_g, win_all, conv_w)


def _mla_proj_body(x_ref, mod_ref, ng_ref, wcq_ref, wckv_ref, wtail_ref, qag_ref, wqb_ref, kvag_ref, wkvb_ref,
                   gq_ref, gk_ref, qb_ref, kb_ref, cos_ref, sin_ref,
                   sz_ref, q_ref, kt_ref, v_ref, h_scr, cqn_scr, ckvn_scr):
    h_scr[...] = _modulated_norm(x_ref, mod_ref, ng_ref)
    cos_t = cos_ref[...]
    sin_t = sin_ref[...]

    def rope(slab):
        return slab * cos_t + pltpu.roll(slab, ROPE_SLAB // 2, axis=1) * sin_t

    def rms(parts, width):
        ss = sum(jnp.sum(p * p, axis=-1, keepdims=True) for p in parts)
        return lax.rsqrt(ss * (1.0 / width) + EPS)

    cq = [_dot(h_scr[...], wcq_ref[:, lo:lo + MXU_N]) for lo in range(0, Q_LORA, MXU_N)]
    r = rms(cq, Q_LORA)
    for i, part in enumerate(cq):
        lo = i * MXU_N
        cqn_scr[:, lo:lo + MXU_N] = (part * r * qag_ref[:, lo:lo + MXU_N]).astype(_BF16)
    ckv = _dot(h_scr[...], wckv_ref[...])
    ckvn_scr[...] = (ckv * rms([ckv], KV_LORA) * kvag_ref[...]).astype(_BF16)

    kr = _dot(h_scr[...], wtail_ref[:, 0:ROPE_SLAB])
    gk = gk_ref[...]
    gq = gq_ref[...]
    kr_rot = rope(kr * gk[:, QK_NOPE:HEAD_SLAB])

    def lane_sum(sq):
        ones = jnp.ones((sq.shape[1], LANES), _BF16)
        return _dot(sq.astype(_BF16), ones)

    ss_r = lane_sum(kr * kr)
    for hd in range(N_HEADS):
        lo = hd * HEAD_SLAB
        slab = _dot(cqn_scr[...], wqb_ref[:, lo:lo + HEAD_SLAB])
        r = lax.rsqrt(lane_sum(slab * slab) * (1.0 / QK_HEAD) + EPS)
        q_ref[0, hd, :, 0:QK_NOPE] = (slab[:, 0:QK_NOPE] * r * gq[:, 0:QK_NOPE]).astype(_BF16)
        q_rope = rope(slab[:, QK_NOPE:HEAD_SLAB] * r * gq[:, QK_NOPE:HEAD_SLAB])
        q_ref[0, hd, :, QK_NOPE:HEAD_SLAB] = (q_rope + qb_ref[...]).astype(_BF16)

        kv = _dot(ckvn_scr[...], wkvb_ref[:, hd * KV_HEAD:(hd + 1) * KV_HEAD])
        kn = kv[:, 0:QK_NOPE]
        r = lax.rsqrt((lane_sum(kn * kn) + ss_r) * (1.0 / QK_HEAD) + EPS)
        kt_ref[0, hd, 0:QK_NOPE, :] = (kn * r * gk[:, 0:QK_NOPE]).T.astype(_BF16)
        kt_ref[0, hd, QK_NOPE:HEAD_SLAB, :] = (kr_rot * r + kb_ref[...]).T.astype(_BF16)
        v_ref[0, hd, :, :] = kv[:, QK_NOPE:KV_HEAD].astype(_BF16)

    for lo in range(0, D_ATTN, MXU_N):
        za = _dot(h_scr[...], wtail_ref[:, ROPE_SLAB + lo:ROPE_SLAB + lo + MXU_N])
        sz_ref[:, lo:lo + MXU_N] = (za * _sigmoid(za)).astype(_BF16)


def _mla_proj(layer, x2d, mod3, norm_g, win_all, wtail, q_a_g, wqb, kv_a_g, wkvb, gq, gk, qb, kb, cos_t, sin_t,
              bsz, seq):
    t = x2d.shape[0]
    tm = MLA_TM
    tpb = seq // tm
    tok = lambda width: pl.BlockSpec((tm, width), lambda i: (i, 0))
    head = lambda width: pl.BlockSpec((1, N_HEADS, tm, width), lambda i: (i // tpb, 0, i % tpb, 0))
    return pl.pallas_call(
        _mla_proj_body,
        grid=(t // tm,),
        in_specs=[
            tok(D_MODEL),
            pl.BlockSpec((1, 3, D_MODEL), lambda i: (i // tpb, 0, 0)),
            _resident((1, D_MODEL)),
            _resident((None, D_MODEL, Q_LORA), (layer, 0, OFF_CQ // Q_LORA)),
            _resident((None, D_MODEL, KV_LORA), (layer, 0, OFF_CKV // KV_LORA)),
            _resident((D_MODEL, TAIL_COLS)),
            _resident((1, Q_LORA)),
            _resident((Q_LORA, N_HEADS * HEAD_SLAB)),
            _resident((1, KV_LORA)),
            _resident((KV_LORA, N_HEADS * (QK_NOPE + V_HEAD))),
            _resident((1, HEAD_SLAB)),
            _resident((1, HEAD_SLAB)),
            _resident((1, ROPE_SLAB)),
            _resident((1, ROPE_SLAB)),
            tok(ROPE_SLAB),
            tok(ROPE_SLAB),
        ],
        out_specs=[
            tok(D_ATTN),
            head(HEAD_SLAB),
            pl.BlockSpec((1, N_HEADS, HEAD_SLAB, tm), lambda i: (i // tpb, 0, 0, i % tpb)),
            head(V_HEAD),
        ],
        out_shape=[
            jax.ShapeDtypeStruct((t, D_ATTN), _BF16),
            jax.ShapeDtypeStruct((bsz, N_HEADS, seq, HEAD_SLAB), _BF16),
            jax.ShapeDtypeStruct((bsz, N_HEADS, HEAD_SLAB, seq), _BF16),
            jax.ShapeDtypeStruct((bsz, N_HEADS, seq, V_HEAD), _BF16),
        ],
        scratch_shapes=[
            pltpu.VMEM((tm, D_MODEL), _BF16),
            pltpu.VMEM((tm, Q_LORA), _BF16),
            pltpu.VMEM((tm, KV_LORA), _BF16),
        ],
        compiler_params=pltpu.CompilerParams(
            dimension_semantics=("arbitrary",), vmem_limit_bytes=VMEM_LIMIT),
        name="mla_proj",
    )(x2d, mod3, norm_g, win_all, win_all, wtail, q_a_g, wqb, kv_a_g, wkvb, gq, gk, qb, kb, cos_t, sin_t)


def _attn_body(online_max, q_ref, kt_ref, v_ref, sz_ref, o_ref,
               sa0_scr, sa1_scr, sb_scr, m_scr, l_scr, acc_scr):
    tq = ATTN_TQ
    tk = ATTN_TK
    half = tq // 2
    assert tq == 2 * tk
    nq = q_ref.shape[2] // tq
    assert nq % 2 == 0

    def scores(qi, j, r0, nr):
        qstart = pl.multiple_of(qi * tq + r0, half)
        kstart = pl.multiple_of(j * tk, tk)
        return _dot(q_ref[0, 0, pl.ds(qstart, nr), :], kt_ref[0, 0, :, pl.ds(kstart, tk)])

    def update(j, s, r0, nr, triangular):
        start = pl.multiple_of(j * tk, tk)
        v = v_ref[0, 0, pl.ds(start, tk), :]
        if triangular:
            row = lax.broadcasted_iota(jnp.int32, (nr, tk), 0)
            col = lax.broadcasted_iota(jnp.int32, (nr, tk), 1)
            s = jnp.where(col <= row, s, -jnp.inf)
        tiles = [s[:, t * LANES:(t + 1) * LANES] for t in range(tk // LANES)]
        if not online_max:
            ps = [jnp.exp2(t) for t in tiles]
            l_scr[r0:r0 + nr, :] += functools.reduce(jnp.add, ps)
            p = jnp.concatenate(ps, axis=-1).astype(_BF16)
            acc_scr[r0:r0 + nr, :] += _dot(p, v)
            return
        mx = functools.reduce(jnp.maximum, tiles)
        m_prev = m_scr[r0:r0 + nr, :]
        m_new = jnp.maximum(m_prev, jnp.max(mx, axis=-1, keepdims=True))
        alpha = jnp.exp2(m_prev - m_new)
        ps = [jnp.exp2(t - m_new) for t in tiles]
        l_scr[r0:r0 + nr, :] = alpha * l_scr[r0:r0 + nr, :] + functools.reduce(jnp.add, ps)
        p = jnp.concatenate(ps, axis=-1).astype(_BF16)
        acc_scr[r0:r0 + nr, :] = alpha * acc_scr[r0:r0 + nr, :] + _dot(p, v)
        m_scr[r0:r0 + nr, :] = m_new

    def query_block(qi, sa_scr, sa_next_scr):
        if online_max:
            m_scr[...] = jnp.full_like(m_scr, -jnp.inf)
        l_scr[...] = jnp.zeros_like(l_scr)
        acc_scr[...] = jnp.zeros_like(acc_scr)

        def pair(jj, carry):
            j = 2 * jj
            sb_scr[...] = scores(qi, j + 1, 0, tq)
            update(j, sa_scr[...], 0, tq, False)
            sa_scr[...] = scores(qi, j + 2, 0, tq)
            update(j + 1, sb_scr[...], 0, tq, False)
            return carry

        lax.fori_loop(0, qi, pair, 0)

        jd = 2 * qi
        sb_scr[half:tq, :] = scores(qi, jd + 1, half, half)
        update(jd, sa_scr[0:half, :], 0, half, True)
        sa_next_scr[...] = scores(jnp.minimum(qi + 1, nq - 1), 0, 0, tq)
        update(jd, sa_scr[half:tq, :], half, half, False)
        update(jd + 1, sb_scr[half:tq, :], half, half, True)

        rows = pl.ds(pl.multiple_of(qi * tq, tq), tq)
        o = acc_scr[...] / jnp.sum(l_scr[...], axis=-1, keepdims=True)
        o_ref[rows, :] = (o * sz_ref[rows, :].astype(_F32)).astype(_BF16)

    sa0_scr[...] = scores(0, 0, 0, tq)

    def two_blocks(ib, carry):
        query_block(2 * ib, sa0_scr, sa1_scr)
        query_block(2 * ib + 1, sa1_scr, sa0_scr)
        return carry

    lax.fori_loop(0, nq // 2, two_blocks, 0)


def _attention(online_max, q, k, v, sz):
    bsz, _, seq, _ = q.shape
    tq = ATTN_TQ
    per_head = lambda width: pl.BlockSpec((1, 1, seq, width), lambda b, h: (b, h, 0, 0))
    return pl.pallas_call(
        functools.partial(_attn_body, online_max),
        grid=(bsz, N_HEADS),
        in_specs=[
            per_head(HEAD_SLAB),
            pl.BlockSpec((1, 1, HEAD_SLAB, seq), lambda b, h: (b, h, 0, 0)),
            per_head(V_HEAD),
            pl.BlockSpec((seq, V_HEAD), lambda b, h: (b, h)),
        ],
        out_specs=pl.BlockSpec((seq, V_HEAD), lambda b, h: (b, h)),
        out_shape=jax.ShapeDtypeStruct((bsz * seq, D_ATTN), _BF16),
        scratch_shapes=[
            pltpu.VMEM((tq, ATTN_TK), _F32),
            pltpu.VMEM((tq, ATTN_TK), _F32),
            pltpu.VMEM((tq, ATTN_TK), _F32),
            pltpu.VMEM((tq, LANES), _F32),
            pltpu.VMEM((tq, LANES), _F32),
            pltpu.VMEM((tq, V_HEAD), _F32),
        ],
        compiler_params=pltpu.CompilerParams(
            dimension_semantics=("arbitrary", "arbitrary"), vmem_limit_bytes=VMEM_LIMIT),
        name="attention_online_max" if online_max else "attention_shifted",
    )(q, k, v, sz)


def _out_proj_body(yc_ref, ya_ref, x_ref, mod_ref, w_ref, o_ref):
    y = _dot(yc_ref[...], w_ref[0:D_CONV, :]) + _dot(ya_ref[...], w_ref[D_CONV:D_MIX, :])
    o_ref[...] = x_ref[...] + mod_ref[0, 2:3, :] * y


def _out_proj(yconv, yattn, x2d, mod3, wout, seq):
    t = x2d.shape[0]
    tm = OUT_TM
    tpb = seq // tm
    return pl.pallas_call(
        _out_proj_body,
        grid=(t // tm,),
        in_specs=[
            pl.BlockSpec((tm, D_CONV), lambda i: (i, 0)),
            pl.BlockSpec((tm, D_ATTN), lambda i: (i, 0)),
            pl.BlockSpec((tm, D_MODEL), lambda i: (i, 0)),
            pl.BlockSpec((1, 3, D_MODEL), lambda i: (i // tpb, 0, 0)),
            _resident((D_MIX, D_MODEL)),
        ],
        out_specs=pl.BlockSpec((tm, D_MODEL), lambda i: (i, 0)),
        out_shape=jax.ShapeDtypeStruct((t, D_MODEL), _F32),
        compiler_params=pltpu.CompilerParams(
            dimension_semantics=("arbitrary",), vmem_limit_bytes=VMEM_LIMIT),
        name="out_proj",
    )(yconv, yattn, x2d, mod3, wout)


def _score_shift(q_g, k_g, qscale):
    bound = QK_HEAD * qscale * jnp.max(jnp.abs(q_g)) * jnp.max(jnp.abs(k_g))
    shift = (bound * 1.03 + 1.0).astype(_BF16).astype(_F32)
    use_shift = bound + shift <= EXP2_NORMAL_RANGE
    return jnp.where(use_shift, shift, 0.0), use_shift


def _rope_slab_cols(w):
    z = jnp.zeros(w.shape[:-1] + (HALF,), w.dtype)
    return jnp.concatenate([w[..., :HALF], z, w[..., HALF:], z], axis=-1)


def _layer(layer, x2d, mod, cos_t, sin_t, norm_g, w_in, win_all, conv_w, q_a_g, w_q_b, kv_a_g, w_kv_b, q_g, k_g,
           w_out, bsz, seq):
    wtail = jnp.concatenate([_rope_slab_cols(w_in[:, OFF_KR:OFF_ZA]), w_in[:, OFF_ZA:]], axis=-1).astype(_BF16)
    wq = w_q_b.reshape(Q_LORA, N_HEADS, QK_HEAD)
    wqb = jnp.concatenate([wq[..., :QK_NOPE], _rope_slab_cols(wq[..., QK_NOPE:])], axis=-1)
    wqb = wqb.reshape(Q_LORA, N_HEADS * HEAD_SLAB).astype(_BF16)
    wkvb = w_kv_b.astype(_BF16)
    qscale = math.log2(math.e) / math.sqrt(QK_HEAD)
    gq = jnp.concatenate([q_g[:QK_NOPE], _rope_slab_cols(q_g[QK_NOPE:])]).reshape(1, HEAD_SLAB) * qscale
    gk = jnp.concatenate([k_g[:QK_NOPE], _rope_slab_cols(k_g[QK_NOPE:])]).reshape(1, HEAD_SLAB)
    mod3 = mod.reshape(bsz, 3, D_MODEL)
    shift, use_shift = _score_shift(q_g, k_g, qscale)
    pad_lane = jnp.arange(ROPE_SLAB) == SHIFT_LANE
    qb = jnp.where(pad_lane, -shift, 0.0).reshape(1, ROPE_SLAB)
    kb = jnp.where(pad_lane, 1.0, 0.0).reshape(1, ROPE_SLAB)

    ng = norm_g.reshape(1, D_MODEL)
    yconv = _conv_proj(layer, x2d, mod3, ng, win_all, conv_w, seq)
    sz, q, k, v = _mla_proj(
        layer, x2d, mod3, ng, win_all, wtail, q_a_g.reshape(1, Q_LORA), wqb,
        kv_a_g.reshape(1, KV_LORA), wkvb, gq, gk, qb, kb, cos_t, sin_t, bsz, seq)
    yattn = lax.cond(use_shift, functools.partial(_attention, False), functools.partial(_attention, True),
                     q, k, v, sz)
    return _out_proj(yconv, yattn, x2d, mod3, w_out.astype(_BF16), seq)


def kernel(x, c, positions, ada_w, ada_b, norm_g, w_in, conv_w, q_a_g, w_q_b, kv_a_g, w_kv_b, q_g, k_g, w_out):
    bsz, seq, _ = x.shape
    depth = ada_w.shape[0]
    cos_t, sin_t = _rope_tables(positions)
    x2d = x.reshape(bsz * seq, D_MODEL)
    win_all = w_in.astype(_BF16)
    for l in range(depth):
        mod = _adaln_mod(c, ada_w[l], ada_b[l])
        x2d = _layer(l, x2d, mod, cos_t, sin_t, norm_g[l], w_in[l], win_all, conv_w[l], q_a_g[l], w_q_b[l],
                     kv_a_g[l], w_kv_b[l], q_g[l], k_g[l], w_out[l], bsz, seq)
    return x2d.reshape(bsz, seq, D_MODEL)
```

```python
import functools
import math

import jax
import jax.numpy as jnp
from jax import lax
from jax.experimental import pallas as pl
from jax.experimental.pallas import tpu as pltpu

D_MODEL = 2048
D_CONV = 1024
CONV_WIDTH = 3
N_HEADS = 8
QK_NOPE = 128
QK_ROPE = 64
QK_HEAD = QK_NOPE + QK_ROPE
V_HEAD = 128
D_ATTN = N_HEADS * V_HEAD
Q_LORA = 512
KV_LORA = 256
ROPE_BASE = 10000.0
D_MIX = D_CONV + D_ATTN
EPS = 1e-6

KV_HEAD = QK_NOPE + V_HEAD
MXU_N = 256
LANES = 128
SUBLANES = 8
ROPE_SLAB = LANES
HEAD_SLAB = QK_NOPE + ROPE_SLAB
HALF = QK_ROPE // 2
SHIFT_LANE = HALF
EXP2_NORMAL_RANGE = 120.0

OFF_XC = 0
OFF_BC = D_CONV
OFF_CC = 2 * D_CONV
OFF_ZC = 3 * D_CONV
OFF_CQ = 4 * D_CONV
OFF_CKV = OFF_CQ + Q_LORA
OFF_KR = OFF_CKV + KV_LORA
OFF_ZA = OFF_KR + QK_ROPE
TAIL_COLS = ROPE_SLAB + D_ATTN

ADA_TN = 512
ROPE_TT = 1024
CONV_TM = 512
MLA_TM = 512
CONV_CW = 256
HEAD_GROUP = 4
ATTN_TQ = 1024
ATTN_TK = 512
OUT_TM = 512
VMEM_LIMIT = 56 * 1024 * 1024

_BF16 = jnp.bfloat16
_F32 = jnp.float32


def _sigmoid(z):
    return 1.0 / (1.0 + jnp.exp(-z))


def _dot(a, b):
    return jnp.dot(a, b, preferred_element_type=_F32)


def _adaln_body(ct_ref, w_ref, b_ref, o_ref):
    ct = ct_ref[...]
    sc = ct * _sigmoid(ct)
    w = w_ref[...]
    rows = [jnp.sum(w * sc[:, b:b + 1], axis=0, keepdims=True) for b in range(ct.shape[1])]
    o_ref[...] = jnp.concatenate(rows, axis=0) + b_ref[...]


def _adaln_mod(c, ada_w, ada_b):
    bsz = c.shape[0]
    n = ada_w.shape[1]
    return pl.pallas_call(
        _adaln_body,
        grid=(n // ADA_TN,),
        in_specs=[
            pl.BlockSpec((D_MODEL, bsz), lambda j: (0, 0)),
            pl.BlockSpec((D_MODEL, ADA_TN), lambda j: (0, j)),
            pl.BlockSpec((1, ADA_TN), lambda j: (0, j)),
        ],
        out_specs=pl.BlockSpec((bsz, ADA_TN), lambda j: (0, j)),
        out_shape=jax.ShapeDtypeStruct((bsz, n), _F32),
        name="adaln_mod",
    )(c.T, ada_w, ada_b.reshape(1, n))


def _rope_body(pos_ref, f_ref, cos_ref, sin_ref):
    ang = f_ref[...] * pos_ref[...].astype(_F32)
    c = jnp.cos(ang)
    s = jnp.sin(ang)
    z = jnp.zeros_like(c)
    cos_ref[...] = jnp.concatenate([c, z, c, z], axis=0).T
    sin_ref[...] = jnp.concatenate([-s, z, s, z], axis=0).T


def _rope_tables(positions):
    t = positions.size
    inv_freq = ROPE_BASE ** (-jnp.arange(0, QK_ROPE, 2, dtype=_F32) / QK_ROPE)
    out = pl.BlockSpec((ROPE_TT, ROPE_SLAB), lambda i: (i, 0))
    return pl.pallas_call(
        _rope_body,
        grid=(t // ROPE_TT,),
        in_specs=[pl.BlockSpec((1, ROPE_TT), lambda i: (0, i)), pl.BlockSpec((HALF, 1), lambda i: (0, 0))],
        out_specs=[out, out],
        out_shape=[jax.ShapeDtypeStruct((t, ROPE_SLAB), _F32)] * 2,
        name="rope_tables",
    )(positions.reshape(1, t), inv_freq.reshape(HALF, 1))


def _modulated_norm(x_ref, mod_ref, ng_ref):
    x = x_ref[...]
    xn = x * lax.rsqrt(jnp.mean(x * x, axis=-1, keepdims=True) + EPS)
    a = ng_ref[...] * (1.0 + mod_ref[0, 1:2, :])
    return (xn * a + mod_ref[0, 0:1, :]).astype(_BF16)


def _conv_proj_body(tiles_per_batch, x_ref, mod_ref, ng_ref, win_ref, convw_ref, yconv_ref,
                    h_scr, carry_scr, vbuf_scr):
    tm = x_ref.shape[0]

    @pl.when(pl.program_id(0) % tiles_per_batch == 0)
    def _():
        carry_scr[...] = jnp.zeros_like(carry_scr)

    h_scr[...] = _modulated_norm(x_ref, mod_ref, ng_ref)

    def proj(off, width):
        return _dot(h_scr[...], win_ref[:, off:off + width])

    for lo in range(0, D_CONV, CONV_CW):
        xc = proj(OFF_XC + lo, CONV_CW)
        bc = proj(OFF_BC + lo, CONV_CW)
        cc = proj(OFF_CC + lo, CONV_CW)
        zc = proj(OFF_ZC + lo, CONV_CW)
        v0 = cc * xc
        vbuf_scr[0:SUBLANES, :] = carry_scr[:, lo:lo + CONV_CW]
        vbuf_scr[SUBLANES:SUBLANES + tm, :] = v0
        carry_scr[:, lo:lo + CONV_CW] = v0[tm - SUBLANES:, :]
        v1 = vbuf_scr[SUBLANES - 1:SUBLANES - 1 + tm, :]
        v2 = vbuf_scr[SUBLANES - 2:SUBLANES - 2 + tm, :]
        w = convw_ref[:, lo:lo + CONV_CW]
        conv = w[0:1, :] * v2 + w[1:2, :] * v1 + w[2:3, :] * v0
        yconv_ref[:, lo:lo + CONV_CW] = (bc * conv * (zc * _sigmoid(zc))).astype(_BF16)


def _resident(shape, index=None):
    index = (0,) * len(shape) if index is None else index
    return pl.BlockSpec(shape, lambda i: index, pipeline_mode=pl.Buffered(1))


def _conv_proj(layer, x2d, mod3, norm_g, win_all, conv_w, seq):
    t = x2d.shape[0]
    tm = CONV_TM
    tpb = seq // tm
    return pl.pallas_call(
        functools.partial(_conv_proj_body, tpb),
        grid=(t // tm,),
        in_specs=[
            pl.BlockSpec((tm, D_MODEL), lambda i: (i, 0)),
            pl.BlockSpec((1, 3, D_MODEL), lambda i: (i // tpb, 0, 0)),
            _resident((1, D_MODEL)),
            _resident((None, D_MODEL, 4 * D_CONV), (layer, 0, 0)),
            _resident((CONV_WIDTH, D_CONV)),
        ],
        out_specs=pl.BlockSpec((tm, D_CONV), lambda i: (i, 0)),
        out_shape=jax.ShapeDtypeStruct((t, D_CONV), _BF16),
        scratch_shapes=[
            pltpu.VMEM((tm, D_MODEL), _BF16),
            pltpu.VMEM((SUBLANES, D_CONV), _F32),
            pltpu.VMEM((SUBLANES + tm, CONV_CW), _F32),
        ],
        compiler_params=pltpu.CompilerParams(
            dimension_semantics=("arbitrary",), vmem_limit_bytes=VMEM_LIMIT),
        name="conv_proj",
    )(x2d, mod3, norm_g, win_all, conv_w)


def _mla_proj_body(x_ref, mod_ref, ng_ref, wcq_ref, wckv_ref, wtail_ref, qag_ref, wqb_ref, kvag_ref, wkvb_ref,
                   gq_ref, gk_ref, qb_ref, kb_ref, cos_ref, sin_ref,
                   sz_ref, q_ref, kt_ref, v_ref, h_scr, cqn_scr, ckvn_scr):
    h_scr[...] = _modulated_norm(x_ref, mod_ref, ng_ref)
    cos_t = cos_ref[...]
    sin_t = sin_ref[...]

    def rope(slab):
        return slab * cos_t + pltpu.roll(slab, ROPE_SLAB // 2, axis=1) * sin_t

    def rms(parts, width):
        ss = sum(jnp.sum(p * p, axis=-1, keepdims=True) for p in parts)
        return lax.rsqrt(ss * (1.0 / width) + EPS)

    cq = _dot(h_scr[...], wcq_ref[...])
    cqn_scr[...] = (cq * rms([cq], Q_LORA) * qag_ref[...]).astype(_BF16)
    ckv = _dot(h_scr[...], wckv_ref[...])
    ckvn_scr[...] = (ckv * rms([ckv], KV_LORA) * kvag_ref[...]).astype(_BF16)

    kr = _dot(h_scr[...], wtail_ref[:, 0:ROPE_SLAB])
    gk = gk_ref[...]
    gq = gq_ref[...]
    kr_rot = rope(kr * gk[:, QK_NOPE:HEAD_SLAB])

    def lane_sum(sq):
        ones = jnp.ones((sq.shape[1], LANES), _BF16)
        return _dot(sq.astype(_BF16), ones)

    ss_r = lane_sum(kr * kr)
    gate_cols = D_ATTN // (N_HEADS // HEAD_GROUP)
    for g0 in range(0, N_HEADS, HEAD_GROUP):
        heads = range(g0, g0 + HEAD_GROUP)
        slabs = [_dot(cqn_scr[...], wqb_ref[:, hd * HEAD_SLAB:(hd + 1) * HEAD_SLAB]) for hd in heads]
        kvs = [_dot(ckvn_scr[...], wkvb_ref[:, hd * KV_HEAD:(hd + 1) * KV_HEAD]) for hd in heads]
        ssq = [lane_sum(s * s) for s in slabs]
        ssk = [lane_sum(kv[:, 0:QK_NOPE] * kv[:, 0:QK_NOPE]) for kv in kvs]
        glo = (g0 // HEAD_GROUP) * gate_cols
        zas = [(lo, _dot(h_scr[...], wtail_ref[:, ROPE_SLAB + lo:ROPE_SLAB + lo + 2 * MXU_N]))
               for lo in range(glo, glo + gate_cols, 2 * MXU_N)]
        for hd, slab, kv, sq, sk in zip(heads, slabs, kvs, ssq, ssk):
            r = lax.rsqrt(sq * (1.0 / QK_HEAD) + EPS)
            q_ref[0, hd, :, 0:QK_NOPE] = (slab[:, 0:QK_NOPE] * r * gq[:, 0:QK_NOPE]).astype(_BF16)
            q_rope = rope(slab[:, QK_NOPE:HEAD_SLAB] * r * gq[:, QK_NOPE:HEAD_SLAB])
            q_ref[0, hd, :, QK_NOPE:HEAD_SLAB] = (q_rope + qb_ref[...]).astype(_BF16)
            r = lax.rsqrt((sk + ss_r) * (1.0 / QK_HEAD) + EPS)
            kt_ref[0, hd, 0:QK_NOPE, :] = (kv[:, 0:QK_NOPE] * r * gk[:, 0:QK_NOPE]).T.astype(_BF16)
            kt_ref[0, hd, QK_NOPE:HEAD_SLAB, :] = (kr_rot * r + kb_ref[...]).T.astype(_BF16)
            v_ref[0, hd, :, :] = kv[:, QK_NOPE:KV_HEAD].astype(_BF16)
        for lo, za in zas:
            sz_ref[:, lo:lo + 2 * MXU_N] = (za * _sigmoid(za)).astype(_BF16)


def _mla_proj(layer, x2d, mod3, norm_g, win_all, wtail, q_a_g, wqb, kv_a_g, wkvb, gq, gk, qb, kb, cos_t, sin_t,
              bsz, seq):
    t = x2d.shape[0]
    tm = MLA_TM
    tpb = seq // tm
    tok = lambda width: pl.BlockSpec((tm, width), lambda i: (i, 0))
    head = lambda width: pl.BlockSpec((1, N_HEADS, tm, width), lambda i: (i // tpb, 0, i % tpb, 0))
    return pl.pallas_call(
        _mla_proj_body,
        grid=(t // tm,),
        in_specs=[
            tok(D_MODEL),
            pl.BlockSpec((1, 3, D_MODEL), lambda i: (i // tpb, 0, 0)),
            _resident((1, D_MODEL)),
            _resident((None, D_MODEL, Q_LORA), (layer, 0, OFF_CQ // Q_LORA)),
            _resident((None, D_MODEL, KV_LORA), (layer, 0, OFF_CKV // KV_LORA)),
            _resident((D_MODEL, TAIL_COLS)),
            _resident((1, Q_LORA)),
            _resident((Q_LORA, N_HEADS * HEAD_SLAB)),
            _resident((1, KV_LORA)),
            _resident((KV_LORA, N_HEADS * (QK_NOPE + V_HEAD))),
            _resident((1, HEAD_SLAB)),
            _resident((1, HEAD_SLAB)),
            _resident((1, ROPE_SLAB)),
            _resident((1, ROPE_SLAB)),
            tok(ROPE_SLAB),
            tok(ROPE_SLAB),
        ],
        out_specs=[
            tok(D_ATTN),
            head(HEAD_SLAB),
            pl.BlockSpec((1, N_HEADS, HEAD_SLAB, tm), lambda i: (i // tpb, 0, 0, i % tpb)),
            head(V_HEAD),
        ],
        out_shape=[
            jax.ShapeDtypeStruct((t, D_ATTN), _BF16),
            jax.ShapeDtypeStruct((bsz, N_HEADS, seq, HEAD_SLAB), _BF16),
            jax.ShapeDtypeStruct((bsz, N_HEADS, HEAD_SLAB, seq), _BF16),
            jax.ShapeDtypeStruct((bsz, N_HEADS, seq, V_HEAD), _BF16),
        ],
        scratch_shapes=[
            pltpu.VMEM((tm, D_MODEL), _BF16),
            pltpu.VMEM((tm, Q_LORA), _BF16),
            pltpu.VMEM((tm, KV_LORA), _BF16),
        ],
        compiler_params=pltpu.CompilerParams(
            dimension_semantics=("arbitrary",), vmem_limit_bytes=VMEM_LIMIT),
        name="mla_proj",
    )(x2d, mod3, norm_g, win_all, win_all, wtail, q_a_g, wqb, kv_a_g, wkvb, gq, gk, qb, kb, cos_t, sin_t)


def _attn_body(online_max, q_ref, kt_ref, v_ref, sz_ref, o_ref,
               sa0_scr, sa1_scr, sb_scr, m_scr, l_scr, acc_scr):
    tq = ATTN_TQ
    tk = ATTN_TK
    half = tq // 2
    assert tq == 2 * tk
    nq = q_ref.shape[2] // tq
    assert nq % 2 == 0

    def scores(qi, j, r0, nr):
        qstart = pl.multiple_of(qi * tq + r0, half)
        kstart = pl.multiple_of(j * tk, tk)
        return _dot(q_ref[0, 0, pl.ds(qstart, nr), :], kt_ref[0, 0, :, pl.ds(kstart, tk)])

    def update(j, s, r0, nr, triangular):
        start = pl.multiple_of(j * tk, tk)
        v = v_ref[0, 0, pl.ds(start, tk), :]
        if triangular:
            row = lax.broadcasted_iota(jnp.int32, (nr, tk), 0)
            col = lax.broadcasted_iota(jnp.int32, (nr, tk), 1)
            s = jnp.where(col <= row, s, -jnp.inf)
        tiles = [s[:, t * LANES:(t + 1) * LANES] for t in range(tk // LANES)]
        if not online_max:
            ps = [jnp.exp2(t) for t in tiles]
            l_scr[r0:r0 + nr, :] += functools.reduce(jnp.add, ps)
            p = jnp.concatenate(ps, axis=-1).astype(_BF16)
            acc_scr[r0:r0 + nr, :] += _dot(p, v)
            return
        mx = functools.reduce(jnp.maximum, tiles)
        m_prev = m_scr[r0:r0 + nr, :]
        m_new = jnp.maximum(m_prev, jnp.max(mx, axis=-1, keepdims=True))
        alpha = jnp.exp2(m_prev - m_new)
        ps = [jnp.exp2(t - m_new) for t in tiles]
        l_scr[r0:r0 + nr, :] = alpha * l_scr[r0:r0 + nr, :] + functools.reduce(jnp.add, ps)
        p = jnp.concatenate(ps, axis=-1).astype(_BF16)
        acc_scr[r0:r0 + nr, :] = alpha * acc_scr[r0:r0 + nr, :] + _dot(p, v)
        m_scr[r0:r0 + nr, :] = m_new

    def query_block(qi, sa_scr, sa_next_scr):
        if online_max:
            m_scr[...] = jnp.full_like(m_scr, -jnp.inf)
        l_scr[...] = jnp.zeros_like(l_scr)
        acc_scr[...] = jnp.zeros_like(acc_scr)

        def pair(jj, carry):
            j = 2 * jj
            sb_scr[...] = scores(qi, j + 1, 0, tq)
            update(j, sa_scr[...], 0, tq, False)
            sa_scr[...] = scores(qi, j + 2, 0, tq)
            update(j + 1, sb_scr[...], 0, tq, False)
            return carry

        lax.fori_loop(0, qi, pair, 0)

        jd = 2 * qi
        sb_scr[half:tq, :] = scores(qi, jd + 1, half, half)
        update(jd, sa_scr[0:half, :], 0, half, True)
        sa_next_scr[...] = scores(jnp.minimum(qi + 1, nq - 1), 0, 0, tq)
        update(jd, sa_scr[half:tq, :], half, half, False)
        update(jd + 1, sb_scr[half:tq, :], half, half, True)

        rows = pl.ds(pl.multiple_of(qi * tq, tq), tq)
        o = acc_scr[...] / jnp.sum(l_scr[...], axis=-1, keepdims=True)
        o_ref[rows, :] = (o * sz_ref[rows, :].astype(_F32)).astype(_BF16)

    sa0_scr[...] = scores(0, 0, 0, tq)

    def two_blocks(ib, carry):
        query_block(2 * ib, sa0_scr, sa1_scr)
        query_block(2 * ib + 1, sa1_scr, sa0_scr)
        return carry

    lax.fori_loop(0, nq // 2, two_blocks, 0)


def _attention(online_max, q, k, v, sz):
    bsz, _, seq, _ = q.shape
    tq = ATTN_TQ
    per_head = lambda width: pl.BlockSpec((1, 1, seq, width), lambda b, h: (b, h, 0, 0))
    return pl.pallas_call(
        functools.partial(_attn_body, online_max),
        grid=(bsz, N_HEADS),
        in_specs=[
            per_head(HEAD_SLAB),
            pl.BlockSpec((1, 1, HEAD_SLAB, seq), lambda b, h: (b, h, 0, 0)),
            per_head(V_HEAD),
            pl.BlockSpec((seq, V_HEAD), lambda b, h: (b, h)),
        ],
        out_specs=pl.BlockSpec((seq, V_HEAD), lambda b, h: (b, h)),
        out_shape=jax.ShapeDtypeStruct((bsz * seq, D_ATTN), _BF16),
        scratch_shapes=[
            pltpu.VMEM((tq, ATTN_TK), _F32),
            pltpu.VMEM((tq, ATTN_TK), _F32),
            pltpu.VMEM((tq, ATTN_TK), _F32),
            pltpu.VMEM((tq, LANES), _F32),
            pltpu.VMEM((tq, LANES), _F32),
            pltpu.VMEM((tq, V_HEAD), _F32),
        ],
        compiler_params=pltpu.CompilerParams(
            dimension_semantics=("arbitrary", "arbitrary"), vmem_limit_bytes=VMEM_LIMIT),
        name="attention_online_max" if online_max else "attention_shifted",
    )(q, k, v, sz)


def _out_proj_body(yc_ref, ya_ref, x_ref, mod_ref, w_ref, o_ref):
    y = _dot(yc_ref[...], w_ref[0:D_CONV, :]) + _dot(ya_ref[...], w_ref[D_CONV:D_MIX, :])
    o_ref[...] = x_ref[...] + mod_ref[0, 2:3, :] * y


def _out_proj(yconv, yattn, x2d, mod3, wout, seq):
    t = x2d.shape[0]
    tm = OUT_TM
    tpb = seq // tm
    return pl.pallas_call(
        _out_proj_body,
        grid=(t // tm,),
        in_specs=[
            pl.BlockSpec((tm, D_CONV), lambda i: (i, 0)),
            pl.BlockSpec((tm, D_ATTN), lambda i: (i, 0)),
            pl.BlockSpec((tm, D_MODEL), lambda i: (i, 0)),
            pl.BlockSpec((1, 3, D_MODEL), lambda i: (i // tpb, 0, 0)),
            _resident((D_MIX, D_MODEL)),
        ],
        out_specs=pl.BlockSpec((tm, D_MODEL), lambda i: (i, 0)),
        out_shape=jax.ShapeDtypeStruct((t, D_MODEL), _F32),
        compiler_params=pltpu.CompilerParams(
            dimension_semantics=("arbitrary",), vmem_limit_bytes=VMEM_LIMIT),
        name="out_proj",
    )(yconv, yattn, x2d, mod3, wout)


def _score_shift(q_g, k_g, qscale):
    bound = QK_HEAD * qscale * jnp.max(jnp.abs(q_g)) * jnp.max(jnp.abs(k_g))
    shift = (bound * 1.03 + 1.0).astype(_BF16).astype(_F32)
    use_shift = bound + shift <= EXP2_NORMAL_RANGE
    return jnp.where(use_shift, shift, 0.0), use_shift


def _rope_slab_cols(w):
    z = jnp.zeros(w.shape[:-1] + (HALF,), w.dtype)
    return jnp.concatenate([w[..., :HALF], z, w[..., HALF:], z], axis=-1)


def _layer(layer, x2d, mod, cos_t, sin_t, norm_g, w_in, win_all, conv_w, q_a_g, w_q_b, kv_a_g, w_kv_b, q_g, k_g,
           w_out, bsz, seq):
    wtail = jnp.concatenate([_rope_slab_cols(w_in[:, OFF_KR:OFF_ZA]), w_in[:, OFF_ZA:]], axis=-1).astype(_BF16)
    wq = w_q_b.reshape(Q_LORA, N_HEADS, QK_HEAD)
    wqb = jnp.concatenate([wq[..., :QK_NOPE], _rope_slab_cols(wq[..., QK_NOPE:])], axis=-1)
    wqb = wqb.reshape(Q_LORA, N_HEADS * HEAD_SLAB).astype(_BF16)
    wkvb = w_kv_b.astype(_BF16)
    qscale = math.log2(math.e) / math.sqrt(QK_HEAD)
    gq = jnp.concatenate([q_g[:QK_NOPE], _rope_slab_cols(q_g[QK_NOPE:])]).reshape(1, HEAD_SLAB) * qscale
    gk = jnp.concatenate([k_g[:QK_NOPE], _rope_slab_cols(k_g[QK_NOPE:])]).reshape(1, HEAD_SLAB)
    mod3 = mod.reshape(bsz, 3, D_MODEL)
    shift, use_shift = _score_shift(q_g, k_g, qscale)
    pad_lane = jnp.arange(ROPE_SLAB) == SHIFT_LANE
    qb = jnp.where(pad_lane, -shift, 0.0).reshape(1, ROPE_SLAB)
    kb = jnp.where(pad_lane, 1.0, 0.0).reshape(1, ROPE_SLAB)

    ng = norm_g.reshape(1, D_MODEL)
    yconv = _conv_proj(layer, x2d, mod3, ng, win_all, conv_w, seq)
    sz, q, k, v = _mla_proj(
        layer, x2d, mod3, ng, win_all, wtail, q_a_g.reshape(1, Q_LORA), wqb,
        kv_a_g.reshape(1, KV_LORA), wkvb, gq, gk, qb, kb, cos_t, sin_t, bsz, seq)
    yattn = lax.cond(use_shift, functools.partial(_attention, False), functools.partial(_attention, True),
                     q, k, v, sz)
    return _out_proj(yconv, yattn, x2d, mod3, w_out.astype(_BF16), seq)


def kernel(x, c, positions, ada_w, ada_b, norm_g, w_in, conv_w, q_a_g, w_q_b, kv_a_g, w_kv_b, q_g, k_g, w_out):
    bsz, seq, _ = x.shape
    depth = ada_w.shape[0]
    cos_t, sin_t = _rope_tables(positions)
    x2d = x.reshape(bsz * seq, D_MODEL)
    win_all = w_in.astype(_BF16)
    for l in range(depth):
        mod = _adaln_mod(c, ada_w[l], ada_b[l])
        x2d = _layer(l, x2d, mod, cos_t, sin_t, norm_g[l], w_in[l], win_all, conv_w[l], q_a_g[l], w_q_b[l],
                     kv_a_g[l], w_kv_b[l], q_g[l], k_g[l], w_out[l], bsz, seq)
    return x2d.reshape(bsz, seq, D_MODEL)
```

```python
import functools
import math

import jax
import jax.numpy as jnp
from jax import lax
from jax.experimental import pallas as pl
from jax.experimental.pallas import tpu as pltpu

D_MODEL = 2048
D_CONV = 1024
CONV_WIDTH = 3
N_HEADS = 8
QK_NOPE = 128
QK_ROPE = 64
QK_HEAD = QK_NOPE + QK_ROPE
V_HEAD = 128
D_ATTN = N_HEADS * V_HEAD
Q_LORA = 512
KV_LORA = 256
ROPE_BASE = 10000.0
D_MIX = D_CONV + D_ATTN
EPS = 1e-6

KV_HEAD = QK_NOPE + V_HEAD
MXU_N = 256
LANES = 128
SUBLANES = 8
ROPE_SLAB = LANES
HEAD_SLAB = QK_NOPE + ROPE_SLAB
HALF = QK_ROPE // 2
SHIFT_LANE = HALF
EXP2_NORMAL_RANGE = 120.0

OFF_XC = 0
OFF_BC = D_CONV
OFF_CC = 2 * D_CONV
OFF_ZC = 3 * D_CONV
OFF_CQ = 4 * D_CONV
OFF_CKV = OFF_CQ + Q_LORA
OFF_KR = OFF_CKV + KV_LORA
OFF_ZA = OFF_KR + QK_ROPE
TAIL_COLS = ROPE_SLAB + D_ATTN

ADA_TN = 512
ROPE_TT = 1024
CONV_TM = 512
MLA_TM = 512
CONV_CW = 256
HEAD_GROUP = 4
ATTN_TQ = 1024
ATTN_TK = 512
OUT_TM = 512
VMEM_LIMIT = 56 * 1024 * 1024

_BF16 = jnp.bfloat16
_F32 = jnp.float32


def _sigmoid(z):
    return 1.0 / (1.0 + jnp.exp(-z))


def _dot(a, b):
    return jnp.dot(a, b, preferred_element_type=_F32)


def _adaln_body(ct_ref, w_ref, b_ref, o_ref):
    ct = ct_ref[...]
    sc = ct * _sigmoid(ct)
    w = w_ref[...]
    rows = [jnp.sum(w * sc[:, b:b + 1], axis=0, keepdims=True) for b in range(ct.shape[1])]
    o_ref[...] = jnp.concatenate(rows, axis=0) + b_ref[...]


def _adaln_mod(c, ada_w, ada_b):
    bsz = c.shape[0]
    n = ada_w.shape[1]
    return pl.pallas_call(
        _adaln_body,
        grid=(n // ADA_TN,),
        in_specs=[
            pl.BlockSpec((D_MODEL, bsz), lambda j: (0, 0)),
            pl.BlockSpec((D_MODEL, ADA_TN), lambda j: (0, j)),
            pl.BlockSpec((1, ADA_TN), lambda j: (0, j)),
        ],
        out_specs=pl.BlockSpec((bsz, ADA_TN), lambda j: (0, j)),
        out_shape=jax.ShapeDtypeStruct((bsz, n), _F32),
        name="adaln_mod",
    )(c.T, ada_w, ada_b.reshape(1, n))


def _rope_body(pos_ref, f_ref, cos_ref, sin_ref):
    ang = f_ref[...] * pos_ref[...].astype(_F32)
    c = jnp.cos(ang)
    s = jnp.sin(ang)
    z = jnp.zeros_like(c)
    cos_ref[...] = jnp.concatenate([c, z, c, z], axis=0).T
    sin_ref[...] = jnp.concatenate([-s, z, s, z], axis=0).T


def _rope_tables(positions):
    t = positions.size
    inv_freq = ROPE_BASE ** (-jnp.arange(0, QK_ROPE, 2, dtype=_F32) / QK_ROPE)
    out = pl.BlockSpec((ROPE_TT, ROPE_SLAB), lambda i: (i, 0))
    return pl.pallas_call(
        _rope_body,
        grid=(t // ROPE_TT,),
        in_specs=[pl.BlockSpec((1, ROPE_TT), lambda i: (0, i)), pl.BlockSpec((HALF, 1), lambda i: (0, 0))],
        out_specs=[out, out],
        out_shape=[jax.ShapeDtypeStruct((t, ROPE_SLAB), _F32)] * 2,
        name="rope_tables",
    )(positions.reshape(1, t), inv_freq.reshape(HALF, 1))


def _modulated_norm(x_ref, mod_ref, ng_ref):
    x = x_ref[...]
    xn = x * lax.rsqrt(jnp.mean(x * x, axis=-1, keepdims=True) + EPS)
    a = ng_ref[...] * (1.0 + mod_ref[0, 1:2, :])
    return (xn * a + mod_ref[0, 0:1, :]).astype(_BF16)


def _conv_proj_body(tiles_per_batch, x_ref, mod_ref, ng_ref, win_ref, convw_ref, yconv_ref,
                    h_scr, carry_scr, vbuf_scr):
    tm = x_ref.shape[0]

    @pl.when(pl.program_id(0) % tiles_per_batch == 0)
    def _():
        carry_scr[...] = jnp.zeros_like(carry_scr)

    h_scr[...] = _modulated_norm(x_ref, mod_ref, ng_ref)

    def proj(off, width):
        return _dot(h_scr[...], win_ref[:, off:off + width])

    for lo in range(0, D_CONV, CONV_CW):
        xc = proj(OFF_XC + lo, CONV_CW)
        bc = proj(OFF_BC + lo, CONV_CW)
        cc = proj(OFF_CC + lo, CONV_CW)
        zc = proj(OFF_ZC + lo, CONV_CW)
        v0 = cc * xc
        vbuf_scr[0:SUBLANES, :] = carry_scr[:, lo:lo + CONV_CW]
        vbuf_scr[SUBLANES:SUBLANES + tm, :] = v0
        carry_scr[:, lo:lo + CONV_CW] = v0[tm - SUBLANES:, :]
        v1 = vbuf_scr[SUBLANES - 1:SUBLANES - 1 + tm, :]
        v2 = vbuf_scr[SUBLANES - 2:SUBLANES - 2 + tm, :]
        w = convw_ref[:, lo:lo + CONV_CW]
        conv = w[0:1, :] * v2 + w[1:2, :] * v1 + w[2:3, :] * v0
        yconv_ref[:, lo:lo + CONV_CW] = (bc * conv * (zc * _sigmoid(zc))).astype(_BF16)


def _resident(shape, index=None):
    index = (0,) * len(shape) if index is None else index
    return pl.BlockSpec(shape, lambda i: index, pipeline_mode=pl.Buffered(1))


def _conv_proj(layer, x2d, mod3, norm_g, win_all, conv_w, seq):
    t = x2d.shape[0]
    tm = CONV_TM
    tpb = seq // tm
    return pl.pallas_call(
        functools.partial(_conv_proj_body, tpb),
        grid=(t // tm,),
        in_specs=[
            pl.BlockSpec((tm, D_MODEL), lambda i: (i, 0)),
            pl.BlockSpec((1, 3, D_MODEL), lambda i: (i // tpb, 0, 0)),
            _resident((1, D_MODEL)),
            _resident((None, D_MODEL, 4 * D_CONV), (layer, 0, 0)),
            _resident((CONV_WIDTH, D_CONV)),
        ],
        out_specs=pl.BlockSpec((tm, D_CONV), lambda i: (i, 0)),
        out_shape=jax.ShapeDtypeStruct((t, D_CONV), _BF16),
        scratch_shapes=[
            pltpu.VMEM((tm, D_MODEL), _BF16),
            pltpu.VMEM((SUBLANES, D_CONV), _F32),
            pltpu.VMEM((SUBLANES + tm, CONV_CW), _F32),
        ],
        compiler_params=pltpu.CompilerParams(
            dimension_semantics=("arbitrary",), vmem_limit_bytes=VMEM_LIMIT),
        name="conv_proj",
    )(x2d, mod3, norm_g, win_all, conv_w)


def _mla_proj_body(x_ref, mod_ref, ng_ref, wcq_ref, wckv_ref, wtail_ref, qag_ref, wqb_ref, kvag_ref, wkvb_ref,
                   gq_ref, gk_ref, qb_ref, kb_ref, cos_ref, sin_ref,
                   sz_ref, q_ref, kt_ref, v_ref, h_scr, cqn_scr, ckvn_scr):
    h_scr[...] = _modulated_norm(x_ref, mod_ref, ng_ref)
    cos_t = cos_ref[...]
    sin_t = sin_ref[...]

    def rope(slab):
        return slab * cos_t + pltpu.roll(slab, ROPE_SLAB // 2, axis=1) * sin_t

    def rms(parts, width):
        ss = sum(jnp.sum(p * p, axis=-1, keepdims=True) for p in parts)
        return lax.rsqrt(ss * (1.0 / width) + EPS)

    cq = _dot(h_scr[...], wcq_ref[...])
    cqn_scr[...] = (cq * rms([cq], Q_LORA) * qag_ref[...]).astype(_BF16)
    ckv = _dot(h_scr[...], wckv_ref[...])
    ckvn_scr[...] = (ckv * rms([ckv], KV_LORA) * kvag_ref[...]).astype(_BF16)

    kr = _dot(h_scr[...], wtail_ref[:, 0:ROPE_SLAB])
    gk = gk_ref[...]
    gq = gq_ref[...]
    kr_rot = rope(kr * gk[:, QK_NOPE:HEAD_SLAB])

    def lane_sum(sq):
        ones = jnp.ones((sq.shape[1], LANES), _BF16)
        return _dot(sq.astype(_BF16), ones)

    ss_r = lane_sum(kr * kr)
    gate_cols = D_ATTN // (N_HEADS // HEAD_GROUP)
    for g0 in range(0, N_HEADS, HEAD_GROUP):
        heads = range(g0, g0 + HEAD_GROUP)
        slabs = [_dot(cqn_scr[...], wqb_ref[:, hd * HEAD_SLAB:(hd + 1) * HEAD_SLAB]) for hd in heads]
        kvs = [_dot(ckvn_scr[...], wkvb_ref[:, hd * KV_HEAD:(hd + 1) * KV_HEAD]) for hd in heads]
        ssq = [lane_sum(s * s) for s in slabs]
        ssk = [lane_sum(kv[:, 0:QK_NOPE] * kv[:, 0:QK_NOPE]) for kv in kvs]
        glo = (g0 // HEAD_GROUP) * gate_cols
        zas = [(lo, _dot(h_scr[...], wtail_ref[:, ROPE_SLAB + lo:ROPE_SLAB + lo + 2 * MXU_N]))
               for lo in range(glo, glo + gate_cols, 2 * MXU_N)]
        for hd, slab, kv, sq, sk in zip(heads, slabs, kvs, ssq, ssk):
            r = lax.rsqrt(sq * (1.0 / QK_HEAD) + EPS)
            q_ref[0, hd, :, 0:QK_NOPE] = (slab[:, 0:QK_NOPE] * r * gq[:, 0:QK_NOPE]).astype(_BF16)
            q_rope = rope(slab[:, QK_NOPE:HEAD_SLAB] * r * gq[:, QK_NOPE:HEAD_SLAB])
            q_ref[0, hd, :, QK_NOPE:HEAD_SLAB] = (q_rope + qb_ref[...]).astype(_BF16)
            r = lax.rsqrt((sk + ss_r) * (1.0 / QK_HEAD) + EPS)
            kt_ref[0, hd, 0:QK_NOPE, :] = (kv[:, 0:QK_NOPE] * r * gk[:, 0:QK_NOPE]).T.astype(_BF16)
            kt_ref[0, hd, QK_NOPE:HEAD_SLAB, :] = (kr_rot * r + kb_ref[...]).T.astype(_BF16)
            v_ref[0, hd, :, :] = kv[:, QK_NOPE:KV_HEAD].astype(_BF16)
        for lo, za in zas:
            sz_ref[:, lo:lo + 2 * MXU_N] = (za * _sigmoid(za)).astype(_BF16)


def _mla_proj(layer, x2d, mod3, norm_g, win_all, wtail, q_a_g, wqb, kv_a_g, wkvb, gq, gk, qb, kb, cos_t, sin_t,
              bsz, seq):
    t = x2d.shape[0]
    tm = MLA_TM
    tpb = seq // tm
    tok = lambda width: pl.BlockSpec((tm, width), lambda i: (i, 0))
    head = lambda width: pl.BlockSpec((1, N_HEADS, tm, width), lambda i: (i // tpb, 0, i % tpb, 0))
    return pl.pallas_call(
        _mla_proj_body,
        grid=(t // tm,),
        in_specs=[
            tok(D_MODEL),
            pl.BlockSpec((1, 3, D_MODEL), lambda i: (i // tpb, 0, 0)),
            _resident((1, D_MODEL)),
            _resident((None, D_MODEL, Q_LORA), (layer, 0, OFF_CQ // Q_LORA)),
            _resident((None, D_MODEL, KV_LORA), (layer, 0, OFF_CKV // KV_LORA)),
            _resident((D_MODEL, TAIL_COLS)),
            _resident((1, Q_LORA)),
            _resident((Q_LORA, N_HEADS * HEAD_SLAB)),
            _resident((1, KV_LORA)),
            _resident((KV_LORA, N_HEADS * (QK_NOPE + V_HEAD))),
            _resident((1, HEAD_SLAB)),
            _resident((1, HEAD_SLAB)),
            _resident((1, ROPE_SLAB)),
            _resident((1, ROPE_SLAB)),
            tok(ROPE_SLAB),
            tok(ROPE_SLAB),
        ],
        out_specs=[
            tok(D_ATTN),
            head(HEAD_SLAB),
            pl.BlockSpec((1, N_HEADS, HEAD_SLAB, tm), lambda i: (i // tpb, 0, 0, i % tpb)),
            head(V_HEAD),
        ],
        out_shape=[
            jax.ShapeDtypeStruct((t, D_ATTN), _BF16),
            jax.ShapeDtypeStruct((bsz, N_HEADS, seq, HEAD_SLAB), _BF16),
            jax.ShapeDtypeStruct((bsz, N_HEADS, HEAD_SLAB, seq), _BF16),
            jax.ShapeDtypeStruct((bsz, N_HEADS, seq, V_HEAD), _BF16),
        ],
        scratch_shapes=[
            pltpu.VMEM((tm, D_MODEL), _BF16),
            pltpu.VMEM((tm, Q_LORA), _BF16),
            pltpu.VMEM((tm, KV_LORA), _BF16),
        ],
        compiler_params=pltpu.CompilerParams(
            dimension_semantics=("arbitrary",), vmem_limit_bytes=VMEM_LIMIT),
        name="mla_proj",
    )(x2d, mod3, norm_g, win_all, win_all, wtail, q_a_g, wqb, kv_a_g, wkvb, gq, gk, qb, kb, cos_t, sin_t)


def _attn_body(online_max, q_ref, kt_ref, v_ref, sz_ref, o_ref,
               sa0_scr, sa1_scr, sb_scr, m_scr, l_scr, acc_scr):
    tq = ATTN_TQ
    tk = ATTN_TK
    half = tq // 2
    assert tq == 2 * tk
    nq = q_ref.shape[2] // tq
    assert nq % 2 == 0

    def scores(qi, j, r0, nr):
        qstart = pl.multiple_of(qi * tq + r0, half)
        kstart = pl.multiple_of(j * tk, tk)
        return _dot(q_ref[0, 0, pl.ds(qstart, nr), :], kt_ref[0, 0, :, pl.ds(kstart, tk)])

    def update(j, s, r0, nr, triangular):
        start = pl.multiple_of(j * tk, tk)
        v = v_ref[0, 0, pl.ds(start, tk), :]
        if triangular:
            row = lax.broadcasted_iota(jnp.int32, (nr, tk), 0)
            col = lax.broadcasted_iota(jnp.int32, (nr, tk), 1)
            s = jnp.where(col <= row, s, -jnp.inf)
        tiles = [s[:, t * LANES:(t + 1) * LANES] for t in range(tk // LANES)]
        if not online_max:
            ps = [jnp.exp2(t) for t in tiles]
            l_scr[r0:r0 + nr, :] += functools.reduce(jnp.add, ps)
            p = jnp.concatenate(ps, axis=-1).astype(_BF16)
            acc_scr[r0:r0 + nr, :] += _dot(p, v)
            return
        mx = functools.reduce(jnp.maximum, tiles)
        m_prev = m_scr[r0:r0 + nr, :]
        m_new = jnp.maximum(m_prev, jnp.max(mx, axis=-1, keepdims=True))
        alpha = jnp.exp2(m_prev - m_new)
        ps = [jnp.exp2(t - m_new) for t in tiles]
        l_scr[r0:r0 + nr, :] = alpha * l_scr[r0:r0 + nr, :] + functools.reduce(jnp.add, ps)
        p = jnp.concatenate(ps, axis=-1).astype(_BF16)
        acc_scr[r0:r0 + nr, :] = alpha * acc_scr[r0:r0 + nr, :] + _dot(p, v)
        m_scr[r0:r0 + nr, :] = m_new

    def query_block(qi, sa_scr, sa_next_scr):
        if online_max:
            m_scr[...] = jnp.full_like(m_scr, -jnp.inf)
        l_scr[...] = jnp.zeros_like(l_scr)
        acc_scr[...] = jnp.zeros_like(acc_scr)

        def pair(jj):
            j = 2 * jj
            sb_scr[...] = scores(qi, j + 1, 0, tq)
            update(j, sa_scr[...], 0, tq, False)
            sa_scr[...] = scores(qi, j + 2, 0, tq)
            update(j + 1, sb_scr[...], 0, tq, False)

        def two_pairs(jq, carry):
            pair(2 * jq)
            pair(2 * jq + 1)
            return carry

        def last_pair(_, carry):
            pair(qi - 1)
            return carry

        lax.fori_loop(0, qi // 2, two_pairs, 0)
        lax.fori_loop(0, qi % 2, last_pair, 0)

        jd = 2 * qi
        sb_scr[half:tq, :] = scores(qi, jd + 1, half, half)
        update(jd, sa_scr[0:half, :], 0, half, True)
        sa_next_scr[...] = scores(jnp.minimum(qi + 1, nq - 1), 0, 0, tq)
        update(jd, sa_scr[half:tq, :], half, half, False)
        update(jd + 1, sb_scr[half:tq, :], half, half, True)

        rows = pl.ds(pl.multiple_of(qi * tq, tq), tq)
        o = acc_scr[...] / jnp.sum(l_scr[...], axis=-1, keepdims=True)
        o_ref[rows, :] = (o * sz_ref[rows, :].astype(_F32)).astype(_BF16)

    sa0_scr[...] = scores(0, 0, 0, tq)

    def two_blocks(ib, carry):
        query_block(2 * ib, sa0_scr, sa1_scr)
        query_block(2 * ib + 1, sa1_scr, sa0_scr)
        return carry

    lax.fori_loop(0, nq // 2, two_blocks, 0)


def _attention(online_max, q, k, v, sz):
    bsz, _, seq, _ = q.shape
    tq = ATTN_TQ
    per_head = lambda width: pl.BlockSpec((1, 1, seq, width), lambda b, h: (b, h, 0, 0))
    return pl.pallas_call(
        functools.partial(_attn_body, online_max),
        grid=(bsz, N_HEADS),
        in_specs=[
            per_head(HEAD_SLAB),
            pl.BlockSpec((1, 1, HEAD_SLAB, seq), lambda b, h: (b, h, 0, 0)),
            per_head(V_HEAD),
            pl.BlockSpec((seq, V_HEAD), lambda b, h: (b, h)),
        ],
        out_specs=pl.BlockSpec((seq, V_HEAD), lambda b, h: (b, h)),
        out_shape=jax.ShapeDtypeStruct((bsz * seq, D_ATTN), _BF16),
        scratch_shapes=[
            pltpu.VMEM((tq, ATTN_TK), _F32),
            pltpu.VMEM((tq, ATTN_TK), _F32),
            pltpu.VMEM((tq, ATTN_TK), _F32),
            pltpu.VMEM((tq, LANES), _F32),
            pltpu.VMEM((tq, LANES), _F32),
            pltpu.VMEM((tq, V_HEAD), _F32),
        ],
        compiler_params=pltpu.CompilerParams(
            dimension_semantics=("arbitrary", "arbitrary"), vmem_limit_bytes=VMEM_LIMIT),
        name="attention_online_max" if online_max else "attention_shifted",
    )(q, k, v, sz)


def _out_proj_body(yc_ref, ya_ref, x_ref, mod_ref, w_ref, o_ref):
    y = _dot(yc_ref[...], w_ref[0:D_CONV, :]) + _dot(ya_ref[...], w_ref[D_CONV:D_MIX, :])
    o_ref[...] = x_ref[...] + mod_ref[0, 2:3, :] * y


def _out_proj(yconv, yattn, x2d, mod3, wout, seq):
    t = x2d.shape[0]
    tm = OUT_TM
    tpb = seq // tm
    return pl.pallas_call(
        _out_proj_body,
        grid=(t // tm,),
        in_specs=[
            pl.BlockSpec((tm, D_CONV), lambda i: (i, 0)),
            pl.BlockSpec((tm, D_ATTN), lambda i: (i, 0)),
            pl.BlockSpec((tm, D_MODEL), lambda i: (i, 0)),
            pl.BlockSpec((1, 3, D_MODEL), lambda i: (i // tpb, 0, 0)),
            _resident((D_MIX, D_MODEL)),
        ],
        out_specs=pl.BlockSpec((tm, D_MODEL), lambda i: (i, 0)),
        out_shape=jax.ShapeDtypeStruct((t, D_MODEL), _F32),
        compiler_params=pltpu.CompilerParams(
            dimension_semantics=("arbitrary",), vmem_limit_bytes=VMEM_LIMIT),
        name="out_proj",
    )(yconv, yattn, x2d, mod3, wout)


def _score_shift(q_g, k_g, qscale):
    bound = QK_HEAD * qscale * jnp.max(jnp.abs(q_g)) * jnp.max(jnp.abs(k_g))
    shift = (bound * 1.03 + 1.0).astype(_BF16).astype(_F32)
    use_shift = bound + shift <= EXP2_NORMAL_RANGE
    return jnp.where(use_shift, shift, 0.0), use_shift


def _rope_slab_cols(w):
    z = jnp.zeros(w.shape[:-1] + (HALF,), w.dtype)
    return jnp.concatenate([w[..., :HALF], z, w[..., HALF:], z], axis=-1)


def _layer(layer, x2d, mod, cos_t, sin_t, norm_g, w_in, win_all, conv_w, q_a_g, w_q_b, kv_a_g, w_kv_b, q_g, k_g,
           w_out, bsz, seq):
    wtail = jnp.concatenate([_rope_slab_cols(w_in[:, OFF_KR:OFF_ZA]), w_in[:, OFF_ZA:]], axis=-1).astype(_BF16)
    wq = w_q_b.reshape(Q_LORA, N_HEADS, QK_HEAD)
    wqb = jnp.concatenate([wq[..., :QK_NOPE], _rope_slab_cols(wq[..., QK_NOPE:])], axis=-1)
    wqb = wqb.reshape(Q_LORA, N_HEADS * HEAD_SLAB).astype(_BF16)
    wkvb = w_kv_b.astype(_BF16)
    qscale = math.log2(math.e) / math.sqrt(QK_HEAD)
    gq = jnp.concatenate([q_g[:QK_NOPE], _rope_slab_cols(q_g[QK_NOPE:])]).reshape(1, HEAD_SLAB) * qscale
    gk = jnp.concatenate([k_g[:QK_NOPE], _rope_slab_cols(k_g[QK_NOPE:])]).reshape(1, HEAD_SLAB)
    mod3 = mod.reshape(bsz, 3, D_MODEL)
    shift, use_shift = _score_shift(q_g, k_g, qscale)
    pad_lane = jnp.arange(ROPE_SLAB) == SHIFT_LANE
    qb = jnp.where(pad_lane, -shift, 0.0).reshape(1, ROPE_SLAB)
    kb = jnp.where(pad_lane, 1.0, 0.0).reshape(1, ROPE_SLAB)

    ng = norm_g.reshape(1, D_MODEL)
    yconv = _conv_proj(layer, x2d, mod3, ng, win_all, conv_w, seq)
    sz, q, k, v = _mla_proj(
        layer, x2d, mod3, ng, win_all, wtail, q_a_g.reshape(1, Q_LORA), wqb,
        kv_a_g.reshape(1, KV_LORA), wkvb, gq, gk, qb, kb, cos_t, sin_t, bsz, seq)
    yattn = lax.cond(use_shift, functools.partial(_attention, False), functools.partial(_attention, True),
                     q, k, v, sz)
    return _out_proj(yconv, yattn, x2d, mod3, w_out.astype(_BF16), seq)


def kernel(x, c, positions, ada_w, ada_b, norm_g, w_in, conv_w, q_a_g, w_q_b, kv_a_g, w_kv_b, q_g, k_g, w_out):
    bsz, seq, _ = x.shape
    depth = ada_w.shape[0]
    cos_t, sin_t = _rope_tables(positions)
    x2d = x.reshape(bsz * seq, D_MODEL)
    win_all = w_in.astype(_BF16)
    for l in range(depth):
        mod = _adaln_mod(c, ada_w[l], ada_b[l])
        x2d = _layer(l, x2d, mod, cos_t, sin_t, norm_g[l], w_in[l], win_all, conv_w[l], q_a_g[l], w_q_b[l],
                     kv_a_g[l], w_kv_b[l], q_g[l], k_g[l], w_out[l], bsz, seq)
    return x2d.reshape(bsz, seq, D_MODEL)
```

```python
import functools
import math

import jax
import jax.numpy as jnp
from jax import lax
from jax.experimental import pallas as pl
from jax.experimental.pallas import tpu as pltpu

D_MODEL = 2048
D_CONV = 1024
CONV_WIDTH = 3
N_HEADS = 8
QK_NOPE = 128
QK_ROPE = 64
QK_HEAD = QK_NOPE + QK_ROPE
V_HEAD = 128
D_ATTN = N_HEADS * V_HEAD
Q_LORA = 512
KV_LORA = 256
ROPE_BASE = 10000.0
D_MIX = D_CONV + D_ATTN
EPS = 1e-6

KV_HEAD = QK_NOPE + V_HEAD
MXU_N = 256
LANES = 128
SUBLANES = 8
ROPE_SLAB = LANES
HEAD_SLAB = QK_NOPE + ROPE_SLAB
HALF = QK_ROPE // 2
SHIFT_LANE = HALF
EXP2_NORMAL_RANGE = 120.0

OFF_XC = 0
OFF_BC = D_CONV
OFF_CC = 2 * D_CONV
OFF_ZC = 3 * D_CONV
OFF_CQ = 4 * D_CONV
OFF_CKV = OFF_CQ + Q_LORA
OFF_KR = OFF_CKV + KV_LORA
OFF_ZA = OFF_KR + QK_ROPE
TAIL_COLS = ROPE_SLAB + D_ATTN

ADA_TN = 512
ROPE_TT = 1024
CONV_TM = 512
MLA_TM = 512
CONV_CW = 256
HEAD_GROUP = 4
ATTN_TQ = 1024
ATTN_TK = 512
OUT_TM = 512
VMEM_LIMIT = 56 * 1024 * 1024

_BF16 = jnp.bfloat16
_F32 = jnp.float32


def _sigmoid(z):
    return 1.0 / (1.0 + jnp.exp(-z))


def _dot(a, b):
    return jnp.dot(a, b, preferred_element_type=_F32)


def _adaln_body(ct_ref, w_ref, b_ref, o_ref):
    ct = ct_ref[...]
    sc = ct * _sigmoid(ct)
    w = w_ref[...]
    rows = [jnp.sum(w * sc[:, b:b + 1], axis=0, keepdims=True) for b in range(ct.shape[1])]
    o_ref[...] = jnp.concatenate(rows, axis=0) + b_ref[...]


def _adaln_mod(c, ada_w, ada_b):
    bsz = c.shape[0]
    n = ada_w.shape[1]
    return pl.pallas_call(
        _adaln_body,
        grid=(n // ADA_TN,),
        in_specs=[
            pl.BlockSpec((D_MODEL, bsz), lambda j: (0, 0)),
            pl.BlockSpec((D_MODEL, ADA_TN), lambda j: (0, j)),
            pl.BlockSpec((1, ADA_TN), lambda j: (0, j)),
        ],
        out_specs=pl.BlockSpec((bsz, ADA_TN), lambda j: (0, j)),
        out_shape=jax.ShapeDtypeStruct((bsz, n), _F32),
        name="adaln_mod",
    )(c.T, ada_w, ada_b.reshape(1, n))


def _rope_body(pos_ref, f_ref, cos_ref, sin_ref):
    ang = f_ref[...] * pos_ref[...].astype(_F32)
    c = jnp.cos(ang)
    s = jnp.sin(ang)
    z = jnp.zeros_like(c)
    cos_ref[...] = jnp.concatenate([c, z, c, z], axis=0).T
    sin_ref[...] = jnp.concatenate([-s, z, s, z], axis=0).T


def _rope_tables(positions):
    t = positions.size
    inv_freq = ROPE_BASE ** (-jnp.arange(0, QK_ROPE, 2, dtype=_F32) / QK_ROPE)
    out = pl.BlockSpec((ROPE_TT, ROPE_SLAB), lambda i: (i, 0))
    return pl.pallas_call(
        _rope_body,
        grid=(t // ROPE_TT,),
        in_specs=[pl.BlockSpec((1, ROPE_TT), lambda i: (0, i)), pl.BlockSpec((HALF, 1), lambda i: (0, 0))],
        out_specs=[out, out],
        out_shape=[jax.ShapeDtypeStruct((t, ROPE_SLAB), _F32)] * 2,
        name="rope_tables",
    )(positions.reshape(1, t), inv_freq.reshape(HALF, 1))


def _modulated_norm(x_ref, mod_ref, ng_ref):
    x = x_ref[...]
    xn = x * lax.rsqrt(jnp.mean(x * x, axis=-1, keepdims=True) + EPS)
    a = ng_ref[...] * (1.0 + mod_ref[0, 1:2, :])
    return (xn * a + mod_ref[0, 0:1, :]).astype(_BF16)


def _conv_proj_body(tiles_per_batch, x_ref, mod_ref, ng_ref, win_ref, convw_ref, yconv_ref,
                    h_scr, carry_scr, vbuf_scr):
    tm = x_ref.shape[0]

    @pl.when(pl.program_id(0) % tiles_per_batch == 0)
    def _():
        carry_scr[...] = jnp.zeros_like(carry_scr)

    h_scr[...] = _modulated_norm(x_ref, mod_ref, ng_ref)

    def proj(off, width):
        return _dot(h_scr[...], win_ref[:, off:off + width])

    for lo in range(0, D_CONV, CONV_CW):
        xc = proj(OFF_XC + lo, CONV_CW)
        bc = proj(OFF_BC + lo, CONV_CW)
        cc = proj(OFF_CC + lo, CONV_CW)
        zc = proj(OFF_ZC + lo, CONV_CW)
        v0 = cc * xc
        vbuf_scr[0:SUBLANES, :] = carry_scr[:, lo:lo + CONV_CW]
        vbuf_scr[SUBLANES:SUBLANES + tm, :] = v0
        carry_scr[:, lo:lo + CONV_CW] = v0[tm - SUBLANES:, :]
        v1 = vbuf_scr[SUBLANES - 1:SUBLANES - 1 + tm, :]
        v2 = vbuf_scr[SUBLANES - 2:SUBLANES - 2 + tm, :]
        w = convw_ref[:, lo:lo + CONV_CW]
        conv = w[0:1, :] * v2 + w[1:2, :] * v1 + w[2:3, :] * v0
        yconv_ref[:, lo:lo + CONV_CW] = (bc * conv * (zc * _sigmoid(zc))).astype(_BF16)


def _resident(shape, index=None):
    index = (0,) * len(shape) if index is None else index
    return pl.BlockSpec(shape, lambda i: index, pipeline_mode=pl.Buffered(1))


def _conv_proj(layer, x2d, mod3, norm_g, win_all, conv_w, seq):
    t = x2d.shape[0]
    tm = CONV_TM
    tpb = seq // tm
    return pl.pallas_call(
        functools.partial(_conv_proj_body, tpb),
        grid=(t // tm,),
        in_specs=[
            pl.BlockSpec((tm, D_MODEL), lambda i: (i, 0)),
            pl.BlockSpec((1, 3, D_MODEL), lambda i: (i // tpb, 0, 0)),
            _resident((1, D_MODEL)),
            _resident((None, D_MODEL, 4 * D_CONV), (layer, 0, 0)),
            _resident((CONV_WIDTH, D_CONV)),
        ],
        out_specs=pl.BlockSpec((tm, D_CONV), lambda i: (i, 0)),
        out_shape=jax.ShapeDtypeStruct((t, D_CONV), _BF16),
        scratch_shapes=[
            pltpu.VMEM((tm, D_MODEL), _BF16),
            pltpu.VMEM((SUBLANES, D_CONV), _F32),
            pltpu.VMEM((SUBLANES + tm, CONV_CW), _F32),
        ],
        compiler_params=pltpu.CompilerParams(
            dimension_semantics=("arbitrary",), vmem_limit_bytes=VMEM_LIMIT),
        name="conv_proj",
    )(x2d, mod3, norm_g, win_all, conv_w)


def _mla_proj_body(x_ref, mod_ref, ng_ref, wcq_ref, wckv_ref, wtail_ref, qag_ref, wqb_ref, kvag_ref, wkvb_ref,
                   gq_ref, gk_ref, qb_ref, kb_ref, cos_ref, sin_ref,
                   sz_ref, q_ref, kt_ref, v_ref, h_scr, cqn_scr, ckvn_scr):
    h_scr[...] = _modulated_norm(x_ref, mod_ref, ng_ref)
    cos_t = cos_ref[...]
    sin_t = sin_ref[...]

    def rope(slab):
        return slab * cos_t + pltpu.roll(slab, ROPE_SLAB // 2, axis=1) * sin_t

    def rms(parts, width):
        ss = sum(jnp.sum(p * p, axis=-1, keepdims=True) for p in parts)
        return lax.rsqrt(ss * (1.0 / width) + EPS)

    cq = _dot(h_scr[...], wcq_ref[...])
    cqn_scr[...] = (cq * rms([cq], Q_LORA) * qag_ref[...]).astype(_BF16)
    ckv = _dot(h_scr[...], wckv_ref[...])
    ckvn_scr[...] = (ckv * rms([ckv], KV_LORA) * kvag_ref[...]).astype(_BF16)

    kr = _dot(h_scr[...], wtail_ref[:, 0:ROPE_SLAB])
    gk = gk_ref[...]
    gq = gq_ref[...]
    kr_rot = rope(kr * gk[:, QK_NOPE:HEAD_SLAB])

    def lane_sum(sq):
        ones = jnp.ones((sq.shape[1], LANES), _BF16)
        return _dot(sq.astype(_BF16), ones)

    ss_r = lane_sum(kr * kr)
    gate_cols = D_ATTN // (N_HEADS // HEAD_GROUP)
    for g0 in range(0, N_HEADS, HEAD_GROUP):
        heads = range(g0, g0 + HEAD_GROUP)
        slabs = [_dot(cqn_scr[...], wqb_ref[:, hd * HEAD_SLAB:(hd + 1) * HEAD_SLAB]) for hd in heads]
        kvs = [_dot(ckvn_scr[...], wkvb_ref[:, hd * KV_HEAD:(hd + 1) * KV_HEAD]) for hd in heads]
        ssq = [lane_sum(s * s) for s in slabs]
        ssk = [lane_sum(kv[:, 0:QK_NOPE] * kv[:, 0:QK_NOPE]) for kv in kvs]
        glo = (g0 // HEAD_GROUP) * gate_cols
        zas = [(lo, _dot(h_scr[...], wtail_ref[:, ROPE_SLAB + lo:ROPE_SLAB + lo + 2 * MXU_N]))
               for lo in range(glo, glo + gate_cols, 2 * MXU_N)]
        for hd, slab, kv, sq, sk in zip(heads, slabs, kvs, ssq, ssk):
            r = lax.rsqrt(sq * (1.0 / QK_HEAD) + EPS)
            q_ref[0, hd, :, 0:QK_NOPE] = (slab[:, 0:QK_NOPE] * r * gq[:, 0:QK_NOPE]).astype(_BF16)
            q_rope = rope(slab[:, QK_NOPE:HEAD_SLAB] * r * gq[:, QK_NOPE:HEAD_SLAB])
            q_ref[0, hd, :, QK_NOPE:HEAD_SLAB] = (q_rope + qb_ref[...]).astype(_BF16)
            r = lax.rsqrt((sk + ss_r) * (1.0 / QK_HEAD) + EPS)
            kt_ref[0, hd, 0:QK_NOPE, :] = (kv[:, 0:QK_NOPE] * r * gk[:, 0:QK_NOPE]).T.astype(_BF16)
            kt_ref[0, hd, QK_NOPE:HEAD_SLAB, :] = (kr_rot * r + kb_ref[...]).T.astype(_BF16)
            v_ref[0, hd, :, :] = kv[:, QK_NOPE:KV_HEAD].astype(_BF16)
        for lo, za in zas:
            sz_ref[:, lo:lo + 2 * MXU_N] = (za * _sigmoid(za)).astype(_BF16)


def _mla_proj(layer, x2d, mod3, norm_g, win_all, wtail, q_a_g, wqb, kv_a_g, wkvb, gq, gk, qb, kb, cos_t, sin_t,
              bsz, seq):
    t = x2d.shape[0]
    tm = MLA_TM
    tpb = seq // tm
    tok = lambda width: pl.BlockSpec((tm, width), lambda i: (i, 0))
    head = lambda width: pl.BlockSpec((1, N_HEADS, tm, width), lambda i: (i // tpb, 0, i % tpb, 0))
    return pl.pallas_call(
        _mla_proj_body,
        grid=(t // tm,),
        in_specs=[
            tok(D_MODEL),
            pl.BlockSpec((1, 3, D_MODEL), lambda i: (i // tpb, 0, 0)),
            _resident((1, D_MODEL)),
            _resident((None, D_MODEL, Q_LORA), (layer, 0, OFF_CQ // Q_LORA)),
            _resident((None, D_MODEL, KV_LORA), (layer, 0, OFF_CKV // KV_LORA)),
            _resident((D_MODEL, TAIL_COLS)),
            _resident((1, Q_LORA)),
            _resident((Q_LORA, N_HEADS * HEAD_SLAB)),
            _resident((1, KV_LORA)),
            _resident((KV_LORA, N_HEADS * (QK_NOPE + V_HEAD))),
            _resident((1, HEAD_SLAB)),
            _resident((1, HEAD_SLAB)),
            _resident((1, ROPE_SLAB)),
            _resident((1, ROPE_SLAB)),
            tok(ROPE_SLAB),
            tok(ROPE_SLAB),
        ],
        out_specs=[
            tok(D_ATTN),
            head(HEAD_SLAB),
            pl.BlockSpec((1, N_HEADS, HEAD_SLAB, tm), lambda i: (i // tpb, 0, 0, i % tpb)),
            head(V_HEAD),
        ],
        out_shape=[
            jax.ShapeDtypeStruct((t, D_ATTN), _BF16),
            jax.ShapeDtypeStruct((bsz, N_HEADS, seq, HEAD_SLAB), _BF16),
            jax.ShapeDtypeStruct((bsz, N_HEADS, HEAD_SLAB, seq), _BF16),
            jax.ShapeDtypeStruct((bsz, N_HEADS, seq, V_HEAD), _BF16),
        ],
        scratch_shapes=[
            pltpu.VMEM((tm, D_MODEL), _BF16),
            pltpu.VMEM((tm, Q_LORA), _BF16),
            pltpu.VMEM((tm, KV_LORA), _BF16),
        ],
        compiler_params=pltpu.CompilerParams(
            dimension_semantics=("arbitrary",), vmem_limit_bytes=VMEM_LIMIT),
        name="mla_proj",
    )(x2d, mod3, norm_g, win_all, win_all, wtail, q_a_g, wqb, kv_a_g, wkvb, gq, gk, qb, kb, cos_t, sin_t)


def _attn_body(online_max, q_ref, kt_ref, v_ref, sz_ref, o_ref,
               sa0_scr, sa1_scr, sb_scr, m_scr, l_scr, acc_scr):
    tq = ATTN_TQ
    tk = ATTN_TK
    half = tq // 2
    assert tq == 2 * tk
    nq = q_ref.shape[2] // tq
    assert nq % 2 == 0

    def scores(qi, j, r0, nr):
        qstart = pl.multiple_of(qi * tq + r0, half)
        kstart = pl.multiple_of(j * tk, tk)
        return _dot(q_ref[0, 0, pl.ds(qstart, nr), :], kt_ref[0, 0, :, pl.ds(kstart, tk)])

    def update(j, s, r0, nr, triangular):
        start = pl.multiple_of(j * tk, tk)
        v = v_ref[0, 0, pl.ds(start, tk), :]
        if triangular:
            row = lax.broadcasted_iota(jnp.int32, (nr, tk), 0)
            col = lax.broadcasted_iota(jnp.int32, (nr, tk), 1)
            s = jnp.where(col <= row, s, -jnp.inf)
        tiles = [s[:, t * LANES:(t + 1) * LANES] for t in range(tk // LANES)]
        if not online_max:
            ps = [jnp.exp2(t) for t in tiles]
            l_scr[r0:r0 + nr, :] += functools.reduce(jnp.add, ps)
            p = jnp.concatenate(ps, axis=-1).astype(_BF16)
            acc_scr[r0:r0 + nr, :] += _dot(p, v)
            return
        mx = functools.reduce(jnp.maximum, tiles)
        m_prev = m_scr[r0:r0 + nr, :]
        m_new = jnp.maximum(m_prev, jnp.max(mx, axis=-1, keepdims=True))
        alpha = jnp.exp2(m_prev - m_new)
        ps = [jnp.exp2(t - m_new) for t in tiles]
        l_scr[r0:r0 + nr, :] = alpha * l_scr[r0:r0 + nr, :] + functools.reduce(jnp.add, ps)
        p = jnp.concatenate(ps, axis=-1).astype(_BF16)
        acc_scr[r0:r0 + nr, :] = alpha * acc_scr[r0:r0 + nr, :] + _dot(p, v)
        m_scr[r0:r0 + nr, :] = m_new

    def query_block(qi, sa_scr, sa_next_scr):
        if online_max:
            m_scr[...] = jnp.full_like(m_scr, -jnp.inf)
        l_scr[...] = jnp.zeros_like(l_scr)
        acc_scr[...] = jnp.zeros_like(acc_scr)

        def pair(jj, carry):
            j = 2 * jj
            sb_scr[...] = scores(qi, j + 1, 0, tq)
            update(j, sa_scr[...], 0, tq, False)
            sa_scr[...] = scores(qi, j + 2, 0, tq)
            update(j + 1, sb_scr[...], 0, tq, False)
            return carry

        lax.fori_loop(0, qi, pair, 0)

        jd = 2 * qi
        sb_scr[half:tq, :] = scores(qi, jd + 1, half, half)
        update(jd, sa_scr[0:half, :], 0, half, True)
        sa_next_scr[...] = scores(jnp.minimum(qi + 1, nq - 1), 0, 0, tq)
        update(jd, sa_scr[half:tq, :], half, half, False)
        update(jd + 1, sb_scr[half:tq, :], half, half, True)

        rows = pl.ds(pl.multiple_of(qi * tq, tq), tq)
        o = acc_scr[...] / jnp.sum(l_scr[...], axis=-1, keepdims=True)
        o_ref[rows, :] = (o * sz_ref[rows, :].astype(_F32)).astype(_BF16)

    sa0_scr[...] = scores(0, 0, 0, tq)

    def two_blocks(ib, carry):
        query_block(2 * ib, sa0_scr, sa1_scr)
        query_block(2 * ib + 1, sa1_scr, sa0_scr)
        return carry

    lax.fori_loop(0, nq // 2, two_blocks, 0)


def _attn_shifted_body(q_ref, kt_ref, v_ref, sz_ref, o_ref, pa0_scr, pa1_scr, pb_scr, lnext_scr, l_scr, acc_scr):
    tq = ATTN_TQ
    tk = ATTN_TK
    half = tq // 2
    assert tq == 2 * tk
    nq = q_ref.shape[2] // tq
    assert nq % 2 == 0
    col_minus_row = (lax.broadcasted_iota(jnp.int32, (tq, tk), 1)
                     - lax.broadcasted_iota(jnp.int32, (tq, tk), 0))

    def probs(qi, j, r0, nr, masked):
        qstart = pl.multiple_of(qi * tq + r0, half)
        kstart = pl.multiple_of(j * tk, tk)
        s = _dot(q_ref[0, 0, pl.ds(qstart, nr), :], kt_ref[0, 0, :, pl.ds(kstart, tk)])
        if masked:
            s = jnp.where(col_minus_row[r0:r0 + nr, :] <= qi * tq - j * tk, s, -jnp.inf)
        ps = [jnp.exp2(s[:, t * LANES:(t + 1) * LANES]) for t in range(tk // LANES)]
        return jnp.concatenate(ps, axis=-1).astype(_BF16), functools.reduce(jnp.add, ps)

    def values(j):
        return v_ref[0, 0, pl.ds(pl.multiple_of(j * tk, tk), tk), :]

    def query_block(qi, pa_scr, pa_next_scr):
        l_scr[...] = lnext_scr[...]
        acc_scr[...] = jnp.zeros_like(acc_scr)

        def pair(jj, carry):
            j = 2 * jj
            p, ls = probs(qi, j + 1, 0, tq, False)
            pb_scr[...] = p
            l_scr[...] += ls
            acc_scr[...] += _dot(pa_scr[...], values(j))
            p, ls = probs(qi, j + 2, 0, tq, True)
            pa_scr[...] = p
            l_scr[...] += ls
            acc_scr[...] += _dot(pb_scr[...], values(j + 1))
            return carry

        lax.fori_loop(0, qi, pair, 0)

        jd = 2 * qi
        p, ls = probs(qi, jd + 1, half, half, True)
        pb_scr[half:tq, :] = p
        l_scr[half:tq, :] += ls
        acc_scr[...] += _dot(pa_scr[...], values(jd))
        nxt = jnp.minimum(qi + 1, nq - 1)
        p, ls = probs(nxt, 0, 0, tq, True)
        pa_next_scr[...] = p
        lnext_scr[...] = ls
        acc_scr[half:tq, :] += _dot(pb_scr[half:tq, :], values(jd + 1))

        rows = pl.ds(pl.multiple_of(qi * tq, tq), tq)
        o = acc_scr[...] / jnp.sum(l_scr[...], axis=-1, keepdims=True)
        o_ref[rows, :] = (o * sz_ref[rows, :].astype(_F32)).astype(_BF16)

    p, ls = probs(0, 0, 0, tq, True)
    pa0_scr[...] = p
    lnext_scr[...] = ls

    def two_blocks(ib, carry):
        query_block(2 * ib, pa0_scr, pa1_scr)
        query_block(2 * ib + 1, pa1_scr, pa0_scr)
        return carry

    lax.fori_loop(0, nq // 2, two_blocks, 0)


def _attention(online_max, q, k, v, sz):
    bsz, _, seq, _ = q.shape
    tq = ATTN_TQ
    per_head = lambda width: pl.BlockSpec((1, 1, seq, width), lambda b, h: (b, h, 0, 0))
    stats = [pltpu.VMEM((tq, LANES), _F32), pltpu.VMEM((tq, LANES), _F32), pltpu.VMEM((tq, V_HEAD), _F32)]
    if online_max:
        body = functools.partial(_attn_body, True)
        scratch = [pltpu.VMEM((tq, ATTN_TK), _F32)] * 3 + stats
    else:
        body = _attn_shifted_body
        scratch = [pltpu.VMEM((tq, ATTN_TK), _BF16)] * 3 + stats
    return pl.pallas_call(
        body,
        grid=(bsz, N_HEADS),
        in_specs=[
            per_head(HEAD_SLAB),
            pl.BlockSpec((1, 1, HEAD_SLAB, seq), lambda b, h: (b, h, 0, 0)),
            per_head(V_HEAD),
            pl.BlockSpec((seq, V_HEAD), lambda b, h: (b, h)),
        ],
        out_specs=pl.BlockSpec((seq, V_HEAD), lambda b, h: (b, h)),
        out_shape=jax.ShapeDtypeStruct((bsz * seq, D_ATTN), _BF16),
        scratch_shapes=scratch,
        compiler_params=pltpu.CompilerParams(
            dimension_semantics=("arbitrary", "arbitrary"), vmem_limit_bytes=VMEM_LIMIT),
        name="attention_online_max" if online_max else "attention_shifted",
    )(q, k, v, sz)


def _out_proj_body(yc_ref, ya_ref, x_ref, mod_ref, w_ref, o_ref):
    y = _dot(yc_ref[...], w_ref[0:D_CONV, :]) + _dot(ya_ref[...], w_ref[D_CONV:D_MIX, :])
    o_ref[...] = x_ref[...] + mod_ref[0, 2:3, :] * y


def _out_proj(yconv, yattn, x2d, mod3, wout, seq):
    t = x2d.shape[0]
    tm = OUT_TM
    tpb = seq // tm
    return pl.pallas_call(
        _out_proj_body,
        grid=(t // tm,),
        in_specs=[
            pl.BlockSpec((tm, D_CONV), lambda i: (i, 0)),
            pl.BlockSpec((tm, D_ATTN), lambda i: (i, 0)),
            pl.BlockSpec((tm, D_MODEL), lambda i: (i, 0)),
            pl.BlockSpec((1, 3, D_MODEL), lambda i: (i // tpb, 0, 0)),
            _resident((D_MIX, D_MODEL)),
        ],
        out_specs=pl.BlockSpec((tm, D_MODEL), lambda i: (i, 0)),
        out_shape=jax.ShapeDtypeStruct((t, D_MODEL), _F32),
        compiler_params=pltpu.CompilerParams(
            dimension_semantics=("arbitrary",), vmem_limit_bytes=VMEM_LIMIT),
        name="out_proj",
    )(yconv, yattn, x2d, mod3, wout)


def _score_shift(q_g, k_g, qscale):
    bound = QK_HEAD * qscale * jnp.max(jnp.abs(q_g)) * jnp.max(jnp.abs(k_g))
    shift = (bound * 1.03 + 1.0).astype(_BF16).astype(_F32)
    use_shift = bound + shift <= EXP2_NORMAL_RANGE
    return jnp.where(use_shift, shift, 0.0), use_shift


def _rope_slab_cols(w):
    z = jnp.zeros(w.shape[:-1] + (HALF,), w.dtype)
    return jnp.concatenate([w[..., :HALF], z, w[..., HALF:], z], axis=-1)


def _layer(layer, x2d, mod, cos_t, sin_t, norm_g, w_in, win_all, conv_w, q_a_g, w_q_b, kv_a_g, w_kv_b, q_g, k_g,
           w_out, bsz, seq):
    wtail = jnp.concatenate([_rope_slab_cols(w_in[:, OFF_KR:OFF_ZA]), w_in[:, OFF_ZA:]], axis=-1).astype(_BF16)
    wq = w_q_b.reshape(Q_LORA, N_HEADS, QK_HEAD)
    wqb = jnp.concatenate([wq[..., :QK_NOPE], _rope_slab_cols(wq[..., QK_NOPE:])], axis=-1)
    wqb = wqb.reshape(Q_LORA, N_HEADS * HEAD_SLAB).astype(_BF16)
    wkvb = w_kv_b.astype(_BF16)
    qscale = math.log2(math.e) / math.sqrt(QK_HEAD)
    gq = jnp.concatenate([q_g[:QK_NOPE], _rope_slab_cols(q_g[QK_NOPE:])]).reshape(1, HEAD_SLAB) * qscale
    gk = jnp.concatenate([k_g[:QK_NOPE], _rope_slab_cols(k_g[QK_NOPE:])]).reshape(1, HEAD_SLAB)
    mod3 = mod.reshape(bsz, 3, D_MODEL)
    shift, use_shift = _score_shift(q_g, k_g, qscale)
    pad_lane = jnp.arange(ROPE_SLAB) == SHIFT_LANE
    qb = jnp.where(pad_lane, -shift, 0.0).reshape(1, ROPE_SLAB)
    kb = jnp.where(pad_lane, 1.0, 0.0).reshape(1, ROPE_SLAB)

    ng = norm_g.reshape(1, D_MODEL)
    yconv = _conv_proj(layer, x2d, mod3, ng, win_all, conv_w, seq)
    sz, q, k, v = _mla_proj(
        layer, x2d, mod3, ng, win_all, wtail, q_a_g.reshape(1, Q_LORA), wqb,
        kv_a_g.reshape(1, KV_LORA), wkvb, gq, gk, qb, kb, cos_t, sin_t, bsz, seq)
    yattn = lax.cond(use_shift, functools.partial(_attention, False), functools.partial(_attention, True),
                     q, k, v, sz)
    return _out_proj(yconv, yattn, x2d, mod3, w_out.astype(_BF16), seq)


def kernel(x, c, positions, ada_w, ada_b, norm_g, w_in, conv_w, q_a_g, w_q_b, kv_a_g, w_kv_b, q_g, k_g, w_out):
    bsz, seq, _ = x.shape
    depth = ada_w.shape[0]
    cos_t, sin_t = _rope_tables(positions)
    x2d = x.reshape(bsz * seq, D_MODEL)
    win_all = w_in.astype(_BF16)
    for l in range(depth):
        mod = _adaln_mod(c, ada_w[l], ada_b[l])
        x2d = _layer(l, x2d, mod, cos_t, sin_t, norm_g[l], w_in[l], win_all, conv_w[l], q_a_g[l], w_q_b[l],
                     kv_a_g[l], w_kv_b[l], q_g[l], k_g[l], w_out[l], bsz, seq)
    return x2d.reshape(bsz, seq, D_MODEL)
```

```python
import functools
import math

import jax
import jax.numpy as jnp
from jax import lax
from jax.experimental import pallas as pl
from jax.experimental.pallas import tpu as pltpu

D_MODEL = 2048
D_CONV = 1024
CONV_WIDTH = 3
N_HEADS = 8
QK_NOPE = 128
QK_ROPE = 64
QK_HEAD = QK_NOPE + QK_ROPE
V_HEAD = 128
D_ATTN = N_HEADS * V_HEAD
Q_LORA = 512
KV_LORA = 256
ROPE_BASE = 10000.0
D_MIX = D_CONV + D_ATTN
EPS = 1e-6

KV_HEAD = QK_NOPE + V_HEAD
MXU_N = 256
LANES = 128
SUBLANES = 8
ROPE_SLAB = LANES
HEAD_SLAB = QK_NOPE + ROPE_SLAB
HALF = QK_ROPE // 2
SHIFT_LANE = HALF
EXP2_NORMAL_RANGE = 120.0

OFF_XC = 0
OFF_BC = D_CONV
OFF_CC = 2 * D_CONV
OFF_ZC = 3 * D_CONV
OFF_CQ = 4 * D_CONV
OFF_CKV = OFF_CQ + Q_LORA
OFF_KR = OFF_CKV + KV_LORA
OFF_ZA = OFF_KR + QK_ROPE
TAIL_COLS = ROPE_SLAB + D_ATTN

ADA_TN = 512
ROPE_TT = 1024
CONV_TM = 512
MLA_TM = 512
CONV_CW = 256
WLOAD_COLS = 256
HEAD_GROUP = 4
ATTN_TQ = 1024
ATTN_TK = 512
OUT_TM = 512
VMEM_LIMIT = 56 * 1024 * 1024

_BF16 = jnp.bfloat16
_F32 = jnp.float32


def _sigmoid(z):
    return 1.0 / (1.0 + jnp.exp(-z))


def _dot(a, b):
    return jnp.dot(a, b, preferred_element_type=_F32)


def _adaln_body(ct_ref, w_ref, b_ref, o_ref):
    ct = ct_ref[...]
    sc = ct * _sigmoid(ct)
    w = w_ref[...]
    rows = [jnp.sum(w * sc[:, b:b + 1], axis=0, keepdims=True) for b in range(ct.shape[1])]
    o_ref[...] = jnp.concatenate(rows, axis=0) + b_ref[...]


def _adaln_mod(c, ada_w, ada_b):
    bsz = c.shape[0]
    n = ada_w.shape[1]
    return pl.pallas_call(
        _adaln_body,
        grid=(n // ADA_TN,),
        in_specs=[
            pl.BlockSpec((D_MODEL, bsz), lambda j: (0, 0)),
            pl.BlockSpec((D_MODEL, ADA_TN), lambda j: (0, j)),
            pl.BlockSpec((1, ADA_TN), lambda j: (0, j)),
        ],
        out_specs=pl.BlockSpec((bsz, ADA_TN), lambda j: (0, j)),
        out_shape=jax.ShapeDtypeStruct((bsz, n), _F32),
        name="adaln_mod",
    )(c.T, ada_w, ada_b.reshape(1, n))


def _rope_body(pos_ref, f_ref, cos_ref, sin_ref):
    ang = f_ref[...] * pos_ref[...].astype(_F32)
    c = jnp.cos(ang)
    s = jnp.sin(ang)
    z = jnp.zeros_like(c)
    cos_ref[...] = jnp.concatenate([c, z, c, z], axis=0).T
    sin_ref[...] = jnp.concatenate([-s, z, s, z], axis=0).T


def _rope_tables(positions):
    t = positions.size
    inv_freq = ROPE_BASE ** (-jnp.arange(0, QK_ROPE, 2, dtype=_F32) / QK_ROPE)
    out = pl.BlockSpec((ROPE_TT, ROPE_SLAB), lambda i: (i, 0))
    return pl.pallas_call(
        _rope_body,
        grid=(t // ROPE_TT,),
        in_specs=[pl.BlockSpec((1, ROPE_TT), lambda i: (0, i)), pl.BlockSpec((HALF, 1), lambda i: (0, 0))],
        out_specs=[out, out],
        out_shape=[jax.ShapeDtypeStruct((t, ROPE_SLAB), _F32)] * 2,
        name="rope_tables",
    )(positions.reshape(1, t), inv_freq.reshape(HALF, 1))


def _modulated_norm(x_ref, mod_ref, ng_ref):
    x = x_ref[...]
    xn = x * lax.rsqrt(jnp.mean(x * x, axis=-1, keepdims=True) + EPS)
    a = ng_ref[...] * (1.0 + mod_ref[0, 1:2, :])
    return (xn * a + mod_ref[0, 0:1, :]).astype(_BF16)


def _conv_proj_body(layer, tiles_per_batch, x_ref, mod_ref, ng_ref, win_hbm_ref, convw_ref, yconv_ref,
                    win_ref, stage_scr, stage_sem, h_scr, carry_scr, vbuf_scr):
    tm = x_ref.shape[0]

    @pl.when(pl.program_id(0) == 0)
    def _():
        n_chunks = win_ref.shape[1] // WLOAD_COLS

        def chunk_copy(c):
            return pltpu.make_async_copy(
                win_hbm_ref.at[layer, :, pl.ds(c * WLOAD_COLS, WLOAD_COLS)],
                stage_scr.at[c % 2], stage_sem.at[c % 2])

        chunk_copy(0).start()
        for c in range(n_chunks):
            if c + 1 < n_chunks:
                chunk_copy(c + 1).start()
            chunk_copy(c).wait()
            win_ref[:, c * WLOAD_COLS:(c + 1) * WLOAD_COLS] = stage_scr[c % 2].astype(_BF16)

    @pl.when(pl.program_id(0) % tiles_per_batch == 0)
    def _():
        carry_scr[...] = jnp.zeros_like(carry_scr)

    h_scr[...] = _modulated_norm(x_ref, mod_ref, ng_ref)

    def proj(off, width):
        return _dot(h_scr[...], win_ref[:, off:off + width])

    for lo in range(0, D_CONV, CONV_CW):
        xc = proj(OFF_XC + lo, CONV_CW)
        bc = proj(OFF_BC + lo, CONV_CW)
        cc = proj(OFF_CC + lo, CONV_CW)
        zc = proj(OFF_ZC + lo, CONV_CW)
        v0 = cc * xc
        vbuf_scr[0:SUBLANES, :] = carry_scr[:, lo:lo + CONV_CW]
        vbuf_scr[SUBLANES:SUBLANES + tm, :] = v0
        carry_scr[:, lo:lo + CONV_CW] = v0[tm - SUBLANES:, :]
        v1 = vbuf_scr[SUBLANES - 1:SUBLANES - 1 + tm, :]
        v2 = vbuf_scr[SUBLANES - 2:SUBLANES - 2 + tm, :]
        w = convw_ref[:, lo:lo + CONV_CW]
        conv = w[0:1, :] * v2 + w[1:2, :] * v1 + w[2:3, :] * v0
        yconv_ref[:, lo:lo + CONV_CW] = (bc * conv * (zc * _sigmoid(zc))).astype(_BF16)


def _resident(shape, index=None):
    index = (0,) * len(shape) if index is None else index
    return pl.BlockSpec(shape, lambda i: index, pipeline_mode=pl.Buffered(1))


def _conv_proj(layer, x2d, mod3, norm_g, w_in, conv_w, seq):
    t = x2d.shape[0]
    tm = CONV_TM
    tpb = seq // tm
    return pl.pallas_call(
        functools.partial(_conv_proj_body, layer, tpb),
        grid=(t // tm,),
        in_specs=[
            pl.BlockSpec((tm, D_MODEL), lambda i: (i, 0)),
            pl.BlockSpec((1, 3, D_MODEL), lambda i: (i // tpb, 0, 0)),
            _resident((1, D_MODEL)),
            pl.BlockSpec(memory_space=pl.ANY),
            _resident((CONV_WIDTH, D_CONV)),
        ],
        out_specs=pl.BlockSpec((tm, D_CONV), lambda i: (i, 0)),
        out_shape=jax.ShapeDtypeStruct((t, D_CONV), _BF16),
        scratch_shapes=[
            pltpu.VMEM((D_MODEL, 4 * D_CONV), _BF16),
            pltpu.VMEM((2, D_MODEL, WLOAD_COLS), _F32),
            pltpu.SemaphoreType.DMA((2,)),
            pltpu.VMEM((tm, D_MODEL), _BF16),
            pltpu.VMEM((SUBLANES, D_CONV), _F32),
            pltpu.VMEM((SUBLANES + tm, CONV_CW), _F32),
        ],
        compiler_params=pltpu.CompilerParams(
            dimension_semantics=("arbitrary",), vmem_limit_bytes=VMEM_LIMIT),
        name="conv_proj",
    )(x2d, mod3, norm_g, w_in, conv_w)


def _mla_proj_body(x_ref, mod_ref, ng_ref, wcq32_ref, wckv32_ref, wtail_ref, qag_ref, wqb_ref, kvag_ref, wkvb_ref,
                   gq_ref, gk_ref, qb_ref, kb_ref, cos_ref, sin_ref,
                   sz_ref, q_ref, kt_ref, v_ref, wcq_ref, wckv_ref, h_scr, cqn_scr, ckvn_scr):
    @pl.when(pl.program_id(0) == 0)
    def _():
        wcq_ref[...] = wcq32_ref[...].astype(_BF16)
        wckv_ref[...] = wckv32_ref[...].astype(_BF16)

    h_scr[...] = _modulated_norm(x_ref, mod_ref, ng_ref)
    cos_t = cos_ref[...]
    sin_t = sin_ref[...]

    def rope(slab):
        return slab * cos_t + pltpu.roll(slab, ROPE_SLAB // 2, axis=1) * sin_t

    def rms(parts, width):
        ss = sum(jnp.sum(p * p, axis=-1, keepdims=True) for p in parts)
        return lax.rsqrt(ss * (1.0 / width) + EPS)

    cq = _dot(h_scr[...], wcq_ref[...])
    cqn_scr[...] = (cq * rms([cq], Q_LORA) * qag_ref[...]).astype(_BF16)
    ckv = _dot(h_scr[...], wckv_ref[...])
    ckvn_scr[...] = (ckv * rms([ckv], KV_LORA) * kvag_ref[...]).astype(_BF16)

    kr = _dot(h_scr[...], wtail_ref[:, 0:ROPE_SLAB])
    gk = gk_ref[...]
    gq = gq_ref[...]
    kr_rot = rope(kr * gk[:, QK_NOPE:HEAD_SLAB])

    def lane_sum(sq):
        ones = jnp.ones((sq.shape[1], LANES), _BF16)
        return _dot(sq.astype(_BF16), ones)

    ss_r = lane_sum(kr * kr)
    gate_cols = D_ATTN // (N_HEADS // HEAD_GROUP)
    for g0 in range(0, N_HEADS, HEAD_GROUP):
        heads = range(g0, g0 + HEAD_GROUP)
        slabs = [_dot(cqn_scr[...], wqb_ref[:, hd * HEAD_SLAB:(hd + 1) * HEAD_SLAB]) for hd in heads]
        kvs = [_dot(ckvn_scr[...], wkvb_ref[:, hd * KV_HEAD:(hd + 1) * KV_HEAD]) for hd in heads]
        ssq = [lane_sum(s * s) for s in slabs]
        ssk = [lane_sum(kv[:, 0:QK_NOPE] * kv[:, 0:QK_NOPE]) for kv in kvs]
        glo = (g0 // HEAD_GROUP) * gate_cols
        zas = [(lo, _dot(h_scr[...], wtail_ref[:, ROPE_SLAB + lo:ROPE_SLAB + lo + 2 * MXU_N]))
               for lo in range(glo, glo + gate_cols, 2 * MXU_N)]
        for hd, slab, kv, sq, sk in zip(heads, slabs, kvs, ssq, ssk):
            r = lax.rsqrt(sq * (1.0 / QK_HEAD) + EPS)
            q_ref[0, hd, :, 0:QK_NOPE] = (slab[:, 0:QK_NOPE] * r * gq[:, 0:QK_NOPE]).astype(_BF16)
            q_rope = rope(slab[:, QK_NOPE:HEAD_SLAB] * r * gq[:, QK_NOPE:HEAD_SLAB])
            q_ref[0, hd, :, QK_NOPE:HEAD_SLAB] = (q_rope + qb_ref[...]).astype(_BF16)
            r = lax.rsqrt((sk + ss_r) * (1.0 / QK_HEAD) + EPS)
            kt_ref[0, hd, 0:QK_NOPE, :] = (kv[:, 0:QK_NOPE] * r * gk[:, 0:QK_NOPE]).T.astype(_BF16)
            kt_ref[0, hd, QK_NOPE:HEAD_SLAB, :] = (kr_rot * r + kb_ref[...]).T.astype(_BF16)
            v_ref[0, hd, :, :] = kv[:, QK_NOPE:KV_HEAD].astype(_BF16)
        for lo, za in zas:
            sz_ref[:, lo:lo + 2 * MXU_N] = (za * _sigmoid(za)).astype(_BF16)


def _mla_proj(layer, x2d, mod3, norm_g, w_in, wtail, q_a_g, wqb, kv_a_g, wkvb, gq, gk, qb, kb, cos_t, sin_t,
              bsz, seq):
    t = x2d.shape[0]
    tm = MLA_TM
    tpb = seq // tm
    tok = lambda width: pl.BlockSpec((tm, width), lambda i: (i, 0))
    head = lambda width: pl.BlockSpec((1, N_HEADS, tm, width), lambda i: (i // tpb, 0, i % tpb, 0))
    return pl.pallas_call(
        _mla_proj_body,
        grid=(t // tm,),
        in_specs=[
            tok(D_MODEL),
            pl.BlockSpec((1, 3, D_MODEL), lambda i: (i // tpb, 0, 0)),
            _resident((1, D_MODEL)),
            _resident((None, D_MODEL, Q_LORA), (layer, 0, OFF_CQ // Q_LORA)),
            _resident((None, D_MODEL, KV_LORA), (layer, 0, OFF_CKV // KV_LORA)),
            _resident((D_MODEL, TAIL_COLS)),
            _resident((1, Q_LORA)),
            _resident((Q_LORA, N_HEADS * HEAD_SLAB)),
            _resident((1, KV_LORA)),
            _resident((KV_LORA, N_HEADS * (QK_NOPE + V_HEAD))),
            _resident((1, HEAD_SLAB)),
            _resident((1, HEAD_SLAB)),
            _resident((1, ROPE_SLAB)),
            _resident((1, ROPE_SLAB)),
            tok(ROPE_SLAB),
            tok(ROPE_SLAB),
        ],
        out_specs=[
            tok(D_ATTN),
            head(HEAD_SLAB),
            pl.BlockSpec((1, N_HEADS, HEAD_SLAB, tm), lambda i: (i // tpb, 0, 0, i % tpb)),
            head(V_HEAD),
        ],
        out_shape=[
            jax.ShapeDtypeStruct((t, D_ATTN), _BF16),
            jax.ShapeDtypeStruct((bsz, N_HEADS, seq, HEAD_SLAB), _BF16),
            jax.ShapeDtypeStruct((bsz, N_HEADS, HEAD_SLAB, seq), _BF16),
            jax.ShapeDtypeStruct((bsz, N_HEADS, seq, V_HEAD), _BF16),
        ],
        scratch_shapes=[
            pltpu.VMEM((D_MODEL, Q_LORA), _BF16),
            pltpu.VMEM((D_MODEL, KV_LORA), _BF16),
            pltpu.VMEM((tm, D_MODEL), _BF16),
            pltpu.VMEM((tm, Q_LORA), _BF16),
            pltpu.VMEM((tm, KV_LORA), _BF16),
        ],
        compiler_params=pltpu.CompilerParams(
            dimension_semantics=("arbitrary",), vmem_limit_bytes=VMEM_LIMIT),
        name="mla_proj",
    )(x2d, mod3, norm_g, w_in, w_in, wtail, q_a_g, wqb, kv_a_g, wkvb, gq, gk, qb, kb, cos_t, sin_t)


def _attn_body(online_max, q_ref, kt_ref, v_ref, sz_ref, o_ref,
               sa0_scr, sa1_scr, sb_scr, m_scr, l_scr, acc_scr):
    tq = ATTN_TQ
    tk = ATTN_TK
    half = tq // 2
    assert tq == 2 * tk
    nq = q_ref.shape[2] // tq
    assert nq % 2 == 0

    def scores(qi, j, r0, nr):
        qstart = pl.multiple_of(qi * tq + r0, half)
        kstart = pl.multiple_of(j * tk, tk)
        return _dot(q_ref[0, 0, pl.ds(qstart, nr), :], kt_ref[0, 0, :, pl.ds(kstart, tk)])

    def update(j, s, r0, nr, triangular):
        start = pl.multiple_of(j * tk, tk)
        v = v_ref[0, 0, pl.ds(start, tk), :]
        if triangular:
            row = lax.broadcasted_iota(jnp.int32, (nr, tk), 0)
            col = lax.broadcasted_iota(jnp.int32, (nr, tk), 1)
            s = jnp.where(col <= row, s, -jnp.inf)
        tiles = [s[:, t * LANES:(t + 1) * LANES] for t in range(tk // LANES)]
        if not online_max:
            ps = [jnp.exp2(t) for t in tiles]
            l_scr[r0:r0 + nr, :] += functools.reduce(jnp.add, ps)
            p = jnp.concatenate(ps, axis=-1).astype(_BF16)
            acc_scr[r0:r0 + nr, :] += _dot(p, v)
            return
        mx = functools.reduce(jnp.maximum, tiles)
        m_prev = m_scr[r0:r0 + nr, :]
        m_new = jnp.maximum(m_prev, jnp.max(mx, axis=-1, keepdims=True))
        alpha = jnp.exp2(m_prev - m_new)
        ps = [jnp.exp2(t - m_new) for t in tiles]
        l_scr[r0:r0 + nr, :] = alpha * l_scr[r0:r0 + nr, :] + functools.reduce(jnp.add, ps)
        p = jnp.concatenate(ps, axis=-1).astype(_BF16)
        acc_scr[r0:r0 + nr, :] = alpha * acc_scr[r0:r0 + nr, :] + _dot(p, v)
        m_scr[r0:r0 + nr, :] = m_new

    def query_block(qi, sa_scr, sa_next_scr):
        if online_max:
            m_scr[...] = jnp.full_like(m_scr, -jnp.inf)
        l_scr[...] = jnp.zeros_like(l_scr)
        acc_scr[...] = jnp.zeros_like(acc_scr)

        def pair(jj, carry):
            j = 2 * jj
            sb_scr[...] = scores(qi, j + 1, 0, tq)
            update(j, sa_scr[...], 0, tq, False)
            sa_scr[...] = scores(qi, j + 2, 0, tq)
            update(j + 1, sb_scr[...], 0, tq, False)
            return carry

        lax.fori_loop(0, qi, pair, 0)

        jd = 2 * qi
        sb_scr[half:tq, :] = scores(qi, jd + 1, half, half)
        update(jd, sa_scr[0:half, :], 0, half, True)
        sa_next_scr[...] = scores(jnp.minimum(qi + 1, nq - 1), 0, 0, tq)
        update(jd, sa_scr[half:tq, :], half, half, False)
        update(jd + 1, sb_scr[half:tq, :], half, half, True)

        rows = pl.ds(pl.multiple_of(qi * tq, tq), tq)
        o = acc_scr[...] / jnp.sum(l_scr[...], axis=-1, keepdims=True)
        o_ref[rows, :] = (o * sz_ref[rows, :].astype(_F32)).astype(_BF16)

    sa0_scr[...] = scores(0, 0, 0, tq)

    def two_blocks(ib, carry):
        query_block(2 * ib, sa0_scr, sa1_scr)
        query_block(2 * ib + 1, sa1_scr, sa0_scr)
        return carry

    lax.fori_loop(0, nq // 2, two_blocks, 0)


def _attn_shifted_body(q_ref, kt_ref, v_ref, sz_ref, o_ref, pa0_scr, pa1_scr, pb_scr, lnext_scr, l_scr, acc_scr):
    tq = ATTN_TQ
    tk = ATTN_TK
    half = tq // 2
    assert tq == 2 * tk
    nq = q_ref.shape[2] // tq
    assert nq % 2 == 0
    col_minus_row = (lax.broadcasted_iota(jnp.int32, (tq, tk), 1)
                     - lax.broadcasted_iota(jnp.int32, (tq, tk), 0))

    def probs(qi, j, r0, nr, masked):
        qstart = pl.multiple_of(qi * tq + r0, half)
        kstart = pl.multiple_of(j * tk, tk)
        s = _dot(q_ref[0, 0, pl.ds(qstart, nr), :], kt_ref[0, 0, :, pl.ds(kstart, tk)])
        if masked:
            s = jnp.where(col_minus_row[r0:r0 + nr, :] <= qi * tq - j * tk, s, -jnp.inf)
        ps = [jnp.exp2(s[:, t * LANES:(t + 1) * LANES]) for t in range(tk // LANES)]
        return jnp.concatenate(ps, axis=-1).astype(_BF16), functools.reduce(jnp.add, ps)

    def values(j):
        return v_ref[0, 0, pl.ds(pl.multiple_of(j * tk, tk), tk), :]

    def query_block(qi, pa_scr, pa_next_scr):
        l_scr[...] = lnext_scr[...]
        acc_scr[...] = jnp.zeros_like(acc_scr)

        def pair(jj, carry):
            j = 2 * jj
            p, ls = probs(qi, j + 1, 0, tq, False)
            pb_scr[...] = p
            l_scr[...] += ls
            acc_scr[...] += _dot(pa_scr[...], values(j))
            p, ls = probs(qi, j + 2, 0, tq, True)
            pa_scr[...] = p
            l_scr[...] += ls
            acc_scr[...] += _dot(pb_scr[...], values(j + 1))
            return carry

        lax.fori_loop(0, qi, pair, 0)

        jd = 2 * qi
        p, ls = probs(qi, jd + 1, half, half, True)
        pb_scr[half:tq, :] = p
        l_scr[half:tq, :] += ls
        acc_scr[...] += _dot(pa_scr[...], values(jd))
        nxt = jnp.minimum(qi + 1, nq - 1)
        p, ls = probs(nxt, 0, 0, tq, True)
        pa_next_scr[...] = p
        lnext_scr[...] = ls
        acc_scr[half:tq, :] += _dot(pb_scr[half:tq, :], values(jd + 1))

        rows = pl.ds(pl.multiple_of(qi * tq, tq), tq)
        o = acc_scr[...] / jnp.sum(l_scr[...], axis=-1, keepdims=True)
        o_ref[rows, :] = (o * sz_ref[rows, :].astype(_F32)).astype(_BF16)

    p, ls = probs(0, 0, 0, tq, True)
    pa0_scr[...] = p
    lnext_scr[...] = ls

    def two_blocks(ib, carry):
        query_block(2 * ib, pa0_scr, pa1_scr)
        query_block(2 * ib + 1, pa1_scr, pa0_scr)
        return carry

    lax.fori_loop(0, nq // 2, two_blocks, 0)


def _attention(online_max, q, k, v, sz):
    bsz, _, seq, _ = q.shape
    tq = ATTN_TQ
    per_head = lambda width: pl.BlockSpec((1, 1, seq, width), lambda b, h: (b, h, 0, 0))
    stats = [pltpu.VMEM((tq, LANES), _F32), pltpu.VMEM((tq, LANES), _F32), pltpu.VMEM((tq, V_HEAD), _F32)]
    if online_max:
        body = functools.partial(_attn_body, True)
        scratch = [pltpu.VMEM((tq, ATTN_TK), _F32)] * 3 + stats
    else:
        body = _attn_shifted_body
        scratch = [pltpu.VMEM((tq, ATTN_TK), _BF16)] * 3 + stats
    return pl.pallas_call(
        body,
        grid=(bsz, N_HEADS),
        in_specs=[
            per_head(HEAD_SLAB),
            pl.BlockSpec((1, 1, HEAD_SLAB, seq), lambda b, h: (b, h, 0, 0)),
            per_head(V_HEAD),
            pl.BlockSpec((seq, V_HEAD), lambda b, h: (b, h)),
        ],
        out_specs=pl.BlockSpec((seq, V_HEAD), lambda b, h: (b, h)),
        out_shape=jax.ShapeDtypeStruct((bsz * seq, D_ATTN), _BF16),
        scratch_shapes=scratch,
        compiler_params=pltpu.CompilerParams(
            dimension_semantics=("arbitrary", "arbitrary"), vmem_limit_bytes=VMEM_LIMIT),
        name="attention_online_max" if online_max else "attention_shifted",
    )(q, k, v, sz)


def _out_proj_body(yc_ref, ya_ref, x_ref, mod_ref, w_ref, o_ref):
    y = _dot(yc_ref[...], w_ref[0:D_CONV, :]) + _dot(ya_ref[...], w_ref[D_CONV:D_MIX, :])
    o_ref[...] = x_ref[...] + mod_ref[0, 2:3, :] * y


def _out_proj(yconv, yattn, x2d, mod3, wout, seq):
    t = x2d.shape[0]
    tm = OUT_TM
    tpb = seq // tm
    return pl.pallas_call(
        _out_proj_body,
        grid=(t // tm,),
        in_specs=[
            pl.BlockSpec((tm, D_CONV), lambda i: (i, 0)),
            pl.BlockSpec((tm, D_ATTN), lambda i: (i, 0)),
            pl.BlockSpec((tm, D_MODEL), lambda i: (i, 0)),
            pl.BlockSpec((1, 3, D_MODEL), lambda i: (i // tpb, 0, 0)),
            _resident((D_MIX, D_MODEL)),
        ],
        out_specs=pl.BlockSpec((tm, D_MODEL), lambda i: (i, 0)),
        out_shape=jax.ShapeDtypeStruct((t, D_MODEL), _F32),
        compiler_params=pltpu.CompilerParams(
            dimension_semantics=("arbitrary",), vmem_limit_bytes=VMEM_LIMIT),
        name="out_proj",
    )(yconv, yattn, x2d, mod3, wout)


def _score_shift(q_g, k_g, qscale):
    bound = QK_HEAD * qscale * jnp.max(jnp.abs(q_g)) * jnp.max(jnp.abs(k_g))
    shift = (bound * 1.03 + 1.0).astype(_BF16).astype(_F32)
    use_shift = bound + shift <= EXP2_NORMAL_RANGE
    return jnp.where(use_shift, shift, 0.0), use_shift


def _rope_slab_cols(w):
    z = jnp.zeros(w.shape[:-1] + (HALF,), w.dtype)
    return jnp.concatenate([w[..., :HALF], z, w[..., HALF:], z], axis=-1)


def _layer(layer, x2d, mod, cos_t, sin_t, norm_g, w_in, conv_w, q_a_g, w_q_b, kv_a_g, w_kv_b, q_g, k_g,
           w_out, bsz, seq):
    wtail = jnp.concatenate(
        [_rope_slab_cols(w_in[layer, :, OFF_KR:OFF_ZA]), w_in[layer, :, OFF_ZA:]], axis=-1).astype(_BF16)
    wq = w_q_b.reshape(Q_LORA, N_HEADS, QK_HEAD)
    wqb = jnp.concatenate([wq[..., :QK_NOPE], _rope_slab_cols(wq[..., QK_NOPE:])], axis=-1)
    wqb = wqb.reshape(Q_LORA, N_HEADS * HEAD_SLAB).astype(_BF16)
    wkvb = w_kv_b.astype(_BF16)
    qscale = math.log2(math.e) / math.sqrt(QK_HEAD)
    gq = jnp.concatenate([q_g[:QK_NOPE], _rope_slab_cols(q_g[QK_NOPE:])]).reshape(1, HEAD_SLAB) * qscale
    gk = jnp.concatenate([k_g[:QK_NOPE], _rope_slab_cols(k_g[QK_NOPE:])]).reshape(1, HEAD_SLAB)
    mod3 = mod.reshape(bsz, 3, D_MODEL)
    shift, use_shift = _score_shift(q_g, k_g, qscale)
    pad_lane = jnp.arange(ROPE_SLAB) == SHIFT_LANE
    qb = jnp.where(pad_lane, -shift, 0.0).reshape(1, ROPE_SLAB)
    kb = jnp.where(pad_lane, 1.0, 0.0).reshape(1, ROPE_SLAB)

    ng = norm_g.reshape(1, D_MODEL)
    yconv = _conv_proj(layer, x2d, mod3, ng, w_in, conv_w, seq)
    sz, q, k, v = _mla_proj(
        layer, x2d, mod3, ng, w_in, wtail, q_a_g.reshape(1, Q_LORA), wqb,
        kv_a_g.reshape(1, KV_LORA), wkvb, gq, gk, qb, kb, cos_t, sin_t, bsz, seq)
    yattn = lax.cond(use_shift, functools.partial(_attention, False), functools.partial(_attention, True),
                     q, k, v, sz)
    return _out_proj(yconv, yattn, x2d, mod3, w_out.astype(_BF16), seq)


def kernel(x, c, positions, ada_w, ada_b, norm_g, w_in, conv_w, q_a_g, w_q_b, kv_a_g, w_kv_b, q_g, k_g, w_out):
    bsz, seq, _ = x.shape
    depth = ada_w.shape[0]
    cos_t, sin_t = _rope_tables(positions)
    x2d = x.reshape(bsz * seq, D_MODEL)
    for l in range(depth):
        mod = _adaln_mod(c, ada_w[l], ada_b[l])
        x2d = _layer(l, x2d, mod, cos_t, sin_t, norm_g[l], w_in, conv_w[l], q_a_g[l], w_q_b[l],
                     kv_a_g[l], w_kv_b[l], q_g[l], k_g[l], w_out[l], bsz, seq)
    return x2d.reshape(bsz, seq, D_MODEL)
```

```python
import functools
import math

import jax
import jax.numpy as jnp
from jax import lax
from jax.experimental import pallas as pl
from jax.experimental.pallas import tpu as pltpu

D_MODEL = 2048
D_CONV = 1024
CONV_WIDTH = 3
N_HEADS = 8
QK_NOPE = 128
QK_ROPE = 64
QK_HEAD = QK_NOPE + QK_ROPE
V_HEAD = 128
D_ATTN = N_HEADS * V_HEAD
Q_LORA = 512
KV_LORA = 256
ROPE_BASE = 10000.0
D_MIX = D_CONV + D_ATTN
EPS = 1e-6

KV_HEAD = QK_NOPE + V_HEAD
MXU_N = 256
LANES = 128
SUBLANES = 8
ROPE_SLAB = LANES
HEAD_SLAB = QK_NOPE + ROPE_SLAB
HALF = QK_ROPE // 2
SHIFT_LANE = HALF
EXP2_NORMAL_RANGE = 120.0

OFF_XC = 0
OFF_BC = D_CONV
OFF_CC = 2 * D_CONV
OFF_ZC = 3 * D_CONV
OFF_CQ = 4 * D_CONV
OFF_CKV = OFF_CQ + Q_LORA
OFF_KR = OFF_CKV + KV_LORA
OFF_ZA = OFF_KR + QK_ROPE
TAIL_COLS = ROPE_SLAB + D_ATTN

ADA_TN = 1024
ROPE_TT = 1024
CONV_TM = 512
MLA_TM = 512
CONV_CW = 256
HEAD_GROUP = 4
ATTN_TQ = 1024
ATTN_TK = 512
OUT_TM = 512
VMEM_LIMIT = 56 * 1024 * 1024

_BF16 = jnp.bfloat16
_F32 = jnp.float32


def _sigmoid(z):
    return 1.0 / (1.0 + jnp.exp(-z))


def _dot(a, b):
    return jnp.dot(a, b, preferred_element_type=_F32)


def _adaln_body(ct_ref, w_ref, b_ref, o_ref):
    ct = ct_ref[...]
    sc = ct * _sigmoid(ct)
    w = w_ref[...]
    rows = [jnp.sum(w * sc[:, b:b + 1], axis=0, keepdims=True) for b in range(ct.shape[1])]
    o_ref[...] = jnp.concatenate(rows, axis=0) + b_ref[...]


def _adaln_mod(c, ada_w, ada_b):
    bsz = c.shape[0]
    n = ada_w.shape[1]
    return pl.pallas_call(
        _adaln_body,
        grid=(n // ADA_TN,),
        in_specs=[
            pl.BlockSpec((D_MODEL, bsz), lambda j: (0, 0)),
            pl.BlockSpec((D_MODEL, ADA_TN), lambda j: (0, j)),
            pl.BlockSpec((1, ADA_TN), lambda j: (0, j)),
        ],
        out_specs=pl.BlockSpec((bsz, ADA_TN), lambda j: (0, j)),
        out_shape=jax.ShapeDtypeStruct((bsz, n), _F32),
        name="adaln_mod",
    )(c.T, ada_w, ada_b.reshape(1, n))


def _rope_body(pos_ref, f_ref, cos_ref, sin_ref):
    ang = f_ref[...] * pos_ref[...].astype(_F32)
    c = jnp.cos(ang)
    s = jnp.sin(ang)
    z = jnp.zeros_like(c)
    cos_ref[...] = jnp.concatenate([c, z, c, z], axis=0).T
    sin_ref[...] = jnp.concatenate([-s, z, s, z], axis=0).T


def _rope_tables(positions):
    t = positions.size
    inv_freq = ROPE_BASE ** (-jnp.arange(0, QK_ROPE, 2, dtype=_F32) / QK_ROPE)
    out = pl.BlockSpec((ROPE_TT, ROPE_SLAB), lambda i: (i, 0))
    return pl.pallas_call(
        _rope_body,
        grid=(t // ROPE_TT,),
        in_specs=[pl.BlockSpec((1, ROPE_TT), lambda i: (0, i)), pl.BlockSpec((HALF, 1), lambda i: (0, 0))],
        out_specs=[out, out],
        out_shape=[jax.ShapeDtypeStruct((t, ROPE_SLAB), _F32)] * 2,
        name="rope_tables",
    )(positions.reshape(1, t), inv_freq.reshape(HALF, 1))


def _modulated_norm(x_ref, mod_ref, ng_ref):
    x = x_ref[...]
    xn = x * lax.rsqrt(jnp.mean(x * x, axis=-1, keepdims=True) + EPS)
    a = ng_ref[...] * (1.0 + mod_ref[0, 1:2, :])
    return (xn * a + mod_ref[0, 0:1, :]).astype(_BF16)


def _conv_proj_body(tiles_per_batch, x_ref, mod_ref, ng_ref, win_ref, convw_ref, yconv_ref,
                    h_scr, carry_scr, vbuf_scr):
    tm = x_ref.shape[0]

    @pl.when(pl.program_id(0) % tiles_per_batch == 0)
    def _():
        carry_scr[...] = jnp.zeros_like(carry_scr)

    h_scr[...] = _modulated_norm(x_ref, mod_ref, ng_ref)

    def proj(off, width):
        return _dot(h_scr[...], win_ref[:, off:off + width])

    for lo in range(0, D_CONV, CONV_CW):
        xc = proj(OFF_XC + lo, CONV_CW)
        bc = proj(OFF_BC + lo, CONV_CW)
        cc = proj(OFF_CC + lo, CONV_CW)
        zc = proj(OFF_ZC + lo, CONV_CW)
        v0 = cc * xc
        vbuf_scr[0:SUBLANES, :] = carry_scr[:, lo:lo + CONV_CW]
        vbuf_scr[SUBLANES:SUBLANES + tm, :] = v0
        carry_scr[:, lo:lo + CONV_CW] = v0[tm - SUBLANES:, :]
        v1 = vbuf_scr[SUBLANES - 1:SUBLANES - 1 + tm, :]
        v2 = vbuf_scr[SUBLANES - 2:SUBLANES - 2 + tm, :]
        w = convw_ref[:, lo:lo + CONV_CW]
        conv = w[0:1, :] * v2 + w[1:2, :] * v1 + w[2:3, :] * v0
        yconv_ref[:, lo:lo + CONV_CW] = (bc * conv * (zc * _sigmoid(zc))).astype(_BF16)


def _resident(shape, index=None):
    index = (0,) * len(shape) if index is None else index
    return pl.BlockSpec(shape, lambda i: index, pipeline_mode=pl.Buffered(1))


def _conv_proj(layer, x2d, mod3, norm_g, win_all, conv_w, seq):
    t = x2d.shape[0]
    tm = CONV_TM
    tpb = seq // tm
    return pl.pallas_call(
        functools.partial(_conv_proj_body, tpb),
        grid=(t // tm,),
        in_specs=[
            pl.BlockSpec((tm, D_MODEL), lambda i: (i, 0)),
            pl.BlockSpec((1, 3, D_MODEL), lambda i: (i // tpb, 0, 0)),
            _resident((1, D_MODEL)),
            _resident((None, D_MODEL, 4 * D_CONV), (layer, 0, 0)),
            _resident((CONV_WIDTH, D_CONV)),
        ],
        out_specs=pl.BlockSpec((tm, D_CONV), lambda i: (i, 0)),
        out_shape=jax.ShapeDtypeStruct((t, D_CONV), _BF16),
        scratch_shapes=[
            pltpu.VMEM((tm, D_MODEL), _BF16),
            pltpu.VMEM((SUBLANES, D_CONV), _F32),
            pltpu.VMEM((SUBLANES + tm, CONV_CW), _F32),
        ],
        compiler_params=pltpu.CompilerParams(
            dimension_semantics=("arbitrary",), vmem_limit_bytes=VMEM_LIMIT),
        name="conv_proj",
    )(x2d, mod3, norm_g, win_all, conv_w)


def _mla_proj_body(x_ref, mod_ref, ng_ref, wcq_ref, wckv_ref, wtail_ref, qag_ref, wqb_ref, kvag_ref, wkvb_ref,
                   gq_ref, gk_ref, qb_ref, kb_ref, cos_ref, sin_ref,
                   sz_ref, q_ref, kt_ref, v_ref, h_scr, cqn_scr, ckvn_scr):
    h_scr[...] = _modulated_norm(x_ref, mod_ref, ng_ref)
    cos_t = cos_ref[...]
    sin_t = sin_ref[...]

    def rope(slab):
        return slab * cos_t + pltpu.roll(slab, ROPE_SLAB // 2, axis=1) * sin_t

    def rms(parts, width):
        ss = sum(jnp.sum(p * p, axis=-1, keepdims=True) for p in parts)
        return lax.rsqrt(ss * (1.0 / width) + EPS)

    cq = _dot(h_scr[...], wcq_ref[...])
    cqn_scr[...] = (cq * rms([cq], Q_LORA) * qag_ref[...]).astype(_BF16)
    ckv = _dot(h_scr[...], wckv_ref[...])
    ckvn_scr[...] = (ckv * rms([ckv], KV_LORA) * kvag_ref[...]).astype(_BF16)

    kr = _dot(h_scr[...], wtail_ref[:, 0:ROPE_SLAB])
    gk = gk_ref[...]
    gq = gq_ref[...]
    kr_rot = rope(kr * gk[:, QK_NOPE:HEAD_SLAB])

    def lane_sum(sq):
        ones = jnp.ones((sq.shape[1], LANES), _BF16)
        return _dot(sq.astype(_BF16), ones)

    ss_r = lane_sum(kr * kr)
    gate_cols = D_ATTN // (N_HEADS // HEAD_GROUP)
    for g0 in range(0, N_HEADS, HEAD_GROUP):
        heads = range(g0, g0 + HEAD_GROUP)
        slabs = [_dot(cqn_scr[...], wqb_ref[:, hd * HEAD_SLAB:(hd + 1) * HEAD_SLAB]) for hd in heads]
        kvs = [_dot(ckvn_scr[...], wkvb_ref[:, hd * KV_HEAD:(hd + 1) * KV_HEAD]) for hd in heads]
        ssq = [lane_sum(s * s) for s in slabs]
        ssk = [lane_sum(kv[:, 0:QK_NOPE] * kv[:, 0:QK_NOPE]) for kv in kvs]
        glo = (g0 // HEAD_GROUP) * gate_cols
        zas = [(lo, _dot(h_scr[...], wtail_ref[:, ROPE_SLAB + lo:ROPE_SLAB + lo + 2 * MXU_N]))
               for lo in range(glo, glo + gate_cols, 2 * MXU_N)]
        for hd, slab, kv, sq, sk in zip(heads, slabs, kvs, ssq, ssk):
            r = lax.rsqrt(sq * (1.0 / QK_HEAD) + EPS)
            q_ref[0, hd, :, 0:QK_NOPE] = (slab[:, 0:QK_NOPE] * r * gq[:, 0:QK_NOPE]).astype(_BF16)
            q_rope = rope(slab[:, QK_NOPE:HEAD_SLAB] * r * gq[:, QK_NOPE:HEAD_SLAB])
            q_ref[0, hd, :, QK_NOPE:HEAD_SLAB] = (q_rope + qb_ref[...]).astype(_BF16)
            r = lax.rsqrt((sk + ss_r) * (1.0 / QK_HEAD) + EPS)
            kt_ref[0, hd, 0:QK_NOPE, :] = (kv[:, 0:QK_NOPE] * r * gk[:, 0:QK_NOPE]).T.astype(_BF16)
            kt_ref[0, hd, QK_NOPE:HEAD_SLAB, :] = (kr_rot * r + kb_ref[...]).T.astype(_BF16)
            v_ref[0, hd, :, :] = kv[:, QK_NOPE:KV_HEAD].astype(_BF16)
        for lo, za in zas:
            sz_ref[:, lo:lo + 2 * MXU_N] = (za * _sigmoid(za)).astype(_BF16)


def _mla_proj(layer, x2d, mod3, norm_g, win_all, wtail, q_a_g, wqb, kv_a_g, wkvb, gq, gk, qb, kb, cos_t, sin_t,
              bsz, seq):
    t = x2d.shape[0]
    tm = MLA_TM
    tpb = seq // tm
    tok = lambda width: pl.BlockSpec((tm, width), lambda i: (i, 0))
    head = lambda width: pl.BlockSpec((1, N_HEADS, tm, width), lambda i: (i // tpb, 0, i % tpb, 0))
    return pl.pallas_call(
        _mla_proj_body,
        grid=(t // tm,),
        in_specs=[
            tok(D_MODEL),
            pl.BlockSpec((1, 3, D_MODEL), lambda i: (i // tpb, 0, 0)),
            _resident((1, D_MODEL)),
            _resident((None, D_MODEL, Q_LORA), (layer, 0, OFF_CQ // Q_LORA)),
            _resident((None, D_MODEL, KV_LORA), (layer, 0, OFF_CKV // KV_LORA)),
            _resident((D_MODEL, TAIL_COLS)),
            _resident((1, Q_LORA)),
            _resident((Q_LORA, N_HEADS * HEAD_SLAB)),
            _resident((1, KV_LORA)),
            _resident((KV_LORA, N_HEADS * (QK_NOPE + V_HEAD))),
            _resident((1, HEAD_SLAB)),
            _resident((1, HEAD_SLAB)),
            _resident((1, ROPE_SLAB)),
            _resident((1, ROPE_SLAB)),
            tok(ROPE_SLAB),
            tok(ROPE_SLAB),
        ],
        out_specs=[
            tok(D_ATTN),
            head(HEAD_SLAB),
            pl.BlockSpec((1, N_HEADS, HEAD_SLAB, tm), lambda i: (i // tpb, 0, 0, i % tpb)),
            head(V_HEAD),
        ],
        out_shape=[
            jax.ShapeDtypeStruct((t, D_ATTN), _BF16),
            jax.ShapeDtypeStruct((bsz, N_HEADS, seq, HEAD_SLAB), _BF16),
            jax.ShapeDtypeStruct((bsz, N_HEADS, HEAD_SLAB, seq), _BF16),
            jax.ShapeDtypeStruct((bsz, N_HEADS, seq, V_HEAD), _BF16),
        ],
        scratch_shapes=[
            pltpu.VMEM((tm, D_MODEL), _BF16),
            pltpu.VMEM((tm, Q_LORA), _BF16),
            pltpu.VMEM((tm, KV_LORA), _BF16),
        ],
        compiler_params=pltpu.CompilerParams(
            dimension_semantics=("arbitrary",), vmem_limit_bytes=VMEM_LIMIT),
        name="mla_proj",
    )(x2d, mod3, norm_g, win_all, win_all, wtail, q_a_g, wqb, kv_a_g, wkvb, gq, gk, qb, kb, cos_t, sin_t)


def _attn_online_body(q_ref, kt_ref, v_ref, sz_ref, o_ref, sa0_scr, sa1_scr, sb_scr, m_scr, l_scr, acc_scr):
    tq = ATTN_TQ
    tk = ATTN_TK
    half = tq // 2
    assert tq == 2 * tk
    nq = q_ref.shape[2] // tq
    assert nq % 2 == 0

    def scores(qi, j, r0, nr):
        qstart = pl.multiple_of(qi * tq + r0, half)
        kstart = pl.multiple_of(j * tk, tk)
        return _dot(q_ref[0, 0, pl.ds(qstart, nr), :], kt_ref[0, 0, :, pl.ds(kstart, tk)])

    def update(j, s, r0, nr, triangular):
        start = pl.multiple_of(j * tk, tk)
        v = v_ref[0, 0, pl.ds(start, tk), :]
        if triangular:
            row = lax.broadcasted_iota(jnp.int32, (nr, tk), 0)
            col = lax.broadcasted_iota(jnp.int32, (nr, tk), 1)
            s = jnp.where(col <= row, s, -jnp.inf)
        tiles = [s[:, t * LANES:(t + 1) * LANES] for t in range(tk // LANES)]
        mx = functools.reduce(jnp.maximum, tiles)
        m_prev = m_scr[r0:r0 + nr, :]
        m_new = jnp.maximum(m_prev, jnp.max(mx, axis=-1, keepdims=True))
        alpha = jnp.exp2(m_prev - m_new)
        ps = [jnp.exp2(t - m_new) for t in tiles]
        l_scr[r0:r0 + nr, :] = alpha * l_scr[r0:r0 + nr, :] + functools.reduce(jnp.add, ps)
        p = jnp.concatenate(ps, axis=-1).astype(_BF16)
        acc_scr[r0:r0 + nr, :] = alpha * acc_scr[r0:r0 + nr, :] + _dot(p, v)
        m_scr[r0:r0 + nr, :] = m_new

    def query_block(qi, sa_scr, sa_next_scr):
        m_scr[...] = jnp.full_like(m_scr, -jnp.inf)
        l_scr[...] = jnp.zeros_like(l_scr)
        acc_scr[...] = jnp.zeros_like(acc_scr)

        def pair(jj, carry):
            j = 2 * jj
            sb_scr[...] = scores(qi, j + 1, 0, tq)
            update(j, sa_scr[...], 0, tq, False)
            sa_scr[...] = scores(qi, j + 2, 0, tq)
            update(j + 1, sb_scr[...], 0, tq, False)
            return carry

        lax.fori_loop(0, qi, pair, 0)

        jd = 2 * qi
        sb_scr[half:tq, :] = scores(qi, jd + 1, half, half)
        update(jd, sa_scr[0:half, :], 0, half, True)
        sa_next_scr[...] = scores(jnp.minimum(qi + 1, nq - 1), 0, 0, tq)
        update(jd, sa_scr[half:tq, :], half, half, False)
        update(jd + 1, sb_scr[half:tq, :], half, half, True)

        rows = pl.ds(pl.multiple_of(qi * tq, tq), tq)
        o = acc_scr[...] / jnp.sum(l_scr[...], axis=-1, keepdims=True)
        o_ref[rows, :] = (o * sz_ref[rows, :].astype(_F32)).astype(_BF16)

    sa0_scr[...] = scores(0, 0, 0, tq)

    def two_blocks(ib, carry):
        query_block(2 * ib, sa0_scr, sa1_scr)
        query_block(2 * ib + 1, sa1_scr, sa0_scr)
        return carry

    lax.fori_loop(0, nq // 2, two_blocks, 0)


def _attn_shifted_body(q_ref, kt_ref, v_ref, sz_ref, o_ref, pa0_scr, pa1_scr, pb_scr, lnext_scr, l_scr, acc_scr):
    tq = ATTN_TQ
    tk = ATTN_TK
    half = tq // 2
    assert tq == 2 * tk
    nq = q_ref.shape[2] // tq
    assert nq % 2 == 0
    col_minus_row = (lax.broadcasted_iota(jnp.int32, (tq, tk), 1)
                     - lax.broadcasted_iota(jnp.int32, (tq, tk), 0))

    def probs(qi, j, r0, nr, masked):
        qstart = pl.multiple_of(qi * tq + r0, half)
        kstart = pl.multiple_of(j * tk, tk)
        s = _dot(q_ref[0, 0, pl.ds(qstart, nr), :], kt_ref[0, 0, :, pl.ds(kstart, tk)])
        if masked:
            s = jnp.where(col_minus_row[r0:r0 + nr, :] <= qi * tq - j * tk, s, -jnp.inf)
        ps = [jnp.exp2(s[:, t * LANES:(t + 1) * LANES]) for t in range(tk // LANES)]
        return jnp.concatenate(ps, axis=-1).astype(_BF16), functools.reduce(jnp.add, ps)

    def values(j):
        return v_ref[0, 0, pl.ds(pl.multiple_of(j * tk, tk), tk), :]

    def query_block(qi, pa_scr, pa_next_scr):
        l_scr[...] = lnext_scr[...]
        acc_scr[...] = jnp.zeros_like(acc_scr)

        def pair(jj, carry):
            j = 2 * jj
            p, ls = probs(qi, j + 1, 0, tq, False)
            pb_scr[...] = p
            l_scr[...] += ls
            acc_scr[...] += _dot(pa_scr[...], values(j))
            p, ls = probs(qi, j + 2, 0, tq, True)
            pa_scr[...] = p
            l_scr[...] += ls
            acc_scr[...] += _dot(pb_scr[...], values(j + 1))
            return carry

        lax.fori_loop(0, qi, pair, 0)

        jd = 2 * qi
        p, ls = probs(qi, jd + 1, half, half, True)
        pb_scr[half:tq, :] = p
        l_scr[half:tq, :] += ls
        acc_scr[...] += _dot(pa_scr[...], values(jd))
        nxt = jnp.minimum(qi + 1, nq - 1)
        p, ls = probs(nxt, 0, 0, tq, True)
        pa_next_scr[...] = p
        lnext_scr[...] = ls
        acc_scr[half:tq, :] += _dot(pb_scr[half:tq, :], values(jd + 1))

        rows = pl.ds(pl.multiple_of(qi * tq, tq), tq)
        o = acc_scr[...] / jnp.sum(l_scr[...], axis=-1, keepdims=True)
        o_ref[rows, :] = (o * sz_ref[rows, :].astype(_F32)).astype(_BF16)

    p, ls = probs(0, 0, 0, tq, True)
    pa0_scr[...] = p
    lnext_scr[...] = ls

    def two_blocks(ib, carry):
        query_block(2 * ib, pa0_scr, pa1_scr)
        query_block(2 * ib + 1, pa1_scr, pa0_scr)
        return carry

    lax.fori_loop(0, nq // 2, two_blocks, 0)


def _attention(online_max, q, k, v, sz):
    bsz, _, seq, _ = q.shape
    tq = ATTN_TQ
    per_head = lambda width: pl.BlockSpec((1, 1, seq, width), lambda b, h: (b, h, 0, 0))
    stats = [pltpu.VMEM((tq, LANES), _F32), pltpu.VMEM((tq, LANES), _F32), pltpu.VMEM((tq, V_HEAD), _F32)]
    if online_max:
        body = _attn_online_body
        scratch = [pltpu.VMEM((tq, ATTN_TK), _F32)] * 3 + stats
    else:
        body = _attn_shifted_body
        scratch = [pltpu.VMEM((tq, ATTN_TK), _BF16)] * 3 + stats
    return pl.pallas_call(
        body,
        grid=(bsz, N_HEADS),
        in_specs=[
            per_head(HEAD_SLAB),
            pl.BlockSpec((1, 1, HEAD_SLAB, seq), lambda b, h: (b, h, 0, 0)),
            per_head(V_HEAD),
            pl.BlockSpec((seq, V_HEAD), lambda b, h: (b, h)),
        ],
        out_specs=pl.BlockSpec((seq, V_HEAD), lambda b, h: (b, h)),
        out_shape=jax.ShapeDtypeStruct((bsz * seq, D_ATTN), _BF16),
        scratch_shapes=scratch,
        compiler_params=pltpu.CompilerParams(
            dimension_semantics=("arbitrary", "arbitrary"), vmem_limit_bytes=VMEM_LIMIT),
        name="attention_online_max" if online_max else "attention_shifted",
    )(q, k, v, sz)


def _out_proj_body(yc_ref, ya_ref, x_ref, mod_ref, w_ref, o_ref):
    y = _dot(yc_ref[...], w_ref[0:D_CONV, :]) + _dot(ya_ref[...], w_ref[D_CONV:D_MIX, :])
    o_ref[...] = x_ref[...] + mod_ref[0, 2:3, :] * y


def _out_proj(yconv, yattn, x2d, mod3, wout, seq):
    t = x2d.shape[0]
    tm = OUT_TM
    tpb = seq // tm
    return pl.pallas_call(
        _out_proj_body,
        grid=(t // tm,),
        in_specs=[
            pl.BlockSpec((tm, D_CONV), lambda i: (i, 0)),
            pl.BlockSpec((tm, D_ATTN), lambda i: (i, 0)),
            pl.BlockSpec((tm, D_MODEL), lambda i: (i, 0)),
            pl.BlockSpec((1, 3, D_MODEL), lambda i: (i // tpb, 0, 0)),
            _resident((D_MIX, D_MODEL)),
        ],
        out_specs=pl.BlockSpec((tm, D_MODEL), lambda i: (i, 0)),
        out_shape=jax.ShapeDtypeStruct((t, D_MODEL), _F32),
        compiler_params=pltpu.CompilerParams(
            dimension_semantics=("arbitrary",), vmem_limit_bytes=VMEM_LIMIT),
        name="out_proj",
    )(yconv, yattn, x2d, mod3, wout)


def _score_shift(q_g, k_g, qscale):
    bound = QK_HEAD * qscale * jnp.max(jnp.abs(q_g)) * jnp.max(jnp.abs(k_g))
    shift = (bound * 1.03 + 1.0).astype(_BF16).astype(_F32)
    use_shift = bound + shift <= EXP2_NORMAL_RANGE
    return jnp.where(use_shift, shift, 0.0), use_shift


def _rope_slab_cols(w):
    z = jnp.zeros(w.shape[:-1] + (HALF,), w.dtype)
    return jnp.concatenate([w[..., :HALF], z, w[..., HALF:], z], axis=-1)


def _layer(layer, x2d, mod, cos_t, sin_t, norm_g, w_in, win_all, conv_w, q_a_g, w_q_b, kv_a_g, w_kv_b, q_g, k_g,
           w_out, bsz, seq):
    wtail = jnp.concatenate([_rope_slab_cols(w_in[:, OFF_KR:OFF_ZA]), w_in[:, OFF_ZA:]], axis=-1).astype(_BF16)
    wq = w_q_b.reshape(Q_LORA, N_HEADS, QK_HEAD)
    wqb = jnp.concatenate([wq[..., :QK_NOPE], _rope_slab_cols(wq[..., QK_NOPE:])], axis=-1)
    wqb = wqb.reshape(Q_LORA, N_HEADS * HEAD_SLAB).astype(_BF16)
    wkvb = w_kv_b.astype(_BF16)
    qscale = math.log2(math.e) / math.sqrt(QK_HEAD)
    gq = jnp.concatenate([q_g[:QK_NOPE], _rope_slab_cols(q_g[QK_NOPE:])]).reshape(1, HEAD_SLAB) * qscale
    gk = jnp.concatenate([k_g[:QK_NOPE], _rope_slab_cols(k_g[QK_NOPE:])]).reshape(1, HEAD_SLAB)
    mod3 = mod.reshape(bsz, 3, D_MODEL)
    shift, use_shift = _score_shift(q_g, k_g, qscale)
    pad_lane = jnp.arange(ROPE_SLAB) == SHIFT_LANE
    qb = jnp.where(pad_lane, -shift, 0.0).reshape(1, ROPE_SLAB)
    kb = jnp.where(pad_lane, 1.0, 0.0).reshape(1, ROPE_SLAB)

    ng = norm_g.reshape(1, D_MODEL)
    yconv = _conv_proj(layer, x2d, mod3, ng, win_all, conv_w, seq)
    sz, q, k, v = _mla_proj(
        layer, x2d, mod3, ng, win_all, wtail, q_a_g.reshape(1, Q_LORA), wqb,
        kv_a_g.reshape(1, KV_LORA), wkvb, gq, gk, qb, kb, cos_t, sin_t, bsz, seq)
    yattn = lax.cond(use_shift, functools.partial(_attention, False), functools.partial(_attention, True),
                     q, k, v, sz)
    return _out_proj(yconv, yattn, x2d, mod3, w_out.astype(_BF16), seq)


def kernel(x, c, positions, ada_w, ada_b, norm_g, w_in, conv_w, q_a_g, w_q_b, kv_a_g, w_kv_b, q_g, k_g, w_out):
    bsz, seq, _ = x.shape
    depth = ada_w.shape[0]
    cos_t, sin_t = _rope_tables(positions)
    x2d = x.reshape(bsz * seq, D_MODEL)
    win_all = w_in.astype(_BF16)
    for l in range(depth):
        mod = _adaln_mod(c, ada_w[l], ada_b[l])
        x2d = _layer(l, x2d, mod, cos_t, sin_t, norm_g[l], w_in[l], win_all, conv_w[l], q_a_g[l], w_q_b[l],
                     kv_a_g[l], w_kv_b[l], q_g[l], k_g[l], w_out[l], bsz, seq)
    return x2d.reshape(bsz, seq, D_MODEL)
```

```python
import functools
import itertools
import math

import jax
import jax.numpy as jnp
from jax import lax
from jax.experimental import pallas as pl
from jax.experimental.pallas import tpu as pltpu

D_MODEL = 2048
D_CONV = 1024
CONV_WIDTH = 3
N_HEADS = 8
QK_NOPE = 128
QK_ROPE = 64
QK_HEAD = QK_NOPE + QK_ROPE
V_HEAD = 128
D_ATTN = N_HEADS * V_HEAD
Q_LORA = 512
KV_LORA = 256
ROPE_BASE = 10000.0
D_MIX = D_CONV + D_ATTN
EPS = 1e-6

KV_HEAD = QK_NOPE + V_HEAD
MXU_N = 256
LANES = 128
SUBLANES = 8
ROPE_SLAB = LANES
HEAD_SLAB = QK_NOPE + ROPE_SLAB
HALF = QK_ROPE // 2
SHIFT_LANE = HALF
EXP2_NORMAL_RANGE = 120.0

OFF_XC = 0
OFF_BC = D_CONV
OFF_CC = 2 * D_CONV
OFF_ZC = 3 * D_CONV
OFF_CQ = 4 * D_CONV
OFF_CKV = OFF_CQ + Q_LORA
OFF_KR = OFF_CKV + KV_LORA
OFF_ZA = OFF_KR + QK_ROPE
TAIL_COLS = ROPE_SLAB + D_ATTN

ADA_TN = 1024
ROPE_TT = 1024
CONV_TM = 512
MLA_TM = 512
CONV_CW = 256
HEAD_GROUPS = (4, 4)
ATTN_TQ = 1024
ATTN_TK = 512
OUT_TM = 512
VMEM_LIMIT = 56 * 1024 * 1024

_BF16 = jnp.bfloat16
_F32 = jnp.float32


def _silu(z):
    hz = 0.5 * z
    return hz + hz * jnp.tanh(hz)


def _dot(a, b):
    return jnp.dot(a, b, preferred_element_type=_F32)


def _adaln_body(ct_ref, w_ref, b_ref, o_ref):
    ct = ct_ref[...]
    sc = _silu(ct)
    w = w_ref[...]
    rows = [jnp.sum(w * sc[:, b:b + 1], axis=0, keepdims=True) for b in range(ct.shape[1])]
    o_ref[...] = jnp.concatenate(rows, axis=0) + b_ref[...]


def _adaln_mod(c, ada_w, ada_b):
    bsz = c.shape[0]
    n = ada_w.shape[1]
    return pl.pallas_call(
        _adaln_body,
        grid=(n // ADA_TN,),
        in_specs=[
            pl.BlockSpec((D_MODEL, bsz), lambda j: (0, 0)),
            pl.BlockSpec((D_MODEL, ADA_TN), lambda j: (0, j)),
            pl.BlockSpec((1, ADA_TN), lambda j: (0, j)),
        ],
        out_specs=pl.BlockSpec((bsz, ADA_TN), lambda j: (0, j)),
        out_shape=jax.ShapeDtypeStruct((bsz, n), _F32),
        name="adaln_mod",
    )(c.T, ada_w, ada_b.reshape(1, n))


def _rope_body(pos_ref, f_ref, cos_ref, sin_ref):
    ang = f_ref[...] * pos_ref[...].astype(_F32)
    c = jnp.cos(ang)
    s = jnp.sin(ang)
    z = jnp.zeros_like(c)
    cos_ref[...] = jnp.concatenate([c, z, c, z], axis=0).T
    sin_ref[...] = jnp.concatenate([-s, z, s, z], axis=0).T


def _rope_tables(positions):
    t = positions.size
    inv_freq = ROPE_BASE ** (-jnp.arange(0, QK_ROPE, 2, dtype=_F32) / QK_ROPE)
    out = pl.BlockSpec((ROPE_TT, ROPE_SLAB), lambda i: (i, 0))
    return pl.pallas_call(
        _rope_body,
        grid=(t // ROPE_TT,),
        in_specs=[pl.BlockSpec((1, ROPE_TT), lambda i: (0, i)), pl.BlockSpec((HALF, 1), lambda i: (0, 0))],
        out_specs=[out, out],
        out_shape=[jax.ShapeDtypeStruct((t, ROPE_SLAB), _F32)] * 2,
        name="rope_tables",
    )(positions.reshape(1, t), inv_freq.reshape(HALF, 1))


def _modulated_norm(x_ref, mod_ref, ng_ref):
    x = x_ref[...]
    xn = x * lax.rsqrt(jnp.mean(x * x, axis=-1, keepdims=True) + EPS)
    a = ng_ref[...] * (1.0 + mod_ref[0, 1:2, :])
    return (xn * a + mod_ref[0, 0:1, :]).astype(_BF16)


def _conv_proj_body(tiles_per_batch, x_ref, mod_ref, ng_ref, win_ref, convw_ref, yconv_ref, h_ref,
                    carry_scr, vbuf_scr):
    tm = x_ref.shape[0]

    @pl.when(pl.program_id(0) % tiles_per_batch == 0)
    def _():
        carry_scr[...] = jnp.zeros_like(carry_scr)

    h_ref[...] = _modulated_norm(x_ref, mod_ref, ng_ref)

    def proj(off, width):
        return _dot(h_ref[...], win_ref[:, off:off + width])

    for lo in range(0, D_CONV, CONV_CW):
        xc = proj(OFF_XC + lo, CONV_CW)
        bc = proj(OFF_BC + lo, CONV_CW)
        cc = proj(OFF_CC + lo, CONV_CW)
        zc = proj(OFF_ZC + lo, CONV_CW)
        v0 = cc * xc
        vbuf_scr[0:SUBLANES, :] = carry_scr[:, lo:lo + CONV_CW]
        vbuf_scr[SUBLANES:SUBLANES + tm, :] = v0
        carry_scr[:, lo:lo + CONV_CW] = v0[tm - SUBLANES:, :]
        v1 = vbuf_scr[SUBLANES - 1:SUBLANES - 1 + tm, :]
        v2 = vbuf_scr[SUBLANES - 2:SUBLANES - 2 + tm, :]
        w = convw_ref[:, lo:lo + CONV_CW]
        conv = w[0:1, :] * v2 + w[1:2, :] * v1 + w[2:3, :] * v0
        yconv_ref[:, lo:lo + CONV_CW] = (bc * conv * _silu(zc)).astype(_BF16)


def _resident(shape, index=None):
    index = (0,) * len(shape) if index is None else index
    return pl.BlockSpec(shape, lambda i: index, pipeline_mode=pl.Buffered(1))


def _conv_proj(layer, x2d, mod3, norm_g, win_all, conv_w, seq):
    t = x2d.shape[0]
    tm = CONV_TM
    tpb = seq // tm
    return pl.pallas_call(
        functools.partial(_conv_proj_body, tpb),
        grid=(t // tm,),
        in_specs=[
            pl.BlockSpec((tm, D_MODEL), lambda i: (i, 0)),
            pl.BlockSpec((1, 3, D_MODEL), lambda i: (i // tpb, 0, 0)),
            _resident((1, D_MODEL)),
            _resident((None, D_MODEL, 4 * D_CONV), (layer, 0, 0)),
            _resident((CONV_WIDTH, D_CONV)),
        ],
        out_specs=[pl.BlockSpec((tm, D_CONV), lambda i: (i, 0)), pl.BlockSpec((tm, D_MODEL), lambda i: (i, 0))],
        out_shape=[jax.ShapeDtypeStruct((t, D_CONV), _BF16), jax.ShapeDtypeStruct((t, D_MODEL), _BF16)],
        scratch_shapes=[
            pltpu.VMEM((SUBLANES, D_CONV), _F32),
            pltpu.VMEM((SUBLANES + tm, CONV_CW), _F32),
        ],
        compiler_params=pltpu.CompilerParams(
            dimension_semantics=("arbitrary",), vmem_limit_bytes=VMEM_LIMIT),
        name="conv_proj",
    )(x2d, mod3, norm_g, win_all, conv_w)


def _mla_proj_body(h_scr, wcq_ref, wckv_ref, wtail_ref, qag_ref, wqb_ref, kvag_ref, wkvb_ref,
                   gq_ref, gk_ref, qb_ref, kb_ref, cos_ref, sin_ref,
                   sz_ref, q_ref, kt_ref, v_ref, cqn_scr, ckvn_scr):
    cos_t = cos_ref[...]
    sin_t = sin_ref[...]

    def rope(slab):
        return slab * cos_t + pltpu.roll(slab, ROPE_SLAB // 2, axis=1) * sin_t

    def rms(parts, width):
        ss = sum(jnp.sum(p * p, axis=-1, keepdims=True) for p in parts)
        return lax.rsqrt(ss * (1.0 / width) + EPS)

    cq = _dot(h_scr[...], wcq_ref[...])
    cqn_scr[...] = (cq * rms([cq], Q_LORA) * qag_ref[...]).astype(_BF16)
    ckv = _dot(h_scr[...], wckv_ref[...])
    ckvn_scr[...] = (ckv * rms([ckv], KV_LORA) * kvag_ref[...]).astype(_BF16)

    kr = _dot(h_scr[...], wtail_ref[:, 0:ROPE_SLAB])
    gk = gk_ref[...]
    gq = gq_ref[...]
    kr_rot = rope(kr * gk[:, QK_NOPE:HEAD_SLAB])

    def lane_sum(sq):
        ones = jnp.ones((sq.shape[1], LANES), _BF16)
        return _dot(sq.astype(_BF16), ones)

    ss_r = lane_sum(kr * kr)
    assert sum(HEAD_GROUPS) == N_HEADS
    for g0, group in zip(itertools.accumulate((0,) + HEAD_GROUPS), HEAD_GROUPS):
        heads = range(g0, g0 + group)
        gate_cols = group * V_HEAD
        slabs = [_dot(cqn_scr[...], wqb_ref[:, hd * HEAD_SLAB:(hd + 1) * HEAD_SLAB]) for hd in heads]
        kvs = [_dot(ckvn_scr[...], wkvb_ref[:, hd * KV_HEAD:(hd + 1) * KV_HEAD]) for hd in heads]
        ssq = [lane_sum(s * s) for s in slabs]
        ssk = [lane_sum(kv[:, 0:QK_NOPE] * kv[:, 0:QK_NOPE]) for kv in kvs]
        glo = g0 * V_HEAD
        zas = [(lo, min(2 * MXU_N, glo + gate_cols - lo)) for lo in range(glo, glo + gate_cols, 2 * MXU_N)]
        zas = [(lo, w, _dot(h_scr[...], wtail_ref[:, ROPE_SLAB + lo:ROPE_SLAB + lo + w])) for lo, w in zas]
        for hd, slab, kv, sq, sk in zip(heads, slabs, kvs, ssq, ssk):
            r = lax.rsqrt(sq * (1.0 / QK_HEAD) + EPS)
            q_ref[0, hd, :, 0:QK_NOPE] = (slab[:, 0:QK_NOPE] * r * gq[:, 0:QK_NOPE]).astype(_BF16)
            q_rope = rope(slab[:, QK_NOPE:HEAD_SLAB] * r * gq[:, QK_NOPE:HEAD_SLAB])
            q_ref[0, hd, :, QK_NOPE:HEAD_SLAB] = (q_rope + qb_ref[...]).astype(_BF16)
            r = lax.rsqrt((sk + ss_r) * (1.0 / QK_HEAD) + EPS)
            kt_ref[0, hd, 0:QK_NOPE, :] = (kv[:, 0:QK_NOPE] * r * gk[:, 0:QK_NOPE]).T.astype(_BF16)
            kt_ref[0, hd, QK_NOPE:HEAD_SLAB, :] = (kr_rot * r + kb_ref[...]).T.astype(_BF16)
            v_ref[0, hd, :, :] = kv[:, QK_NOPE:KV_HEAD].astype(_BF16)
        for lo, w, za in zas:
            sz_ref[:, lo:lo + w] = _silu(za).astype(_BF16)


def _mla_proj(layer, h2d, win_all, wtail, q_a_g, wqb, kv_a_g, wkvb, gq, gk, qb, kb, cos_t, sin_t, bsz, seq):
    t = h2d.shape[0]
    tm = MLA_TM
    tpb = seq // tm
    tok = lambda width: pl.BlockSpec((tm, width), lambda i: (i, 0))
    head = lambda width: pl.BlockSpec((1, N_HEADS, tm, width), lambda i: (i // tpb, 0, i % tpb, 0))
    return pl.pallas_call(
        _mla_proj_body,
        grid=(t // tm,),
        in_specs=[
            tok(D_MODEL),
            _resident((None, D_MODEL, Q_LORA), (layer, 0, OFF_CQ // Q_LORA)),
            _resident((None, D_MODEL, KV_LORA), (layer, 0, OFF_CKV // KV_LORA)),
            _resident((D_MODEL, TAIL_COLS)),
            _resident((1, Q_LORA)),
            _resident((Q_LORA, N_HEADS * HEAD_SLAB)),
            _resident((1, KV_LORA)),
            _resident((KV_LORA, N_HEADS * (QK_NOPE + V_HEAD))),
            _resident((1, HEAD_SLAB)),
            _resident((1, HEAD_SLAB)),
            _resident((1, ROPE_SLAB)),
            _resident((1, ROPE_SLAB)),
            tok(ROPE_SLAB),
            tok(ROPE_SLAB),
        ],
        out_specs=[
            tok(D_ATTN),
            head(HEAD_SLAB),
            pl.BlockSpec((1, N_HEADS, HEAD_SLAB, tm), lambda i: (i // tpb, 0, 0, i % tpb)),
            head(V_HEAD),
        ],
        out_shape=[
            jax.ShapeDtypeStruct((t, D_ATTN), _BF16),
            jax.ShapeDtypeStruct((bsz, N_HEADS, seq, HEAD_SLAB), _BF16),
            jax.ShapeDtypeStruct((bsz, N_HEADS, HEAD_SLAB, seq), _BF16),
            jax.ShapeDtypeStruct((bsz, N_HEADS, seq, V_HEAD), _BF16),
        ],
        scratch_shapes=[
            pltpu.VMEM((tm, Q_LORA), _BF16),
            pltpu.VMEM((tm, KV_LORA), _BF16),
        ],
        compiler_params=pltpu.CompilerParams(
            dimension_semantics=("arbitrary",), vmem_limit_bytes=VMEM_LIMIT),
        name="mla_proj",
    )(h2d, win_all, win_all, wtail, q_a_g, wqb, kv_a_g, wkvb, gq, gk, qb, kb, cos_t, sin_t)


def _attn_online_body(q_ref, kt_ref, v_ref, sz_ref, o_ref, sa0_scr, sa1_scr, sb_scr, m_scr, l_scr, acc_scr):
    tq = ATTN_TQ
    tk = ATTN_TK
    half = tq // 2
    assert tq == 2 * tk
    nq = q_ref.shape[2] // tq
    assert nq % 2 == 0

    def scores(qi, j, r0, nr):
        qstart = pl.multiple_of(qi * tq + r0, half)
        kstart = pl.multiple_of(j * tk, tk)
        return _dot(q_ref[0, 0, pl.ds(qstart, nr), :], kt_ref[0, 0, :, pl.ds(kstart, tk)])

    def update(j, s, r0, nr, triangular):
        start = pl.multiple_of(j * tk, tk)
        v = v_ref[0, 0, pl.ds(start, tk), :]
        if triangular:
            row = lax.broadcasted_iota(jnp.int32, (nr, tk), 0)
            col = lax.broadcasted_iota(jnp.int32, (nr, tk), 1)
            s = jnp.where(col <= row, s, -jnp.inf)
        tiles = [s[:, t * LANES:(t + 1) * LANES] for t in range(tk // LANES)]
        mx = functools.reduce(jnp.maximum, tiles)
        m_prev = m_scr[r0:r0 + nr, :]
        m_new = jnp.maximum(m_prev, jnp.max(mx, axis=-1, keepdims=True))
        alpha = jnp.exp2(m_prev - m_new)
        ps = [jnp.exp2(t - m_new) for t in tiles]
        l_scr[r0:r0 + nr, :] = alpha * l_scr[r0:r0 + nr, :] + functools.reduce(jnp.add, ps)
        p = jnp.concatenate(ps, axis=-1).astype(_BF16)
        acc_scr[r0:r0 + nr, :] = alpha * acc_scr[r0:r0 + nr, :] + _dot(p, v)
        m_scr[r0:r0 + nr, :] = m_new

    def query_block(qi, sa_scr, sa_next_scr):
        m_scr[...] = jnp.full_like(m_scr, -jnp.inf)
        l_scr[...] = jnp.zeros_like(l_scr)
        acc_scr[...] = jnp.zeros_like(acc_scr)

        def pair(jj, carry):
            j = 2 * jj
            sb_scr[...] = scores(qi, j + 1, 0, tq)
            update(j, sa_scr[...], 0, tq, False)
            sa_scr[...] = scores(qi, j + 2, 0, tq)
            update(j + 1, sb_scr[...], 0, tq, False)
            return carry

        lax.fori_loop(0, qi, pair, 0)

        jd = 2 * qi
        sb_scr[half:tq, :] = scores(qi, jd + 1, half, half)
        update(jd, sa_scr[0:half, :], 0, half, True)
        sa_next_scr[...] = scores(jnp.minimum(qi + 1, nq - 1), 0, 0, tq)
        update(jd, sa_scr[half:tq, :], half, half, False)
        update(jd + 1, sb_scr[half:tq, :], half, half, True)

        rows = pl.ds(pl.multiple_of(qi * tq, tq), tq)
        o = acc_scr[...] / jnp.sum(l_scr[...], axis=-1, keepdims=True)
        o_ref[rows, :] = (o * sz_ref[rows, :].astype(_F32)).astype(_BF16)

    sa0_scr[...] = scores(0, 0, 0, tq)

    def two_blocks(ib, carry):
        query_block(2 * ib, sa0_scr, sa1_scr)
        query_block(2 * ib + 1, sa1_scr, sa0_scr)
        return carry

    lax.fori_loop(0, nq // 2, two_blocks, 0)


def _attn_shifted_body(q_ref, kt_ref, v_ref, sz_ref, o_ref, pa0_scr, pa1_scr, pb_scr, lnext_scr, l_scr, acc_scr):
    tq = ATTN_TQ
    tk = ATTN_TK
    half = tq // 2
    assert tq == 2 * tk
    nq = q_ref.shape[2] // tq
    assert nq % 2 == 0
    col_minus_row = (lax.broadcasted_iota(jnp.int32, (tq, tk), 1)
                     - lax.broadcasted_iota(jnp.int32, (tq, tk), 0))

    def probs(qi, j, r0, nr, masked):
        qstart = pl.multiple_of(qi * tq + r0, half)
        kstart = pl.multiple_of(j * tk, tk)
        s = _dot(q_ref[0, 0, pl.ds(qstart, nr), :], kt_ref[0, 0, :, pl.ds(kstart, tk)])
        if masked:
            s = jnp.where(col_minus_row[r0:r0 + nr, :] <= qi * tq - j * tk, s, -jnp.inf)
        ps = [jnp.exp2(s[:, t * LANES:(t + 1) * LANES]) for t in range(tk // LANES)]
        return jnp.concatenate(ps, axis=-1).astype(_BF16), functools.reduce(jnp.add, ps)

    def values(j):
        return v_ref[0, 0, pl.ds(pl.multiple_of(j * tk, tk), tk), :]

    def query_block(qi, pa_scr, pa_next_scr):
        l_scr[...] = lnext_scr[...]
        acc_scr[...] = jnp.zeros_like(acc_scr)

        def pair(jj, carry):
            j = 2 * jj
            p, ls = probs(qi, j + 1, 0, tq, False)
            pb_scr[...] = p
            l_scr[...] += ls
            acc_scr[...] += _dot(pa_scr[...], values(j))
            p, ls = probs(qi, j + 2, 0, tq, True)
            pa_scr[...] = p
            l_scr[...] += ls
            acc_scr[...] += _dot(pb_scr[...], values(j + 1))
            return carry

        lax.fori_loop(0, qi, pair, 0)

        jd = 2 * qi
        p, ls = probs(qi, jd + 1, half, half, True)
        pb_scr[half:tq, :] = p
        l_scr[half:tq, :] += ls
        acc_scr[...] += _dot(pa_scr[...], values(jd))
        nxt = jnp.minimum(qi + 1, nq - 1)
        p, ls = probs(nxt, 0, 0, tq, True)
        pa_next_scr[...] = p
        lnext_scr[...] = ls
        acc_scr[half:tq, :] += _dot(pb_scr[half:tq, :], values(jd + 1))

        rows = pl.ds(pl.multiple_of(qi * tq, tq), tq)
        o = acc_scr[...] / jnp.sum(l_scr[...], axis=-1, keepdims=True)
        o_ref[rows, :] = (o * sz_ref[rows, :].astype(_F32)).astype(_BF16)

    p, ls = probs(0, 0, 0, tq, True)
    pa0_scr[...] = p
    lnext_scr[...] = ls

    def two_blocks(ib, carry):
        query_block(2 * ib, pa0_scr, pa1_scr)
        query_block(2 * ib + 1, pa1_scr, pa0_scr)
        return carry

    lax.fori_loop(0, nq // 2, two_blocks, 0)


def _attention(online_max, q, k, v, sz):
    bsz, _, seq, _ = q.shape
    tq = ATTN_TQ
    per_head = lambda width: pl.BlockSpec((1, 1, seq, width), lambda b, h: (b, h, 0, 0))
    stats = [pltpu.VMEM((tq, LANES), _F32), pltpu.VMEM((tq, LANES), _F32), pltpu.VMEM((tq, V_HEAD), _F32)]
    if online_max:
        body = _attn_online_body
        scratch = [pltpu.VMEM((tq, ATTN_TK), _F32)] * 3 + stats
    else:
        body = _attn_shifted_body
        scratch = [pltpu.VMEM((tq, ATTN_TK), _BF16)] * 3 + stats
    return pl.pallas_call(
        body,
        grid=(bsz, N_HEADS),
        in_specs=[
            per_head(HEAD_SLAB),
            pl.BlockSpec((1, 1, HEAD_SLAB, seq), lambda b, h: (b, h, 0, 0)),
            per_head(V_HEAD),
            pl.BlockSpec((seq, V_HEAD), lambda b, h: (b, h)),
        ],
        out_specs=pl.BlockSpec((seq, V_HEAD), lambda b, h: (b, h)),
        out_shape=jax.ShapeDtypeStruct((bsz * seq, D_ATTN), _BF16),
        scratch_shapes=scratch,
        compiler_params=pltpu.CompilerParams(
            dimension_semantics=("arbitrary", "arbitrary"), vmem_limit_bytes=VMEM_LIMIT),
        name="attention_online_max" if online_max else "attention_shifted",
    )(q, k, v, sz)


def _out_proj_body(yc_ref, ya_ref, x_ref, mod_ref, w_ref, o_ref):
    y = _dot(yc_ref[...], w_ref[0:D_CONV, :]) + _dot(ya_ref[...], w_ref[D_CONV:D_MIX, :])
    o_ref[...] = x_ref[...] + mod_ref[0, 2:3, :] * y


def _out_proj(yconv, yattn, x2d, mod3, wout, seq):
    t = x2d.shape[0]
    tm = OUT_TM
    tpb = seq // tm
    return pl.pallas_call(
        _out_proj_body,
        grid=(t // tm,),
        in_specs=[
            pl.BlockSpec((tm, D_CONV), lambda i: (i, 0)),
            pl.BlockSpec((tm, D_ATTN), lambda i: (i, 0)),
            pl.BlockSpec((tm, D_MODEL), lambda i: (i, 0)),
            pl.BlockSpec((1, 3, D_MODEL), lambda i: (i // tpb, 0, 0)),
            _resident((D_MIX, D_MODEL)),
        ],
        out_specs=pl.BlockSpec((tm, D_MODEL), lambda i: (i, 0)),
        out_shape=jax.ShapeDtypeStruct((t, D_MODEL), _F32),
        compiler_params=pltpu.CompilerParams(
            dimension_semantics=("arbitrary",), vmem_limit_bytes=VMEM_LIMIT),
        name="out_proj",
    )(yconv, yattn, x2d, mod3, wout)


def _score_shift(q_g, k_g, qscale):
    bound = QK_HEAD * qscale * jnp.max(jnp.abs(q_g)) * jnp.max(jnp.abs(k_g))
    shift = (bound * 1.03 + 1.0).astype(_BF16).astype(_F32)
    use_shift = bound + shift <= EXP2_NORMAL_RANGE
    return jnp.where(use_shift, shift, 0.0), use_shift


def _rope_slab_cols(w):
    z = jnp.zeros(w.shape[:-1] + (HALF,), w.dtype)
    return jnp.concatenate([w[..., :HALF], z, w[..., HALF:], z], axis=-1)


def _layer(layer, x2d, mod, cos_t, sin_t, norm_g, w_in, win_all, conv_w, q_a_g, w_q_b, kv_a_g, w_kv_b, q_g, k_g,
           w_out, bsz, seq):
    wtail = jnp.concatenate([_rope_slab_cols(w_in[:, OFF_KR:OFF_ZA]), w_in[:, OFF_ZA:]], axis=-1).astype(_BF16)
    wq = w_q_b.reshape(Q_LORA, N_HEADS, QK_HEAD)
    wqb = jnp.concatenate([wq[..., :QK_NOPE], _rope_slab_cols(wq[..., QK_NOPE:])], axis=-1)
    wqb = wqb.reshape(Q_LORA, N_HEADS * HEAD_SLAB).astype(_BF16)
    wkvb = w_kv_b.astype(_BF16)
    qscale = math.log2(math.e) / math.sqrt(QK_HEAD)
    gq = jnp.concatenate([q_g[:QK_NOPE], _rope_slab_cols(q_g[QK_NOPE:])]).reshape(1, HEAD_SLAB) * qscale
    gk = jnp.concatenate([k_g[:QK_NOPE], _rope_slab_cols(k_g[QK_NOPE:])]).reshape(1, HEAD_SLAB)
    mod3 = mod.reshape(bsz, 3, D_MODEL)
    shift, use_shift = _score_shift(q_g, k_g, qscale)
    pad_lane = jnp.arange(ROPE_SLAB) == SHIFT_LANE
    qb = jnp.where(pad_lane, -shift, 0.0).reshape(1, ROPE_SLAB)
    kb = jnp.where(pad_lane, 1.0, 0.0).reshape(1, ROPE_SLAB)

    ng = norm_g.reshape(1, D_MODEL)
    yconv, h2d = _conv_proj(layer, x2d, mod3, ng, win_all, conv_w, seq)
    sz, q, k, v = _mla_proj(
        layer, h2d, win_all, wtail, q_a_g.reshape(1, Q_LORA), wqb,
        kv_a_g.reshape(1, KV_LORA), wkvb, gq, gk, qb, kb, cos_t, sin_t, bsz, seq)
    yattn = lax.cond(use_shift, functools.partial(_attention, False), functools.partial(_attention, True),
                     q, k, v, sz)
    return _out_proj(yconv, yattn, x2d, mod3, w_out.astype(_BF16), seq)


def kernel(x, c, positions, ada_w, ada_b, norm_g, w_in, conv_w, q_a_g, w_q_b, kv_a_g, w_kv_b, q_g, k_g, w_out):
    bsz, seq, _ = x.shape
    depth = ada_w.shape[0]
    cos_t, sin_t = _rope_tables(positions)
    x2d = x.reshape(bsz * seq, D_MODEL)
    win_all = w_in.astype(_BF16)
    for l in range(depth):
        mod = _adaln_mod(c, ada_w[l], ada_b[l])
        x2d = _layer(l, x2d, mod, cos_t, sin_t, norm_g[l], w_in[l], win_all, conv_w[l], q_a_g[l], w_q_b[l],
                     kv_a_g[l], w_kv_b[l], q_g[l], k_g[l], w_out[l], bsz, seq)
    return x2d.reshape(bsz, seq, D_MODEL)
```

```python
import functools
import itertools
import math

import jax
import jax.numpy as jnp
from jax import lax
from jax.experimental import pallas as pl
from jax.experimental.pallas import tpu as pltpu

D_MODEL = 2048
D_CONV = 1024
CONV_WIDTH = 3
N_HEADS = 8
QK_NOPE = 128
QK_ROPE = 64
QK_HEAD = QK_NOPE + QK_ROPE
V_HEAD = 128
D_ATTN = N_HEADS * V_HEAD
Q_LORA = 512
KV_LORA = 256
ROPE_BASE = 10000.0
D_MIX = D_CONV + D_ATTN
EPS = 1e-6

KV_HEAD = QK_NOPE + V_HEAD
MXU_N = 256
LANES = 128
SUBLANES = 8
ROPE_SLAB = LANES
HEAD_SLAB = QK_NOPE + ROPE_SLAB
HALF = QK_ROPE // 2
SHIFT_LANE = HALF
EXP2_NORMAL_RANGE = 120.0

OFF_XC = 0
OFF_BC = D_CONV
OFF_CC = 2 * D_CONV
OFF_ZC = 3 * D_CONV
OFF_CQ = 4 * D_CONV
OFF_CKV = OFF_CQ + Q_LORA
OFF_KR = OFF_CKV + KV_LORA
OFF_ZA = OFF_KR + QK_ROPE
TAIL_COLS = ROPE_SLAB + D_ATTN

ADA_TN = 1024
ROPE_TT = 1024
CONV_TM = 512
MLA_TM = 512
CONV_CW = 256
HEAD_GROUPS = (4, 4)
ATTN_TQ = 1024
ATTN_TK = 512
OUT_TM = 512
VMEM_LIMIT = 56 * 1024 * 1024

_BF16 = jnp.bfloat16
_F32 = jnp.float32


def _silu(z):
    hz = 0.5 * z
    return hz + hz * jnp.tanh(hz)


def _dot(a, b):
    return jnp.dot(a, b, preferred_element_type=_F32)


def _adaln_body(ct_ref, w_ref, b_ref, o_ref):
    ct = ct_ref[...]
    sc = _silu(ct)
    w = w_ref[...]
    rows = [jnp.sum(w * sc[:, b:b + 1], axis=0, keepdims=True) for b in range(ct.shape[1])]
    o_ref[...] = jnp.concatenate(rows, axis=0) + b_ref[...]


def _adaln_mod(c, ada_w, ada_b):
    bsz = c.shape[0]
    n = ada_w.shape[1]
    return pl.pallas_call(
        _adaln_body,
        grid=(n // ADA_TN,),
        in_specs=[
            pl.BlockSpec((D_MODEL, bsz), lambda j: (0, 0)),
            pl.BlockSpec((D_MODEL, ADA_TN), lambda j: (0, j)),
            pl.BlockSpec((1, ADA_TN), lambda j: (0, j)),
        ],
        out_specs=pl.BlockSpec((bsz, ADA_TN), lambda j: (0, j)),
        out_shape=jax.ShapeDtypeStruct((bsz, n), _F32),
        name="adaln_mod",
    )(c.T, ada_w, ada_b.reshape(1, n))


def _rope_body(pos_ref, f_ref, cos_ref, sin_ref):
    ang = f_ref[...] * pos_ref[...].astype(_F32)
    c = jnp.cos(ang)
    s = jnp.sin(ang)
    z = jnp.zeros_like(c)
    cos_ref[...] = jnp.concatenate([c, z, c, z], axis=0).T
    sin_ref[...] = jnp.concatenate([-s, z, s, z], axis=0).T


def _rope_tables(positions):
    t = positions.size
    inv_freq = ROPE_BASE ** (-jnp.arange(0, QK_ROPE, 2, dtype=_F32) / QK_ROPE)
    out = pl.BlockSpec((ROPE_TT, ROPE_SLAB), lambda i: (i, 0))
    return pl.pallas_call(
        _rope_body,
        grid=(t // ROPE_TT,),
        in_specs=[pl.BlockSpec((1, ROPE_TT), lambda i: (0, i)), pl.BlockSpec((HALF, 1), lambda i: (0, 0))],
        out_specs=[out, out],
        out_shape=[jax.ShapeDtypeStruct((t, ROPE_SLAB), _F32)] * 2,
        name="rope_tables",
    )(positions.reshape(1, t), inv_freq.reshape(HALF, 1))


def _modulated_norm(x_ref, mod_ref, ng_ref):
    x = x_ref[...]
    xn = x * lax.rsqrt(jnp.mean(x * x, axis=-1, keepdims=True) + EPS)
    a = ng_ref[...] * (1.0 + mod_ref[0, 1:2, :])
    return (xn * a + mod_ref[0, 0:1, :]).astype(_BF16)


def _conv_proj_body(tiles_per_batch, h_ref, win_ref, convw_ref, yconv_ref, carry_scr, vbuf_scr):
    tm = h_ref.shape[0]

    @pl.when(pl.program_id(0) % tiles_per_batch == 0)
    def _():
        carry_scr[...] = jnp.zeros_like(carry_scr)

    def proj(off, width):
        return _dot(h_ref[...], win_ref[:, off:off + width])

    for lo in range(0, D_CONV, CONV_CW):
        xc = proj(OFF_XC + lo, CONV_CW)
        bc = proj(OFF_BC + lo, CONV_CW)
        cc = proj(OFF_CC + lo, CONV_CW)
        zc = proj(OFF_ZC + lo, CONV_CW)
        v0 = cc * xc
        vbuf_scr[0:SUBLANES, :] = carry_scr[:, lo:lo + CONV_CW]
        vbuf_scr[SUBLANES:SUBLANES + tm, :] = v0
        carry_scr[:, lo:lo + CONV_CW] = v0[tm - SUBLANES:, :]
        v1 = vbuf_scr[SUBLANES - 1:SUBLANES - 1 + tm, :]
        v2 = vbuf_scr[SUBLANES - 2:SUBLANES - 2 + tm, :]
        w = convw_ref[:, lo:lo + CONV_CW]
        conv = w[0:1, :] * v2 + w[1:2, :] * v1 + w[2:3, :] * v0
        yconv_ref[:, lo:lo + CONV_CW] = (bc * conv * _silu(zc)).astype(_BF16)


def _resident(shape, index=None):
    index = (0,) * len(shape) if index is None else index
    return pl.BlockSpec(shape, lambda i: index, pipeline_mode=pl.Buffered(1))


def _conv_proj(layer, h2d, win_all, conv_w, seq):
    t = h2d.shape[0]
    tm = CONV_TM
    tpb = seq // tm
    return pl.pallas_call(
        functools.partial(_conv_proj_body, tpb),
        grid=(t // tm,),
        in_specs=[
            pl.BlockSpec((tm, D_MODEL), lambda i: (i, 0)),
            _resident((None, D_MODEL, 4 * D_CONV), (layer, 0, 0)),
            _resident((CONV_WIDTH, D_CONV)),
        ],
        out_specs=pl.BlockSpec((tm, D_CONV), lambda i: (i, 0)),
        out_shape=jax.ShapeDtypeStruct((t, D_CONV), _BF16),
        scratch_shapes=[
            pltpu.VMEM((SUBLANES, D_CONV), _F32),
            pltpu.VMEM((SUBLANES + tm, CONV_CW), _F32),
        ],
        compiler_params=pltpu.CompilerParams(
            dimension_semantics=("arbitrary",), vmem_limit_bytes=VMEM_LIMIT),
        name="conv_proj",
    )(h2d, win_all, conv_w)


def _mla_proj_body(x_ref, mod_ref, ng_ref, wcq_ref, wckv_ref, wtail_ref, qag_ref, wqb_ref, kvag_ref, wkvb_ref,
                   gq_ref, gk_ref, qb_ref, kb_ref, cos_ref, sin_ref,
                   h_scr, sz_ref, q_ref, kt_ref, v_ref, cqn_scr, ckvn_scr):
    h_scr[...] = _modulated_norm(x_ref, mod_ref, ng_ref)
    cos_t = cos_ref[...]
    sin_t = sin_ref[...]

    def rope(slab):
        return slab * cos_t + pltpu.roll(slab, ROPE_SLAB // 2, axis=1) * sin_t

    def rms(parts, width):
        ss = sum(jnp.sum(p * p, axis=-1, keepdims=True) for p in parts)
        return lax.rsqrt(ss * (1.0 / width) + EPS)

    cq = _dot(h_scr[...], wcq_ref[...])
    cqn_scr[...] = (cq * rms([cq], Q_LORA) * qag_ref[...]).astype(_BF16)
    ckv = _dot(h_scr[...], wckv_ref[...])
    ckvn_scr[...] = (ckv * rms([ckv], KV_LORA) * kvag_ref[...]).astype(_BF16)

    kr = _dot(h_scr[...], wtail_ref[:, 0:ROPE_SLAB])
    gk = gk_ref[...]
    gq = gq_ref[...]
    kr_rot = rope(kr * gk[:, QK_NOPE:HEAD_SLAB])

    def lane_sum(sq):
        ones = jnp.ones((sq.shape[1], LANES), _BF16)
        return _dot(sq.astype(_BF16), ones)

    ss_r = lane_sum(kr * kr)
    assert sum(HEAD_GROUPS) == N_HEADS
    for g0, group in zip(itertools.accumulate((0,) + HEAD_GROUPS), HEAD_GROUPS):
        heads = range(g0, g0 + group)
        gate_cols = group * V_HEAD
        slabs = [_dot(cqn_scr[...], wqb_ref[:, hd * HEAD_SLAB:(hd + 1) * HEAD_SLAB]) for hd in heads]
        kvs = [_dot(ckvn_scr[...], wkvb_ref[:, hd * KV_HEAD:(hd + 1) * KV_HEAD]) for hd in heads]
        ssq = [lane_sum(s * s) for s in slabs]
        ssk = [lane_sum(kv[:, 0:QK_NOPE] * kv[:, 0:QK_NOPE]) for kv in kvs]
        glo = g0 * V_HEAD
        zas = [(lo, min(2 * MXU_N, glo + gate_cols - lo)) for lo in range(glo, glo + gate_cols, 2 * MXU_N)]
        zas = [(lo, w, _dot(h_scr[...], wtail_ref[:, ROPE_SLAB + lo:ROPE_SLAB + lo + w])) for lo, w in zas]
        for hd, slab, kv, sq, sk in zip(heads, slabs, kvs, ssq, ssk):
            r = lax.rsqrt(sq * (1.0 / QK_HEAD) + EPS)
            q_ref[0, hd, :, 0:QK_NOPE] = (slab[:, 0:QK_NOPE] * r * gq[:, 0:QK_NOPE]).astype(_BF16)
            q_rope = rope(slab[:, QK_NOPE:HEAD_SLAB] * r * gq[:, QK_NOPE:HEAD_SLAB])
            q_ref[0, hd, :, QK_NOPE:HEAD_SLAB] = (q_rope + qb_ref[...]).astype(_BF16)
            r = lax.rsqrt((sk + ss_r) * (1.0 / QK_HEAD) + EPS)
            kt_ref[0, hd, 0:QK_NOPE, :] = (kv[:, 0:QK_NOPE] * r * gk[:, 0:QK_NOPE]).T.astype(_BF16)
            kt_ref[0, hd, QK_NOPE:HEAD_SLAB, :] = (kr_rot * r + kb_ref[...]).T.astype(_BF16)
            v_ref[0, hd, :, :] = kv[:, QK_NOPE:KV_HEAD].astype(_BF16)
        for lo, w, za in zas:
            sz_ref[:, lo:lo + w] = _silu(za).astype(_BF16)


def _mla_proj(layer, x2d, mod3, norm_g, win_all, wtail, q_a_g, wqb, kv_a_g, wkvb, gq, gk, qb, kb, cos_t, sin_t,
              bsz, seq):
    t = x2d.shape[0]
    tm = MLA_TM
    tpb = seq // tm
    tok = lambda width: pl.BlockSpec((tm, width), lambda i: (i, 0))
    head = lambda width: pl.BlockSpec((1, N_HEADS, tm, width), lambda i: (i // tpb, 0, i % tpb, 0))
    return pl.pallas_call(
        _mla_proj_body,
        grid=(t // tm,),
        in_specs=[
            tok(D_MODEL),
            pl.BlockSpec((1, 3, D_MODEL), lambda i: (i // tpb, 0, 0)),
            _resident((1, D_MODEL)),
            _resident((None, D_MODEL, Q_LORA), (layer, 0, OFF_CQ // Q_LORA)),
            _resident((None, D_MODEL, KV_LORA), (layer, 0, OFF_CKV // KV_LORA)),
            _resident((D_MODEL, TAIL_COLS)),
            _resident((1, Q_LORA)),
            _resident((Q_LORA, N_HEADS * HEAD_SLAB)),
            _resident((1, KV_LORA)),
            _resident((KV_LORA, N_HEADS * (QK_NOPE + V_HEAD))),
            _resident((1, HEAD_SLAB)),
            _resident((1, HEAD_SLAB)),
            _resident((1, ROPE_SLAB)),
            _resident((1, ROPE_SLAB)),
            tok(ROPE_SLAB),
            tok(ROPE_SLAB),
        ],
        out_specs=[
            tok(D_MODEL),
            tok(D_ATTN),
            head(HEAD_SLAB),
            pl.BlockSpec((1, N_HEADS, HEAD_SLAB, tm), lambda i: (i // tpb, 0, 0, i % tpb)),
            head(V_HEAD),
        ],
        out_shape=[
            jax.ShapeDtypeStruct((t, D_MODEL), _BF16),
            jax.ShapeDtypeStruct((t, D_ATTN), _BF16),
            jax.ShapeDtypeStruct((bsz, N_HEADS, seq, HEAD_SLAB), _BF16),
            jax.ShapeDtypeStruct((bsz, N_HEADS, HEAD_SLAB, seq), _BF16),
            jax.ShapeDtypeStruct((bsz, N_HEADS, seq, V_HEAD), _BF16),
        ],
        scratch_shapes=[
            pltpu.VMEM((tm, Q_LORA), _BF16),
            pltpu.VMEM((tm, KV_LORA), _BF16),
        ],
        compiler_params=pltpu.CompilerParams(
            dimension_semantics=("arbitrary",), vmem_limit_bytes=VMEM_LIMIT),
        name="mla_proj",
    )(x2d, mod3, norm_g, win_all, win_all, wtail, q_a_g, wqb, kv_a_g, wkvb, gq, gk, qb, kb, cos_t, sin_t)


def _attn_online_body(q_ref, kt_ref, v_ref, sz_ref, o_ref, sa0_scr, sa1_scr, sb_scr, m_scr, l_scr, acc_scr):
    tq = ATTN_TQ
    tk = ATTN_TK
    half = tq // 2
    assert tq == 2 * tk
    nq = q_ref.shape[2] // tq
    assert nq % 2 == 0

    def scores(qi, j, r0, nr):
        qstart = pl.multiple_of(qi * tq + r0, half)
        kstart = pl.multiple_of(j * tk, tk)
        return _dot(q_ref[0, 0, pl.ds(qstart, nr), :], kt_ref[0, 0, :, pl.ds(kstart, tk)])

    def update(j, s, r0, nr, triangular):
        start = pl.multiple_of(j * tk, tk)
        v = v_ref[0, 0, pl.ds(start, tk), :]
        if triangular:
            row = lax.broadcasted_iota(jnp.int32, (nr, tk), 0)
            col = lax.broadcasted_iota(jnp.int32, (nr, tk), 1)
            s = jnp.where(col <= row, s, -jnp.inf)
        tiles = [s[:, t * LANES:(t + 1) * LANES] for t in range(tk // LANES)]
        mx = functools.reduce(jnp.maximum, tiles)
        m_prev = m_scr[r0:r0 + nr, :]
        m_new = jnp.maximum(m_prev, jnp.max(mx, axis=-1, keepdims=True))
        alpha = jnp.exp2(m_prev - m_new)
        ps = [jnp.exp2(t - m_new) for t in tiles]
        l_scr[r0:r0 + nr, :] = alpha * l_scr[r0:r0 + nr, :] + functools.reduce(jnp.add, ps)
        p = jnp.concatenate(ps, axis=-1).astype(_BF16)
        acc_scr[r0:r0 + nr, :] = alpha * acc_scr[r0:r0 + nr, :] + _dot(p, v)
        m_scr[r0:r0 + nr, :] = m_new

    def query_block(qi, sa_scr, sa_next_scr):
        m_scr[...] = jnp.full_like(m_scr, -jnp.inf)
        l_scr[...] = jnp.zeros_like(l_scr)
        acc_scr[...] = jnp.zeros_like(acc_scr)

        def pair(jj, carry):
            j = 2 * jj
            sb_scr[...] = scores(qi, j + 1, 0, tq)
            update(j, sa_scr[...], 0, tq, False)
            sa_scr[...] = scores(qi, j + 2, 0, tq)
            update(j + 1, sb_scr[...], 0, tq, False)
            return carry

        lax.fori_loop(0, qi, pair, 0)

        jd = 2 * qi
        sb_scr[half:tq, :] = scores(qi, jd + 1, half, half)
        update(jd, sa_scr[0:half, :], 0, half, True)
        sa_next_scr[...] = scores(jnp.minimum(qi + 1, nq - 1), 0, 0, tq)
        update(jd, sa_scr[half:tq, :], half, half, False)
        update(jd + 1, sb_scr[half:tq, :], half, half, True)

        rows = pl.ds(pl.multiple_of(qi * tq, tq), tq)
        o = acc_scr[...] / jnp.sum(l_scr[...], axis=-1, keepdims=True)
        o_ref[rows, :] = (o * sz_ref[rows, :].astype(_F32)).astype(_BF16)

    sa0_scr[...] = scores(0, 0, 0, tq)

    def two_blocks(ib, carry):
        query_block(2 * ib, sa0_scr, sa1_scr)
        query_block(2 * ib + 1, sa1_scr, sa0_scr)
        return carry

    lax.fori_loop(0, nq // 2, two_blocks, 0)


def _attn_shifted_body(q_ref, kt_ref, v_ref, sz_ref, o_ref, pa0_scr, pa1_scr, pb_scr, lnext_scr, l_scr, acc_scr):
    tq = ATTN_TQ
    tk = ATTN_TK
    half = tq // 2
    assert tq == 2 * tk
    nq = q_ref.shape[2] // tq
    assert nq % 2 == 0
    col_minus_row = (lax.broadcasted_iota(jnp.int32, (tq, tk), 1)
                     - lax.broadcasted_iota(jnp.int32, (tq, tk), 0))

    def probs(qi, j, r0, nr, masked):
        qstart = pl.multiple_of(qi * tq + r0, half)
        kstart = pl.multiple_of(j * tk, tk)
        s = _dot(q_ref[0, 0, pl.ds(qstart, nr), :], kt_ref[0, 0, :, pl.ds(kstart, tk)])
        if masked:
            s = jnp.where(col_minus_row[r0:r0 + nr, :] <= qi * tq - j * tk, s, -jnp.inf)
        ps = [jnp.exp2(s[:, t * LANES:(t + 1) * LANES]) for t in range(tk // LANES)]
        return jnp.concatenate(ps, axis=-1).astype(_BF16), functools.reduce(jnp.add, ps)

    def values(j):
        return v_ref[0, 0, pl.ds(pl.multiple_of(j * tk, tk), tk), :]

    def query_block(qi, pa_scr, pa_next_scr):
        l_scr[...] = lnext_scr[...]
        acc_scr[...] = jnp.zeros_like(acc_scr)

        def pair(jj, carry):
            j = 2 * jj
            p, ls = probs(qi, j + 1, 0, tq, False)
            pb_scr[...] = p
            l_scr[...] += ls
            acc_scr[...] += _dot(pa_scr[...], values(j))
            p, ls = probs(qi, j + 2, 0, tq, True)
            pa_scr[...] = p
            l_scr[...] += ls
            acc_scr[...] += _dot(pb_scr[...], values(j + 1))
            return carry

        lax.fori_loop(0, qi, pair, 0)

        jd = 2 * qi
        p, ls = probs(qi, jd + 1, half, half, True)
        pb_scr[half:tq, :] = p
        l_scr[half:tq, :] += ls
        acc_scr[...] += _dot(pa_scr[...], values(jd))
        nxt = jnp.minimum(qi + 1, nq - 1)
        p, ls = probs(nxt, 0, 0, tq, True)
        pa_next_scr[...] = p
        lnext_scr[...] = ls
        acc_scr[half:tq, :] += _dot(pb_scr[half:tq, :], values(jd + 1))

        rows = pl.ds(pl.multiple_of(qi * tq, tq), tq)
        o = acc_scr[...] / jnp.sum(l_scr[...], axis=-1, keepdims=True)
        o_ref[rows, :] = (o * sz_ref[rows, :].astype(_F32)).astype(_BF16)

    p, ls = probs(0, 0, 0, tq, True)
    pa0_scr[...] = p
    lnext_scr[...] = ls

    def two_blocks(ib, carry):
        query_block(2 * ib, pa0_scr, pa1_scr)
        query_block(2 * ib + 1, pa1_scr, pa0_scr)
        return carry

    lax.fori_loop(0, nq // 2, two_blocks, 0)


def _attention(online_max, q, k, v, sz):
    bsz, _, seq, _ = q.shape
    tq = ATTN_TQ
    per_head = lambda width: pl.BlockSpec((1, 1, seq, width), lambda b, h: (b, h, 0, 0))
    stats = [pltpu.VMEM((tq, LANES), _F32), pltpu.VMEM((tq, LANES), _F32), pltpu.VMEM((tq, V_HEAD), _F32)]
    if online_max:
        body = _attn_online_body
        scratch = [pltpu.VMEM((tq, ATTN_TK), _F32)] * 3 + stats
    else:
        body = _attn_shifted_body
        scratch = [pltpu.VMEM((tq, ATTN_TK), _BF16)] * 3 + stats
    return pl.pallas_call(
        body,
        grid=(bsz, N_HEADS),
        in_specs=[
            per_head(HEAD_SLAB),
            pl.BlockSpec((1, 1, HEAD_SLAB, seq), lambda b, h: (b, h, 0, 0)),
            per_head(V_HEAD),
            pl.BlockSpec((seq, V_HEAD), lambda b, h: (b, h)),
        ],
        out_specs=pl.BlockSpec((seq, V_HEAD), lambda b, h: (b, h)),
        out_shape=jax.ShapeDtypeStruct((bsz * seq, D_ATTN), _BF16),
        scratch_shapes=scratch,
        compiler_params=pltpu.CompilerParams(
            dimension_semantics=("arbitrary", "arbitrary"), vmem_limit_bytes=VMEM_LIMIT),
        name="attention_online_max" if online_max else "attention_shifted",
    )(q, k, v, sz)


def _out_proj_body(yc_ref, ya_ref, x_ref, mod_ref, w32_ref, o_ref, w_scr):
    @pl.when(pl.program_id(0) == 0)
    def _():
        w_scr[...] = w32_ref[...].astype(_BF16)

    y = _dot(yc_ref[...], w_scr[0:D_CONV, :]) + _dot(ya_ref[...], w_scr[D_CONV:D_MIX, :])
    o_ref[...] = x_ref[...] + mod_ref[0, 2:3, :] * y


def _out_proj(yconv, yattn, x2d, mod3, wout, seq):
    t = x2d.shape[0]
    tm = OUT_TM
    tpb = seq // tm
    return pl.pallas_call(
        _out_proj_body,
        grid=(t // tm,),
        in_specs=[
            pl.BlockSpec((tm, D_CONV), lambda i: (i, 0)),
            pl.BlockSpec((tm, D_ATTN), lambda i: (i, 0)),
            pl.BlockSpec((tm, D_MODEL), lambda i: (i, 0)),
            pl.BlockSpec((1, 3, D_MODEL), lambda i: (i // tpb, 0, 0)),
            _resident((D_MIX, D_MODEL)),
        ],
        out_specs=pl.BlockSpec((tm, D_MODEL), lambda i: (i, 0)),
        out_shape=jax.ShapeDtypeStruct((t, D_MODEL), _F32),
        scratch_shapes=[pltpu.VMEM((D_MIX, D_MODEL), _BF16)],
        compiler_params=pltpu.CompilerParams(
            dimension_semantics=("arbitrary",), vmem_limit_bytes=VMEM_LIMIT),
        name="out_proj",
    )(yconv, yattn, x2d, mod3, wout)


def _score_shift(q_g, k_g, qscale):
    bound = QK_HEAD * qscale * jnp.max(jnp.abs(q_g)) * jnp.max(jnp.abs(k_g))
    shift = (bound * 1.03 + 1.0).astype(_BF16).astype(_F32)
    use_shift = bound + shift <= EXP2_NORMAL_RANGE
    return jnp.where(use_shift, shift, 0.0), use_shift


def _rope_slab_cols(w):
    z = jnp.zeros(w.shape[:-1] + (HALF,), w.dtype)
    return jnp.concatenate([w[..., :HALF], z, w[..., HALF:], z], axis=-1)


def _layer(layer, x2d, mod, cos_t, sin_t, norm_g, w_in, win_all, conv_w, q_a_g, w_q_b, kv_a_g, w_kv_b, q_g, k_g,
           w_out, bsz, seq):
    wtail = jnp.concatenate([_rope_slab_cols(w_in[:, OFF_KR:OFF_ZA]), w_in[:, OFF_ZA:]], axis=-1).astype(_BF16)
    wq = w_q_b.reshape(Q_LORA, N_HEADS, QK_HEAD)
    wqb = jnp.concatenate([wq[..., :QK_NOPE], _rope_slab_cols(wq[..., QK_NOPE:])], axis=-1)
    wqb = wqb.reshape(Q_LORA, N_HEADS * HEAD_SLAB).astype(_BF16)
    wkvb = w_kv_b.astype(_BF16)
    qscale = math.log2(math.e) / math.sqrt(QK_HEAD)
    gq = jnp.concatenate([q_g[:QK_NOPE], _rope_slab_cols(q_g[QK_NOPE:])]).reshape(1, HEAD_SLAB) * qscale
    gk = jnp.concatenate([k_g[:QK_NOPE], _rope_slab_cols(k_g[QK_NOPE:])]).reshape(1, HEAD_SLAB)
    mod3 = mod.reshape(bsz, 3, D_MODEL)
    shift, use_shift = _score_shift(q_g, k_g, qscale)
    pad_lane = jnp.arange(ROPE_SLAB) == SHIFT_LANE
    qb = jnp.where(pad_lane, -shift, 0.0).reshape(1, ROPE_SLAB)
    kb = jnp.where(pad_lane, 1.0, 0.0).reshape(1, ROPE_SLAB)

    ng = norm_g.reshape(1, D_MODEL)
    h2d, sz, q, k, v = _mla_proj(
        layer, x2d, mod3, ng, win_all, wtail, q_a_g.reshape(1, Q_LORA), wqb,
        kv_a_g.reshape(1, KV_LORA), wkvb, gq, gk, qb, kb, cos_t, sin_t, bsz, seq)
    yconv = _conv_proj(layer, h2d, win_all, conv_w, seq)
    yattn = lax.cond(use_shift, functools.partial(_attention, False), functools.partial(_attention, True),
                     q, k, v, sz)
    return _out_proj(yconv, yattn, x2d, mod3, w_out, seq)


def kernel(x, c, positions, ada_w, ada_b, norm_g, w_in, conv_w, q_a_g, w_q_b, kv_a_g, w_kv_b, q_g, k_g, w_out):
    bsz, seq, _ = x.shape
    depth = ada_w.shape[0]
    cos_t, sin_t = _rope_tables(positions)
    x2d = x.reshape(bsz * seq, D_MODEL)
    win_all = w_in.astype(_BF16)
    for l in range(depth):
        mod = _adaln_mod(c, ada_w[l], ada_b[l])
        x2d = _layer(l, x2d, mod, cos_t, sin_t, norm_g[l], w_in[l], win_all, conv_w[l], q_a_g[l], w_q_b[l],
                     kv_a_g[l], w_kv_b[l], q_g[l], k_g[l], w_out[l], bsz, seq)
    return x2d.reshape(bsz, seq, D_MODEL)
```

```python
import functools
import itertools
import math

import jax
import jax.numpy as jnp
from jax import lax
from jax.experimental import pallas as pl
from jax.experimental.pallas import tpu as pltpu

D_MODEL = 2048
D_CONV = 1024
CONV_WIDTH = 3
N_HEADS = 8
QK_NOPE = 128
QK_ROPE = 64
QK_HEAD = QK_NOPE + QK_ROPE
V_HEAD = 128
D_ATTN = N_HEADS * V_HEAD
Q_LORA = 512
KV_LORA = 256
ROPE_BASE = 10000.0
D_MIX = D_CONV + D_ATTN
EPS = 1e-6

KV_HEAD = QK_NOPE + V_HEAD
MXU_N = 256
LANES = 128
SUBLANES = 8
ROPE_SLAB = LANES
HEAD_SLAB = QK_NOPE + ROPE_SLAB
HALF = QK_ROPE // 2
SHIFT_LANE = HALF
EXP2_NORMAL_RANGE = 120.0

OFF_XC = 0
OFF_BC = D_CONV
OFF_CC = 2 * D_CONV
OFF_ZC = 3 * D_CONV
OFF_CQ = 4 * D_CONV
OFF_CKV = OFF_CQ + Q_LORA
OFF_KR = OFF_CKV + KV_LORA
OFF_ZA = OFF_KR + QK_ROPE
TAIL_COLS = ROPE_SLAB + D_ATTN

ADA_TN = 1024
ROPE_TT = 2048
CONV_TM = 1024
MLA_TM = 512
CONV_CW = 256
HEAD_GROUPS = (4, 4)
ATTN_TQ = 1024
ATTN_TK = 512
OUT_TM = 512
VMEM_LIMIT = 56 * 1024 * 1024

_BF16 = jnp.bfloat16
_F32 = jnp.float32


def _silu(z):
    hz = 0.5 * z
    return hz + hz * jnp.tanh(hz)


def _dot(a, b):
    return jnp.dot(a, b, preferred_element_type=_F32)


def _adaln_body(ct_ref, w_ref, b_ref, o_ref):
    ct = ct_ref[...]
    sc = _silu(ct)
    w = w_ref[...]
    rows = [jnp.sum(w * sc[:, b:b + 1], axis=0, keepdims=True) for b in range(ct.shape[1])]
    o_ref[...] = jnp.concatenate(rows, axis=0) + b_ref[...]


def _adaln_mod(c, ada_w, ada_b):
    bsz = c.shape[0]
    n = ada_w.shape[1]
    return pl.pallas_call(
        _adaln_body,
        grid=(n // ADA_TN,),
        in_specs=[
            pl.BlockSpec((D_MODEL, bsz), lambda j: (0, 0)),
            pl.BlockSpec((D_MODEL, ADA_TN), lambda j: (0, j)),
            pl.BlockSpec((1, ADA_TN), lambda j: (0, j)),
        ],
        out_specs=pl.BlockSpec((bsz, ADA_TN), lambda j: (0, j)),
        out_shape=jax.ShapeDtypeStruct((bsz, n), _F32),
        name="adaln_mod",
    )(c.T, ada_w, ada_b.reshape(1, n))


def _rope_body(pos_ref, f_ref, cos_ref, sin_ref):
    ang = f_ref[...] * pos_ref[...].astype(_F32)
    c = jnp.cos(ang)
    s = jnp.sin(ang)
    z = jnp.zeros_like(c)
    cos_ref[...] = jnp.concatenate([c, z, c, z], axis=0).T
    sin_ref[...] = jnp.concatenate([-s, z, s, z], axis=0).T


def _rope_tables(positions):
    t = positions.size
    inv_freq = ROPE_BASE ** (-jnp.arange(0, QK_ROPE, 2, dtype=_F32) / QK_ROPE)
    out = pl.BlockSpec((ROPE_TT, ROPE_SLAB), lambda i: (i, 0))
    return pl.pallas_call(
        _rope_body,
        grid=(t // ROPE_TT,),
        in_specs=[pl.BlockSpec((1, ROPE_TT), lambda i: (0, i)), pl.BlockSpec((HALF, 1), lambda i: (0, 0))],
        out_specs=[out, out],
        out_shape=[jax.ShapeDtypeStruct((t, ROPE_SLAB), _F32)] * 2,
        name="rope_tables",
    )(positions.reshape(1, t), inv_freq.reshape(HALF, 1))


def _modulated_norm(x_ref, mod_ref, ng_ref):
    x = x_ref[...]
    xn = x * lax.rsqrt(jnp.mean(x * x, axis=-1, keepdims=True) + EPS)
    a = ng_ref[...] * (1.0 + mod_ref[0, 1:2, :])
    return (xn * a + mod_ref[0, 0:1, :]).astype(_BF16)


def _conv_proj_body(tiles_per_batch, h_ref, win_ref, convw_ref, yconv_ref, carry_scr, vbuf_scr):
    tm = h_ref.shape[0]

    @pl.when(pl.program_id(0) % tiles_per_batch == 0)
    def _():
        carry_scr[...] = jnp.zeros_like(carry_scr)

    def proj(off, width):
        return _dot(h_ref[...], win_ref[:, off:off + width])

    for lo in range(0, D_CONV, CONV_CW):
        xc = proj(OFF_XC + lo, CONV_CW)
        bc = proj(OFF_BC + lo, CONV_CW)
        cc = proj(OFF_CC + lo, CONV_CW)
        zc = proj(OFF_ZC + lo, CONV_CW)
        v0 = cc * xc
        vbuf_scr[0:SUBLANES, :] = carry_scr[:, lo:lo + CONV_CW]
        vbuf_scr[SUBLANES:SUBLANES + tm, :] = v0
        carry_scr[:, lo:lo + CONV_CW] = v0[tm - SUBLANES:, :]
        v1 = vbuf_scr[SUBLANES - 1:SUBLANES - 1 + tm, :]
        v2 = vbuf_scr[SUBLANES - 2:SUBLANES - 2 + tm, :]
        w = convw_ref[:, lo:lo + CONV_CW]
        conv = w[0:1, :] * v2 + w[1:2, :] * v1 + w[2:3, :] * v0
        yconv_ref[:, lo:lo + CONV_CW] = (bc * conv * _silu(zc)).astype(_BF16)


def _resident(shape, index=None):
    index = (0,) * len(shape) if index is None else index
    return pl.BlockSpec(shape, lambda i: index, pipeline_mode=pl.Buffered(1))


def _conv_proj(layer, h2d, win_all, conv_w, seq):
    t = h2d.shape[0]
    tm = CONV_TM
    tpb = seq // tm
    return pl.pallas_call(
        functools.partial(_conv_proj_body, tpb),
        grid=(t // tm,),
        in_specs=[
            pl.BlockSpec((tm, D_MODEL), lambda i: (i, 0)),
            _resident((None, D_MODEL, 4 * D_CONV), (layer, 0, 0)),
            _resident((CONV_WIDTH, D_CONV)),
        ],
        out_specs=pl.BlockSpec((tm, D_CONV), lambda i: (i, 0)),
        out_shape=jax.ShapeDtypeStruct((t, D_CONV), _BF16),
        scratch_shapes=[
            pltpu.VMEM((SUBLANES, D_CONV), _F32),
            pltpu.VMEM((SUBLANES + tm, CONV_CW), _F32),
        ],
        compiler_params=pltpu.CompilerParams(
            dimension_semantics=("arbitrary",), vmem_limit_bytes=VMEM_LIMIT),
        name="conv_proj",
    )(h2d, win_all, conv_w)


def _mla_proj_body(x_ref, mod_ref, ng_ref, wcq_ref, wckv_ref, wtail_ref, qag_ref, wqb_ref, kvag_ref, wkvb_ref,
                   gq_ref, gk_ref, qb_ref, kb_ref, cos_ref, sin_ref,
                   h_scr, sz_ref, q_ref, kt_ref, v_ref, cqn_scr, ckvn_scr):
    h_scr[...] = _modulated_norm(x_ref, mod_ref, ng_ref)
    cos_t = cos_ref[...]
    sin_t = sin_ref[...]

    def rope(slab):
        return slab * cos_t + pltpu.roll(slab, ROPE_SLAB // 2, axis=1) * sin_t

    def rms(parts, width):
        ss = sum(jnp.sum(p * p, axis=-1, keepdims=True) for p in parts)
        return lax.rsqrt(ss * (1.0 / width) + EPS)

    cq = _dot(h_scr[...], wcq_ref[...])
    cqn_scr[...] = (cq * rms([cq], Q_LORA) * qag_ref[...]).astype(_BF16)
    ckv = _dot(h_scr[...], wckv_ref[...])
    ckvn_scr[...] = (ckv * rms([ckv], KV_LORA) * kvag_ref[...]).astype(_BF16)

    kr = _dot(h_scr[...], wtail_ref[:, 0:ROPE_SLAB])
    gk = gk_ref[...]
    gq = gq_ref[...]
    kr_rot = rope(kr * gk[:, QK_NOPE:HEAD_SLAB])

    def lane_sum(sq):
        ones = jnp.ones((sq.shape[1], LANES), _BF16)
        return _dot(sq.astype(_BF16), ones)

    ss_r = lane_sum(kr * kr)
    assert sum(HEAD_GROUPS) == N_HEADS
    for g0, group in zip(itertools.accumulate((0,) + HEAD_GROUPS), HEAD_GROUPS):
        heads = range(g0, g0 + group)
        gate_cols = group * V_HEAD
        slabs = [_dot(cqn_scr[...], wqb_ref[:, hd * HEAD_SLAB:(hd + 1) * HEAD_SLAB]) for hd in heads]
        kvs = [_dot(ckvn_scr[...], wkvb_ref[:, hd * KV_HEAD:(hd + 1) * KV_HEAD]) for hd in heads]
        ssq = [lane_sum(s * s) for s in slabs]
        ssk = [lane_sum(kv[:, 0:QK_NOPE] * kv[:, 0:QK_NOPE]) for kv in kvs]
        glo = g0 * V_HEAD
        zas = [(lo, min(2 * MXU_N, glo + gate_cols - lo)) for lo in range(glo, glo + gate_cols, 2 * MXU_N)]
        zas = [(lo, w, _dot(h_scr[...], wtail_ref[:, ROPE_SLAB + lo:ROPE_SLAB + lo + w])) for lo, w in zas]
        for hd, slab, kv, sq, sk in zip(heads, slabs, kvs, ssq, ssk):
            r = lax.rsqrt(sq * (1.0 / QK_HEAD) + EPS)
            q_ref[0, hd, :, 0:QK_NOPE] = (slab[:, 0:QK_NOPE] * r * gq[:, 0:QK_NOPE]).astype(_BF16)
            q_rope = rope(slab[:, QK_NOPE:HEAD_SLAB] * r * gq[:, QK_NOPE:HEAD_SLAB])
            q_ref[0, hd, :, QK_NOPE:HEAD_SLAB] = (q_rope + qb_ref[...]).astype(_BF16)
            r = lax.rsqrt((sk + ss_r) * (1.0 / QK_HEAD) + EPS)
            kt_ref[0, hd, 0:QK_NOPE, :] = (kv[:, 0:QK_NOPE] * r * gk[:, 0:QK_NOPE]).T.astype(_BF16)
            kt_ref[0, hd, QK_NOPE:HEAD_SLAB, :] = (kr_rot * r + kb_ref[...]).T.astype(_BF16)
            v_ref[0, hd, :, :] = kv[:, QK_NOPE:KV_HEAD].astype(_BF16)
        for lo, w, za in zas:
            sz_ref[:, lo:lo + w] = _silu(za).astype(_BF16)


def _mla_proj(layer, x2d, mod3, norm_g, win_all, wtail, q_a_g, wqb, kv_a_g, wkvb, gq, gk, qb, kb, cos_t, sin_t,
              bsz, seq):
    t = x2d.shape[0]
    tm = MLA_TM
    tpb = seq // tm
    tok = lambda width: pl.BlockSpec((tm, width), lambda i: (i, 0))
    head = lambda width: pl.BlockSpec((1, N_HEADS, tm, width), lambda i: (i // tpb, 0, i % tpb, 0))
    return pl.pallas_call(
        _mla_proj_body,
        grid=(t // tm,),
        in_specs=[
            tok(D_MODEL),
            pl.BlockSpec((1, 3, D_MODEL), lambda i: (i // tpb, 0, 0)),
            _resident((1, D_MODEL)),
            _resident((None, D_MODEL, Q_LORA), (layer, 0, OFF_CQ // Q_LORA)),
            _resident((None, D_MODEL, KV_LORA), (layer, 0, OFF_CKV // KV_LORA)),
            _resident((D_MODEL, TAIL_COLS)),
            _resident((1, Q_LORA)),
            _resident((Q_LORA, N_HEADS * HEAD_SLAB)),
            _resident((1, KV_LORA)),
            _resident((KV_LORA, N_HEADS * (QK_NOPE + V_HEAD))),
            _resident((1, HEAD_SLAB)),
            _resident((1, HEAD_SLAB)),
            _resident((1, ROPE_SLAB)),
            _resident((1, ROPE_SLAB)),
            tok(ROPE_SLAB),
            tok(ROPE_SLAB),
        ],
        out_specs=[
            tok(D_MODEL),
            tok(D_ATTN),
            head(HEAD_SLAB),
            pl.BlockSpec((1, N_HEADS, HEAD_SLAB, tm), lambda i: (i // tpb, 0, 0, i % tpb)),
            head(V_HEAD),
        ],
        out_shape=[
            jax.ShapeDtypeStruct((t, D_MODEL), _BF16),
            jax.ShapeDtypeStruct((t, D_ATTN), _BF16),
            jax.ShapeDtypeStruct((bsz, N_HEADS, seq, HEAD_SLAB), _BF16),
            jax.ShapeDtypeStruct((bsz, N_HEADS, HEAD_SLAB, seq), _BF16),
            jax.ShapeDtypeStruct((bsz, N_HEADS, seq, V_HEAD), _BF16),
        ],
        scratch_shapes=[
            pltpu.VMEM((tm, Q_LORA), _BF16),
            pltpu.VMEM((tm, KV_LORA), _BF16),
        ],
        compiler_params=pltpu.CompilerParams(
            dimension_semantics=("arbitrary",), vmem_limit_bytes=VMEM_LIMIT),
        name="mla_proj",
    )(x2d, mod3, norm_g, win_all, win_all, wtail, q_a_g, wqb, kv_a_g, wkvb, gq, gk, qb, kb, cos_t, sin_t)


def _attn_online_body(q_ref, kt_ref, v_ref, sz_ref, o_ref, sa0_scr, sa1_scr, sb_scr, m_scr, l_scr, acc_scr):
    tq = ATTN_TQ
    tk = ATTN_TK
    half = tq // 2
    assert tq == 2 * tk
    nq = q_ref.shape[2] // tq
    assert nq % 2 == 0

    def scores(qi, j, r0, nr):
        qstart = pl.multiple_of(qi * tq + r0, half)
        kstart = pl.multiple_of(j * tk, tk)
        return _dot(q_ref[0, 0, pl.ds(qstart, nr), :], kt_ref[0, 0, :, pl.ds(kstart, tk)])

    def update(j, s, r0, nr, triangular):
        start = pl.multiple_of(j * tk, tk)
        v = v_ref[0, 0, pl.ds(start, tk), :]
        if triangular:
            row = lax.broadcasted_iota(jnp.int32, (nr, tk), 0)
            col = lax.broadcasted_iota(jnp.int32, (nr, tk), 1)
            s = jnp.where(col <= row, s, -jnp.inf)
        tiles = [s[:, t * LANES:(t + 1) * LANES] for t in range(tk // LANES)]
        mx = functools.reduce(jnp.maximum, tiles)
        m_prev = m_scr[r0:r0 + nr, :]
        m_new = jnp.maximum(m_prev, jnp.max(mx, axis=-1, keepdims=True))
        alpha = jnp.exp2(m_prev - m_new)
        ps = [jnp.exp2(t - m_new) for t in tiles]
        l_scr[r0:r0 + nr, :] = alpha * l_scr[r0:r0 + nr, :] + functools.reduce(jnp.add, ps)
        p = jnp.concatenate(ps, axis=-1).astype(_BF16)
        acc_scr[r0:r0 + nr, :] = alpha * acc_scr[r0:r0 + nr, :] + _dot(p, v)
        m_scr[r0:r0 + nr, :] = m_new

    def query_block(qi, sa_scr, sa_next_scr):
        m_scr[...] = jnp.full_like(m_scr, -jnp.inf)
        l_scr[...] = jnp.zeros_like(l_scr)
        acc_scr[...] = jnp.zeros_like(acc_scr)

        def pair(jj, carry):
            j = 2 * jj
            sb_scr[...] = scores(qi, j + 1, 0, tq)
            update(j, sa_scr[...], 0, tq, False)
            sa_scr[...] = scores(qi, j + 2, 0, tq)
            update(j + 1, sb_scr[...], 0, tq, False)
            return carry

        lax.fori_loop(0, qi, pair, 0)

        jd = 2 * qi
        sb_scr[half:tq, :] = scores(qi, jd + 1, half, half)
        update(jd, sa_scr[0:half, :], 0, half, True)
        sa_next_scr[...] = scores(jnp.minimum(qi + 1, nq - 1), 0, 0, tq)
        update(jd, sa_scr[half:tq, :], half, half, False)
        update(jd + 1, sb_scr[half:tq, :], half, half, True)

        rows = pl.ds(pl.multiple_of(qi * tq, tq), tq)
        o = acc_scr[...] / jnp.sum(l_scr[...], axis=-1, keepdims=True)
        o_ref[rows, :] = (o * sz_ref[rows, :].astype(_F32)).astype(_BF16)

    sa0_scr[...] = scores(0, 0, 0, tq)

    def two_blocks(ib, carry):
        query_block(2 * ib, sa0_scr, sa1_scr)
        query_block(2 * ib + 1, sa1_scr, sa0_scr)
        return carry

    lax.fori_loop(0, nq // 2, two_blocks, 0)


def _attn_shifted_body(q_ref, kt_ref, v_ref, sz_ref, o_ref, pa0_scr, pa1_scr, pb_scr, lnext_scr, l_scr, acc_scr):
    tq = ATTN_TQ
    tk = ATTN_TK
    half = tq // 2
    assert tq == 2 * tk
    nq = q_ref.shape[2] // tq
    assert nq % 2 == 0
    col_minus_row = (lax.broadcasted_iota(jnp.int32, (tq, tk), 1)
                     - lax.broadcasted_iota(jnp.int32, (tq, tk), 0))

    def probs(qi, j, r0, nr, masked):
        qstart = pl.multiple_of(qi * tq + r0, half)
        kstart = pl.multiple_of(j * tk, tk)
        s = _dot(q_ref[0, 0, pl.ds(qstart, nr), :], kt_ref[0, 0, :, pl.ds(kstart, tk)])
        if masked:
            s = jnp.where(col_minus_row[r0:r0 + nr, :] <= qi * tq - j * tk, s, -jnp.inf)
        ps = [jnp.exp2(s[:, t * LANES:(t + 1) * LANES]) for t in range(tk // LANES)]
        return jnp.concatenate(ps, axis=-1).astype(_BF16), functools.reduce(jnp.add, ps)

    def values(j):
        return v_ref[0, 0, pl.ds(pl.multiple_of(j * tk, tk), tk), :]

    def query_block(qi, pa_scr, pa_next_scr):
        l_scr[...] = lnext_scr[...]
        acc_scr[...] = jnp.zeros_like(acc_scr)

        def pair(jj, carry):
            j = 2 * jj
            p, ls = probs(qi, j + 1, 0, tq, False)
            pb_scr[...] = p
            l_scr[...] += ls
            acc_scr[...] += _dot(pa_scr[...], values(j))
            p, ls = probs(qi, j + 2, 0, tq, True)
            pa_scr[...] = p
            l_scr[...] += ls
            acc_scr[...] += _dot(pb_scr[...], values(j + 1))
            return carry

        lax.fori_loop(0, qi, pair, 0)

        jd = 2 * qi
        p, ls = probs(qi, jd + 1, half, half, True)
        pb_scr[half:tq, :] = p
        l_scr[half:tq, :] += ls
        acc_scr[...] += _dot(pa_scr[...], values(jd))
        nxt = jnp.minimum(qi + 1, nq - 1)
        p, ls = probs(nxt, 0, 0, tq, True)
        pa_next_scr[...] = p
        lnext_scr[...] = ls
        acc_scr[half:tq, :] += _dot(pb_scr[half:tq, :], values(jd + 1))

        rows = pl.ds(pl.multiple_of(qi * tq, tq), tq)
        o = acc_scr[...] / jnp.sum(l_scr[...], axis=-1, keepdims=True)
        o_ref[rows, :] = (o * sz_ref[rows, :].astype(_F32)).astype(_BF16)

    p, ls = probs(0, 0, 0, tq, True)
    pa0_scr[...] = p
    lnext_scr[...] = ls

    def two_blocks(ib, carry):
        query_block(2 * ib, pa0_scr, pa1_scr)
        query_block(2 * ib + 1, pa1_scr, pa0_scr)
        return carry

    lax.fori_loop(0, nq // 2, two_blocks, 0)


def _attention(online_max, q, k, v, sz):
    bsz, _, seq, _ = q.shape
    tq = ATTN_TQ
    per_head = lambda width: pl.BlockSpec((1, 1, seq, width), lambda b, h: (b, h, 0, 0))
    stats = [pltpu.VMEM((tq, LANES), _F32), pltpu.VMEM((tq, LANES), _F32), pltpu.VMEM((tq, V_HEAD), _F32)]
    if online_max:
        body = _attn_online_body
        scratch = [pltpu.VMEM((tq, ATTN_TK), _F32)] * 3 + stats
    else:
        body = _attn_shifted_body
        scratch = [pltpu.VMEM((tq, ATTN_TK), _BF16)] * 3 + stats
    return pl.pallas_call(
        body,
        grid=(bsz, N_HEADS),
        in_specs=[
            per_head(HEAD_SLAB),
            pl.BlockSpec((1, 1, HEAD_SLAB, seq), lambda b, h: (b, h, 0, 0)),
            per_head(V_HEAD),
            pl.BlockSpec((seq, V_HEAD), lambda b, h: (b, h)),
        ],
        out_specs=pl.BlockSpec((seq, V_HEAD), lambda b, h: (b, h)),
        out_shape=jax.ShapeDtypeStruct((bsz * seq, D_ATTN), _BF16),
        scratch_shapes=scratch,
        compiler_params=pltpu.CompilerParams(
            dimension_semantics=("arbitrary", "arbitrary"), vmem_limit_bytes=VMEM_LIMIT),
        name="attention_online_max" if online_max else "attention_shifted",
    )(q, k, v, sz)


def _out_proj_body(yc_ref, ya_ref, x_ref, mod_ref, w32_ref, o_ref, w_scr):
    @pl.when(pl.program_id(0) == 0)
    def _():
        w_scr[...] = w32_ref[...].astype(_BF16)

    y = _dot(yc_ref[...], w_scr[0:D_CONV, :]) + _dot(ya_ref[...], w_scr[D_CONV:D_MIX, :])
    o_ref[...] = x_ref[...] + mod_ref[0, 2:3, :] * y


def _out_proj(yconv, yattn, x2d, mod3, wout, seq):
    t = x2d.shape[0]
    tm = OUT_TM
    tpb = seq // tm
    return pl.pallas_call(
        _out_proj_body,
        grid=(t // tm,),
        in_specs=[
            pl.BlockSpec((tm, D_CONV), lambda i: (i, 0)),
            pl.BlockSpec((tm, D_ATTN), lambda i: (i, 0)),
            pl.BlockSpec((tm, D_MODEL), lambda i: (i, 0)),
            pl.BlockSpec((1, 3, D_MODEL), lambda i: (i // tpb, 0, 0)),
            _resident((D_MIX, D_MODEL)),
        ],
        out_specs=pl.BlockSpec((tm, D_MODEL), lambda i: (i, 0)),
        out_shape=jax.ShapeDtypeStruct((t, D_MODEL), _F32),
        scratch_shapes=[pltpu.VMEM((D_MIX, D_MODEL), _BF16)],
        compiler_params=pltpu.CompilerParams(
            dimension_semantics=("arbitrary",), vmem_limit_bytes=VMEM_LIMIT),
        name="out_proj",
    )(yconv, yattn, x2d, mod3, wout)


def _score_shift(q_g, k_g, qscale):
    bound = QK_HEAD * qscale * jnp.max(jnp.abs(q_g)) * jnp.max(jnp.abs(k_g))
    shift = (bound * 1.03 + 1.0).astype(_BF16).astype(_F32)
    use_shift = bound + shift <= EXP2_NORMAL_RANGE
    return jnp.where(use_shift, shift, 0.0), use_shift


def _rope_slab_cols(w):
    z = jnp.zeros(w.shape[:-1] + (HALF,), w.dtype)
    return jnp.concatenate([w[..., :HALF], z, w[..., HALF:], z], axis=-1)


def _layer(layer, x2d, mod, cos_t, sin_t, norm_g, w_in, win_all, conv_w, q_a_g, w_q_b, kv_a_g, w_kv_b, q_g, k_g,
           w_out, bsz, seq):
    wtail = jnp.concatenate([_rope_slab_cols(w_in[:, OFF_KR:OFF_ZA]), w_in[:, OFF_ZA:]], axis=-1).astype(_BF16)
    wq = w_q_b.reshape(Q_LORA, N_HEADS, QK_HEAD)
    wqb = jnp.concatenate([wq[..., :QK_NOPE], _rope_slab_cols(wq[..., QK_NOPE:])], axis=-1)
    wqb = wqb.reshape(Q_LORA, N_HEADS * HEAD_SLAB).astype(_BF16)
    wkvb = w_kv_b.astype(_BF16)
    qscale = math.log2(math.e) / math.sqrt(QK_HEAD)
    gq = jnp.concatenate([q_g[:QK_NOPE], _rope_slab_cols(q_g[QK_NOPE:])]).reshape(1, HEAD_SLAB) * qscale
    gk = jnp.concatenate([k_g[:QK_NOPE], _rope_slab_cols(k_g[QK_NOPE:])]).reshape(1, HEAD_SLAB)
    mod3 = mod.reshape(bsz, 3, D_MODEL)
    shift, use_shift = _score_shift(q_g, k_g, qscale)
    pad_lane = jnp.arange(ROPE_SLAB) == SHIFT_LANE
    qb = jnp.where(pad_lane, -shift, 0.0).reshape(1, ROPE_SLAB)
    kb = jnp.where(pad_lane, 1.0, 0.0).reshape(1, ROPE_SLAB)

    ng = norm_g.reshape(1, D_MODEL)
    h2d, sz, q, k, v = _mla_proj(
        layer, x2d, mod3, ng, win_all, wtail, q_a_g.reshape(1, Q_LORA), wqb,
        kv_a_g.reshape(1, KV_LORA), wkvb, gq, gk, qb, kb, cos_t, sin_t, bsz, seq)
    yconv = _conv_proj(layer, h2d, win_all, conv_w, seq)
    yattn = lax.cond(use_shift, functools.partial(_attention, False), functools.partial(_attention, True),
                     q, k, v, sz)
    return _out_proj(yconv, yattn, x2d, mod3, w_out, seq)


def kernel(x, c, positions, ada_w, ada_b, norm_g, w_in, conv_w, q_a_g, w_q_b, kv_a_g, w_kv_b, q_g, k_g, w_out):
    bsz, seq, _ = x.shape
    depth = ada_w.shape[0]
    cos_t, sin_t = _rope_tables(positions)
    x2d = x.reshape(bsz * seq, D_MODEL)
    win_all = w_in.astype(_BF16)
    for l in range(depth):
        mod = _adaln_mod(c, ada_w[l], ada_b[l])
        x2d = _layer(l, x2d, mod, cos_t, sin_t, norm_g[l], w_in[l], win_all, conv_w[l], q_a_g[l], w_q_b[l],
                     kv_a_g[l], w_kv_b[l], q_g[l], k_g[l], w_out[l], bsz, seq)
    return x2d.reshape(bsz, seq, D_MODEL)
```

```python
import functools
import itertools
import math

import jax
import jax.numpy as jnp
from jax import lax
from jax.experimental import pallas as pl
from jax.experimental.pallas import tpu as pltpu

D_MODEL = 2048
D_CONV = 1024
CONV_WIDTH = 3
N_HEADS = 8
QK_NOPE = 128
QK_ROPE = 64
QK_HEAD = QK_NOPE + QK_ROPE
V_HEAD = 128
D_ATTN = N_HEADS * V_HEAD
Q_LORA = 512
KV_LORA = 256
ROPE_BASE = 10000.0
D_MIX = D_CONV + D_ATTN
EPS = 1e-6

KV_HEAD = QK_NOPE + V_HEAD
MXU_N = 256
LANES = 128
SUBLANES = 8
ROPE_SLAB = LANES
HEAD_SLAB = QK_NOPE + ROPE_SLAB
HALF = QK_ROPE // 2
SHIFT_LANE = HALF
EXP2_NORMAL_RANGE = 120.0

OFF_XC = 0
OFF_BC = D_CONV
OFF_CC = 2 * D_CONV
OFF_ZC = 3 * D_CONV
OFF_CQ = 4 * D_CONV
OFF_CKV = OFF_CQ + Q_LORA
OFF_KR = OFF_CKV + KV_LORA
OFF_ZA = OFF_KR + QK_ROPE
TAIL_COLS = ROPE_SLAB + D_ATTN

ADA_TN = 1024
ROPE_TT = 2048
CONV_TM = 1024
MLA_TM = 512
CONV_CW = 256
HEAD_GROUPS = (4, 4)
ATTN_TQ = 1024
ATTN_TK = 512
OUT_TM = 512
VMEM_LIMIT = 56 * 1024 * 1024

_BF16 = jnp.bfloat16
_F32 = jnp.float32


def _silu(z):
    hz = 0.5 * z
    return hz + hz * jnp.tanh(hz)


def _dot(a, b):
    return jnp.dot(a, b, preferred_element_type=_F32)


def _adaln_body(ct_ref, w_ref, b_ref, o_ref):
    ct = ct_ref[...]
    sc = _silu(ct)
    w = w_ref[...]
    rows = [jnp.sum(w * sc[:, b:b + 1], axis=0, keepdims=True) for b in range(ct.shape[1])]
    o_ref[...] = jnp.concatenate(rows, axis=0) + b_ref[...]


def _adaln_mod(c, ada_w, ada_b):
    bsz = c.shape[0]
    n = ada_w.shape[1]
    return pl.pallas_call(
        _adaln_body,
        grid=(n // ADA_TN,),
        in_specs=[
            pl.BlockSpec((D_MODEL, bsz), lambda j: (0, 0)),
            pl.BlockSpec((D_MODEL, ADA_TN), lambda j: (0, j)),
            pl.BlockSpec((1, ADA_TN), lambda j: (0, j)),
        ],
        out_specs=pl.BlockSpec((bsz, ADA_TN), lambda j: (0, j)),
        out_shape=jax.ShapeDtypeStruct((bsz, n), _F32),
        name="adaln_mod",
    )(c.T, ada_w, ada_b.reshape(1, n))


def _rope_body(pos_ref, f_ref, cos_ref, sin_ref):
    ang = f_ref[...] * pos_ref[...].astype(_F32)
    c = jnp.cos(ang)
    s = jnp.sin(ang)
    z = jnp.zeros_like(c)
    cos_ref[...] = jnp.concatenate([c, z, c, z], axis=0).T
    sin_ref[...] = jnp.concatenate([-s, z, s, z], axis=0).T


def _rope_tables(positions):
    t = positions.size
    inv_freq = ROPE_BASE ** (-jnp.arange(0, QK_ROPE, 2, dtype=_F32) / QK_ROPE)
    out = pl.BlockSpec((ROPE_TT, ROPE_SLAB), lambda i: (i, 0))
    return pl.pallas_call(
        _rope_body,
        grid=(t // ROPE_TT,),
        in_specs=[pl.BlockSpec((1, ROPE_TT), lambda i: (0, i)), pl.BlockSpec((HALF, 1), lambda i: (0, 0))],
        out_specs=[out, out],
        out_shape=[jax.ShapeDtypeStruct((t, ROPE_SLAB), _F32)] * 2,
        name="rope_tables",
    )(positions.reshape(1, t), inv_freq.reshape(HALF, 1))


def _modulated_norm(x_ref, mod_ref, ng_ref):
    x = x_ref[...]
    xn = x * lax.rsqrt(jnp.mean(x * x, axis=-1, keepdims=True) + EPS)
    a = ng_ref[...] * (1.0 + mod_ref[0, 1:2, :])
    return (xn * a + mod_ref[0, 0:1, :]).astype(_BF16)


def _conv_proj_body(tiles_per_batch, h_ref, win_ref, convw_ref, yconv_ref, carry_scr, vbuf_scr):
    tm = h_ref.shape[0]

    @pl.when(pl.program_id(0) % tiles_per_batch == 0)
    def _():
        carry_scr[...] = jnp.zeros_like(carry_scr)

    def proj(off, width):
        return _dot(h_ref[...], win_ref[:, off:off + width])

    for lo in range(0, D_CONV, CONV_CW):
        xc = proj(OFF_XC + lo, CONV_CW)
        bc = proj(OFF_BC + lo, CONV_CW)
        cc = proj(OFF_CC + lo, CONV_CW)
        zc = proj(OFF_ZC + lo, CONV_CW)
        v0 = cc * xc
        vbuf_scr[0:SUBLANES, :] = carry_scr[:, lo:lo + CONV_CW]
        vbuf_scr[SUBLANES:SUBLANES + tm, :] = v0
        carry_scr[:, lo:lo + CONV_CW] = v0[tm - SUBLANES:, :]
        v1 = vbuf_scr[SUBLANES - 1:SUBLANES - 1 + tm, :]
        v2 = vbuf_scr[SUBLANES - 2:SUBLANES - 2 + tm, :]
        w = convw_ref[:, lo:lo + CONV_CW]
        conv = w[0:1, :] * v2 + w[1:2, :] * v1 + w[2:3, :] * v0
        yconv_ref[:, lo:lo + CONV_CW] = (bc * conv * _silu(zc)).astype(_BF16)


def _resident(shape, index=None):
    index = (0,) * len(shape) if index is None else index
    return pl.BlockSpec(shape, lambda i: index, pipeline_mode=pl.Buffered(1))


def _conv_proj(layer, h2d, win_all, conv_w, seq):
    t = h2d.shape[0]
    tm = CONV_TM
    tpb = seq // tm
    return pl.pallas_call(
        functools.partial(_conv_proj_body, tpb),
        grid=(t // tm,),
        in_specs=[
            pl.BlockSpec((tm, D_MODEL), lambda i: (i, 0)),
            _resident((None, D_MODEL, 4 * D_CONV), (layer, 0, 0)),
            _resident((CONV_WIDTH, D_CONV)),
        ],
        out_specs=pl.BlockSpec((tm, D_CONV), lambda i: (i, 0)),
        out_shape=jax.ShapeDtypeStruct((t, D_CONV), _BF16),
        scratch_shapes=[
            pltpu.VMEM((SUBLANES, D_CONV), _F32),
            pltpu.VMEM((SUBLANES + tm, CONV_CW), _F32),
        ],
        compiler_params=pltpu.CompilerParams(
            dimension_semantics=("arbitrary",), vmem_limit_bytes=VMEM_LIMIT),
        name="conv_proj",
    )(h2d, win_all, conv_w)


def _mla_proj_body(x_ref, mod_ref, ng_ref, wcq_ref, wckv_ref, wtail_ref, qag_ref, wqb_ref, kvag_ref, wkvb_ref,
                   gq_ref, gk_ref, qb_ref, kb_ref, cos_ref, sin_ref,
                   h_scr, sz_ref, q_ref, kt_ref, v_ref, cqn_scr, ckvn_scr):
    h_scr[...] = _modulated_norm(x_ref, mod_ref, ng_ref)
    cos_t = cos_ref[...]
    sin_t = sin_ref[...]

    def rope(slab):
        return slab * cos_t + pltpu.roll(slab, ROPE_SLAB // 2, axis=1) * sin_t

    def rms(parts, width):
        ss = sum(jnp.sum(p * p, axis=-1, keepdims=True) for p in parts)
        return lax.rsqrt(ss * (1.0 / width) + EPS)

    cq = _dot(h_scr[...], wcq_ref[...])
    cqn_scr[...] = (cq * rms([cq], Q_LORA) * qag_ref[...]).astype(_BF16)
    ckv = _dot(h_scr[...], wckv_ref[...])
    ckvn_scr[...] = (ckv * rms([ckv], KV_LORA) * kvag_ref[...]).astype(_BF16)

    kr = _dot(h_scr[...], wtail_ref[:, 0:ROPE_SLAB])
    gk = gk_ref[...]
    gq = gq_ref[...]
    kr_rot = rope(kr * gk[:, QK_NOPE:HEAD_SLAB])

    def lane_sum(sq):
        ones = jnp.ones((sq.shape[1], LANES), _BF16)
        return _dot(sq.astype(_BF16), ones)

    ss_r = lane_sum(kr * kr)
    assert sum(HEAD_GROUPS) == N_HEADS
    for g0, group in zip(itertools.accumulate((0,) + HEAD_GROUPS), HEAD_GROUPS):
        heads = range(g0, g0 + group)
        gate_cols = group * V_HEAD
        slabs = [_dot(cqn_scr[...], wqb_ref[:, hd * HEAD_SLAB:(hd + 1) * HEAD_SLAB]) for hd in heads]
        kvs = [_dot(ckvn_scr[...], wkvb_ref[:, hd * KV_HEAD:(hd + 1) * KV_HEAD]) for hd in heads]
        ssq = [lane_sum(s * s) for s in slabs]
        ssk = [lane_sum(kv[:, 0:QK_NOPE] * kv[:, 0:QK_NOPE]) for kv in kvs]
        glo = g0 * V_HEAD
        zas = [(lo, min(2 * MXU_N, glo + gate_cols - lo)) for lo in range(glo, glo + gate_cols, 2 * MXU_N)]
        zas = [(lo, w, _dot(h_scr[...], wtail_ref[:, ROPE_SLAB + lo:ROPE_SLAB + lo + w])) for lo, w in zas]
        for hd, slab, kv, sq, sk in zip(heads, slabs, kvs, ssq, ssk):
            r = lax.rsqrt(sq * (1.0 / QK_HEAD) + EPS)
            q_ref[0, hd, :, 0:QK_NOPE] = (slab[:, 0:QK_NOPE] * r * gq[:, 0:QK_NOPE]).astype(_BF16)
            q_rope = rope(slab[:, QK_NOPE:HEAD_SLAB] * r * gq[:, QK_NOPE:HEAD_SLAB])
            q_ref[0, hd, :, QK_NOPE:HEAD_SLAB] = (q_rope + qb_ref[...]).astype(_BF16)
            r = lax.rsqrt((sk + ss_r) * (1.0 / QK_HEAD) + EPS)
            kt_ref[0, hd, 0:QK_NOPE, :] = (kv[:, 0:QK_NOPE] * r * gk[:, 0:QK_NOPE]).T.astype(_BF16)
            kt_ref[0, hd, QK_NOPE:HEAD_SLAB, :] = (kr_rot * r + kb_ref[...]).T.astype(_BF16)
            v_ref[0, hd, :, :] = kv[:, QK_NOPE:KV_HEAD].astype(_BF16)
        for lo, w, za in zas:
            sz_ref[:, lo:lo + w] = _silu(za).astype(_BF16)


def _mla_proj(layer, x2d, mod3, norm_g, win_all, wtail, q_a_g, wqb, kv_a_g, wkvb, gq, gk, qb, kb, cos_t, sin_t,
              bsz, seq):
    t = x2d.shape[0]
    tm = MLA_TM
    tpb = seq // tm
    tok = lambda width: pl.BlockSpec((tm, width), lambda i: (i, 0))
    head = lambda width: pl.BlockSpec((1, N_HEADS, tm, width), lambda i: (i // tpb, 0, i % tpb, 0))
    return pl.pallas_call(
        _mla_proj_body,
        grid=(t // tm,),
        in_specs=[
            tok(D_MODEL),
            pl.BlockSpec((1, 3, D_MODEL), lambda i: (i // tpb, 0, 0)),
            _resident((1, D_MODEL)),
            _resident((None, D_MODEL, Q_LORA), (layer, 0, OFF_CQ // Q_LORA)),
            _resident((None, D_MODEL, KV_LORA), (layer, 0, OFF_CKV // KV_LORA)),
            _resident((D_MODEL, TAIL_COLS)),
            _resident((1, Q_LORA)),
            _resident((Q_LORA, N_HEADS * HEAD_SLAB)),
            _resident((1, KV_LORA)),
            _resident((KV_LORA, N_HEADS * (QK_NOPE + V_HEAD))),
            _resident((1, HEAD_SLAB)),
            _resident((1, HEAD_SLAB)),
            _resident((1, ROPE_SLAB)),
            _resident((1, ROPE_SLAB)),
            tok(ROPE_SLAB),
            tok(ROPE_SLAB),
        ],
        out_specs=[
            tok(D_MODEL),
            tok(D_ATTN),
            head(HEAD_SLAB),
            pl.BlockSpec((1, N_HEADS, HEAD_SLAB, tm), lambda i: (i // tpb, 0, 0, i % tpb)),
            head(V_HEAD),
        ],
        out_shape=[
            jax.ShapeDtypeStruct((t, D_MODEL), _BF16),
            jax.ShapeDtypeStruct((t, D_ATTN), _BF16),
            jax.ShapeDtypeStruct((bsz, N_HEADS, seq, HEAD_SLAB), _BF16),
            jax.ShapeDtypeStruct((bsz, N_HEADS, HEAD_SLAB, seq), _BF16),
            jax.ShapeDtypeStruct((bsz, N_HEADS, seq, V_HEAD), _BF16),
        ],
        scratch_shapes=[
            pltpu.VMEM((tm, Q_LORA), _BF16),
            pltpu.VMEM((tm, KV_LORA), _BF16),
        ],
        compiler_params=pltpu.CompilerParams(
            dimension_semantics=("arbitrary",), vmem_limit_bytes=VMEM_LIMIT),
        name="mla_proj",
    )(x2d, mod3, norm_g, win_all, win_all, wtail, q_a_g, wqb, kv_a_g, wkvb, gq, gk, qb, kb, cos_t, sin_t)


def _attn_online_body(q_ref, kt_ref, v_ref, sz_ref, o_ref, sa0_scr, sa1_scr, sb_scr, m_scr, l_scr, acc_scr):
    tq = ATTN_TQ
    tk = ATTN_TK
    half = tq // 2
    assert tq == 2 * tk
    nq = q_ref.shape[2] // tq
    assert nq % 2 == 0

    def scores(qi, j, r0, nr):
        qstart = pl.multiple_of(qi * tq + r0, half)
        kstart = pl.multiple_of(j * tk, tk)
        return _dot(q_ref[0, 0, pl.ds(qstart, nr), :], kt_ref[0, 0, :, pl.ds(kstart, tk)])

    def update(j, s, r0, nr, triangular):
        start = pl.multiple_of(j * tk, tk)
        v = v_ref[0, 0, pl.ds(start, tk), :]
        if triangular:
            row = lax.broadcasted_iota(jnp.int32, (nr, tk), 0)
            col = lax.broadcasted_iota(jnp.int32, (nr, tk), 1)
            s = jnp.where(col <= row, s, -jnp.inf)
        tiles = [s[:, t * LANES:(t + 1) * LANES] for t in range(tk // LANES)]
        mx = functools.reduce(jnp.maximum, tiles)
        m_prev = m_scr[r0:r0 + nr, :]
        m_new = jnp.maximum(m_prev, jnp.max(mx, axis=-1, keepdims=True))
        alpha = jnp.exp2(m_prev - m_new)
        ps = [jnp.exp2(t - m_new) for t in tiles]
        l_scr[r0:r0 + nr, :] = alpha * l_scr[r0:r0 + nr, :] + functools.reduce(jnp.add, ps)
        p = jnp.concatenate(ps, axis=-1).astype(_BF16)
        acc_scr[r0:r0 + nr, :] = alpha * acc_scr[r0:r0 + nr, :] + _dot(p, v)
        m_scr[r0:r0 + nr, :] = m_new

    def query_block(qi, sa_scr, sa_next_scr):
        m_scr[...] = jnp.full_like(m_scr, -jnp.inf)
        l_scr[...] = jnp.zeros_like(l_scr)
        acc_scr[...] = jnp.zeros_like(acc_scr)

        def pair(jj, carry):
            j = 2 * jj
            sb_scr[...] = scores(qi, j + 1, 0, tq)
            update(j, sa_scr[...], 0, tq, False)
            sa_scr[...] = scores(qi, j + 2, 0, tq)
            update(j + 1, sb_scr[...], 0, tq, False)
            return carry

        lax.fori_loop(0, qi, pair, 0)

        jd = 2 * qi
        sb_scr[half:tq, :] = scores(qi, jd + 1, half, half)
        update(jd, sa_scr[0:half, :], 0, half, True)
        sa_next_scr[...] = scores(jnp.minimum(qi + 1, nq - 1), 0, 0, tq)
        update(jd, sa_scr[half:tq, :], half, half, False)
        update(jd + 1, sb_scr[half:tq, :], half, half, True)

        rows = pl.ds(pl.multiple_of(qi * tq, tq), tq)
        o = acc_scr[...] / jnp.sum(l_scr[...], axis=-1, keepdims=True)
        o_ref[rows, :] = (o * sz_ref[rows, :].astype(_F32)).astype(_BF16)

    sa0_scr[...] = scores(0, 0, 0, tq)

    def two_blocks(ib, carry):
        query_block(2 * ib, sa0_scr, sa1_scr)
        query_block(2 * ib + 1, sa1_scr, sa0_scr)
        return carry

    lax.fori_loop(0, nq // 2, two_blocks, 0)


def _attn_shifted_body(q_ref, kt_ref, v_ref, sz_ref, o_ref, pa0_scr, pa1_scr, pb_scr, lnext_scr, l_scr, acc_scr):
    tq = ATTN_TQ
    tk = ATTN_TK
    half = tq // 2
    assert tq == 2 * tk
    nq = q_ref.shape[2] // tq
    assert nq % 2 == 0
    col_minus_row = (lax.broadcasted_iota(jnp.int32, (tq, tk), 1)
                     - lax.broadcasted_iota(jnp.int32, (tq, tk), 0))

    def probs(qi, j, r0, nr, masked):
        qstart = pl.multiple_of(qi * tq + r0, half)
        kstart = pl.multiple_of(j * tk, tk)
        s = _dot(q_ref[0, 0, pl.ds(qstart, nr), :], kt_ref[0, 0, :, pl.ds(kstart, tk)])
        if masked:
            s = jnp.where(col_minus_row[r0:r0 + nr, :] <= qi * tq - j * tk, s, -jnp.inf)
        ps = [jnp.exp2(s[:, t * LANES:(t + 1) * LANES]) for t in range(tk // LANES)]
        return jnp.concatenate(ps, axis=-1).astype(_BF16), functools.reduce(jnp.add, ps)

    def values(j):
        return v_ref[0, 0, pl.ds(pl.multiple_of(j * tk, tk), tk), :]

    def query_block(qi, pa_scr, pa_next_scr):
        l_scr[...] = lnext_scr[...]
        acc_scr[...] = jnp.zeros_like(acc_scr)

        def pair(jj, carry):
            j = 2 * jj
            p, ls = probs(qi, j + 1, 0, tq, False)
            pb_scr[...] = p
            l_scr[...] += ls
            acc_scr[...] += _dot(pa_scr[...], values(j))
            p, ls = probs(qi, j + 2, 0, tq, True)
            pa_scr[...] = p
            l_scr[...] += ls
            acc_scr[...] += _dot(pb_scr[...], values(j + 1))
            return carry

        lax.fori_loop(0, qi, pair, 0)

        jd = 2 * qi
        p, ls = probs(qi, jd + 1, half, half, True)
        pb_scr[half:tq, :] = p
        l_scr[half:tq, :] += ls
        acc_scr[...] += _dot(pa_scr[...], values(jd))
        nxt = jnp.minimum(qi + 1, nq - 1)
        p, ls = probs(nxt, 0, 0, tq, True)
        pa_next_scr[...] = p
        lnext_scr[...] = ls
        acc_scr[half:tq, :] += _dot(pb_scr[half:tq, :], values(jd + 1))

        rows = pl.ds(pl.multiple_of(qi * tq, tq), tq)
        o = acc_scr[...] / jnp.sum(l_scr[...], axis=-1, keepdims=True)
        o_ref[rows, :] = (o * sz_ref[rows, :].astype(_F32)).astype(_BF16)

    p, ls = probs(0, 0, 0, tq, True)
    pa0_scr[...] = p
    lnext_scr[...] = ls

    def two_blocks(ib, carry):
        query_block(2 * ib, pa0_scr, pa1_scr)
        query_block(2 * ib + 1, pa1_scr, pa0_scr)
        return carry

    lax.fori_loop(0, nq // 2, two_blocks, 0)


def _attention(online_max, q, k, v, sz):
    bsz, _, seq, _ = q.shape
    tq = ATTN_TQ
    per_head = lambda width: pl.BlockSpec((1, 1, seq, width), lambda b, h: (b, h, 0, 0))
    stats = [pltpu.VMEM((tq, LANES), _F32), pltpu.VMEM((tq, LANES), _F32), pltpu.VMEM((tq, V_HEAD), _F32)]
    if online_max:
        body = _attn_online_body
        scratch = [pltpu.VMEM((tq, ATTN_TK), _F32)] * 3 + stats
    else:
        body = _attn_shifted_body
        scratch = [pltpu.VMEM((tq, ATTN_TK), _BF16)] * 3 + stats
    return pl.pallas_call(
        body,
        grid=(bsz, N_HEADS),
        in_specs=[
            per_head(HEAD_SLAB),
            pl.BlockSpec((1, 1, HEAD_SLAB, seq), lambda b, h: (b, h, 0, 0)),
            per_head(V_HEAD),
            pl.BlockSpec((seq, V_HEAD), lambda b, h: (b, h)),
        ],
        out_specs=pl.BlockSpec((seq, V_HEAD), lambda b, h: (b, h)),
        out_shape=jax.ShapeDtypeStruct((bsz * seq, D_ATTN), _BF16),
        scratch_shapes=scratch,
        compiler_params=pltpu.CompilerParams(
            dimension_semantics=("arbitrary", "arbitrary"), vmem_limit_bytes=VMEM_LIMIT),
        name="attention_online_max" if online_max else "attention_shifted",
    )(q, k, v, sz)


def _out_proj_body(yc_ref, ya_ref, x_ref, mod_ref, w32_ref, o_ref, w_scr):
    @pl.when(pl.program_id(0) == 0)
    def _():
        w_scr[...] = w32_ref[...].astype(_BF16)

    y = _dot(yc_ref[...], w_scr[0:D_CONV, :]) + _dot(ya_ref[...], w_scr[D_CONV:D_MIX, :])
    o_ref[...] = x_ref[...] + mod_ref[0, 2:3, :] * y


def _out_proj(yconv, yattn, x2d, mod3, wout, seq):
    t = x2d.shape[0]
    tm = OUT_TM
    tpb = seq // tm
    return pl.pallas_call(
        _out_proj_body,
        grid=(t // tm,),
        in_specs=[
            pl.BlockSpec((tm, D_CONV), lambda i: (i, 0)),
            pl.BlockSpec((tm, D_ATTN), lambda i: (i, 0)),
            pl.BlockSpec((tm, D_MODEL), lambda i: (i, 0)),
            pl.BlockSpec((1, 3, D_MODEL), lambda i: (i // tpb, 0, 0)),
            _resident((D_MIX, D_MODEL)),
        ],
        out_specs=pl.BlockSpec((tm, D_MODEL), lambda i: (i, 0)),
        out_shape=jax.ShapeDtypeStruct((t, D_MODEL), _F32),
        scratch_shapes=[pltpu.VMEM((D_MIX, D_MODEL), _BF16)],
        compiler_params=pltpu.CompilerParams(
            dimension_semantics=("arbitrary",), vmem_limit_bytes=VMEM_LIMIT),
        name="out_proj",
    )(yconv, yattn, x2d, mod3, wout)


def _score_shift(q_g, k_g, qscale):
    bound = QK_HEAD * qscale * jnp.max(jnp.abs(q_g)) * jnp.max(jnp.abs(k_g))
    shift = (bound * 1.03 + 1.0).astype(_BF16).astype(_F32)
    use_shift = bound + shift <= EXP2_NORMAL_RANGE
    return jnp.where(use_shift, shift, 0.0), use_shift


def _rope_slab_cols(w):
    z = jnp.zeros(w.shape[:-1] + (HALF,), w.dtype)
    return jnp.concatenate([w[..., :HALF], z, w[..., HALF:], z], axis=-1)


def _layer(layer, x2d, mod, cos_t, sin_t, norm_g, win_all, conv_w, q_a_g, w_q_b, kv_a_g, w_kv_b, q_g, k_g,
           w_out, bsz, seq):
    wtail = jnp.concatenate(
        [_rope_slab_cols(win_all[layer, :, OFF_KR:OFF_ZA]), win_all[layer, :, OFF_ZA:]], axis=-1)
    wq = w_q_b.reshape(Q_LORA, N_HEADS, QK_HEAD)
    wqb = jnp.concatenate([wq[..., :QK_NOPE], _rope_slab_cols(wq[..., QK_NOPE:])], axis=-1)
    wqb = wqb.reshape(Q_LORA, N_HEADS * HEAD_SLAB).astype(_BF16)
    wkvb = w_kv_b.astype(_BF16)
    qscale = math.log2(math.e) / math.sqrt(QK_HEAD)
    gq = jnp.concatenate([q_g[:QK_NOPE], _rope_slab_cols(q_g[QK_NOPE:])]).reshape(1, HEAD_SLAB) * qscale
    gk = jnp.concatenate([k_g[:QK_NOPE], _rope_slab_cols(k_g[QK_NOPE:])]).reshape(1, HEAD_SLAB)
    mod3 = mod.reshape(bsz, 3, D_MODEL)
    shift, use_shift = _score_shift(q_g, k_g, qscale)
    pad_lane = jnp.arange(ROPE_SLAB) == SHIFT_LANE
    qb = jnp.where(pad_lane, -shift, 0.0).reshape(1, ROPE_SLAB)
    kb = jnp.where(pad_lane, 1.0, 0.0).reshape(1, ROPE_SLAB)

    ng = norm_g.reshape(1, D_MODEL)
    h2d, sz, q, k, v = _mla_proj(
        layer, x2d, mod3, ng, win_all, wtail, q_a_g.reshape(1, Q_LORA), wqb,
        kv_a_g.reshape(1, KV_LORA), wkvb, gq, gk, qb, kb, cos_t, sin_t, bsz, seq)
    yconv = _conv_proj(layer, h2d, win_all, conv_w, seq)
    yattn = lax.cond(use_shift, functools.partial(_attention, False), functools.partial(_attention, True),
                     q, k, v, sz)
    return _out_proj(yconv, yattn, x2d, mod3, w_out, seq)


def kernel(x, c, positions, ada_w, ada_b, norm_g, w_in, conv_w, q_a_g, w_q_b, kv_a_g, w_kv_b, q_g, k_g, w_out):
    bsz, seq, _ = x.shape
    depth = ada_w.shape[0]
    cos_t, sin_t = _rope_tables(positions)
    x2d = x.reshape(bsz * seq, D_MODEL)
    win_all = w_in.astype(_BF16)
    for l in range(depth):
        mod = _adaln_mod(c, ada_w[l], ada_b[l])
        x2d = _layer(l, x2d, mod, cos_t, sin_t, norm_g[l], win_all, conv_w[l], q_a_g[l], w_q_b[l],
                     kv_a_g[l], w_kv_b[l], q_g[l], k_g[l], w_out[l], bsz, seq)
    return x2d.reshape(bsz, seq, D_MODEL)
```

```python
import functools
import itertools
import math

import jax
import jax.numpy as jnp
from jax import lax
from jax.experimental import pallas as pl
from jax.experimental.pallas import tpu as pltpu

D_MODEL = 2048
D_CONV = 1024
CONV_WIDTH = 3
N_HEADS = 8
QK_NOPE = 128
QK_ROPE = 64
QK_HEAD = QK_NOPE + QK_ROPE
V_HEAD = 128
D_ATTN = N_HEADS * V_HEAD
Q_LORA = 512
KV_LORA = 256
ROPE_BASE = 10000.0
D_MIX = D_CONV + D_ATTN
EPS = 1e-6

KV_HEAD = QK_NOPE + V_HEAD
MXU_N = 256
LANES = 128
SUBLANES = 8
ROPE_SLAB = LANES
HEAD_SLAB = QK_NOPE + ROPE_SLAB
HALF = QK_ROPE // 2
SHIFT_LANE = HALF
EXP2_NORMAL_RANGE = 120.0

OFF_XC = 0
OFF_BC = D_CONV
OFF_CC = 2 * D_CONV
OFF_ZC = 3 * D_CONV
OFF_CQ = 4 * D_CONV
OFF_CKV = OFF_CQ + Q_LORA
OFF_KR = OFF_CKV + KV_LORA
OFF_ZA = OFF_KR + QK_ROPE

ADA_TN = 1024
ROPE_TT = 2048
CONV_TM = 1024
MLA_TM = 512
CONV_CW = 256
HEAD_GROUPS = (4, 4)
ATTN_TQ = 1024
ATTN_TK = 512
OUT_TM = 512
VMEM_LIMIT = 56 * 1024 * 1024

_BF16 = jnp.bfloat16
_F32 = jnp.float32


def _silu(z):
    hz = 0.5 * z
    return hz + hz * jnp.tanh(hz)


def _dot(a, b):
    return jnp.dot(a, b, preferred_element_type=_F32)


def _adaln_body(ct_ref, w_ref, b_ref, o_ref):
    ct = ct_ref[...]
    sc = _silu(ct)
    w = w_ref[...]
    rows = [jnp.sum(w * sc[:, b:b + 1], axis=0, keepdims=True) for b in range(ct.shape[1])]
    o_ref[...] = jnp.concatenate(rows, axis=0) + b_ref[...]


def _adaln_mod(c, ada_w, ada_b):
    bsz = c.shape[0]
    n = ada_w.shape[1]
    return pl.pallas_call(
        _adaln_body,
        grid=(n // ADA_TN,),
        in_specs=[
            pl.BlockSpec((D_MODEL, bsz), lambda j: (0, 0)),
            pl.BlockSpec((D_MODEL, ADA_TN), lambda j: (0, j)),
            pl.BlockSpec((1, ADA_TN), lambda j: (0, j)),
        ],
        out_specs=pl.BlockSpec((bsz, ADA_TN), lambda j: (0, j)),
        out_shape=jax.ShapeDtypeStruct((bsz, n), _F32),
        name="adaln_mod",
    )(c.T, ada_w, ada_b.reshape(1, n))


def _rope_body(pos_ref, f_ref, cos_ref, sin_ref):
    ang = f_ref[...] * pos_ref[...].astype(_F32)
    c = jnp.cos(ang)
    s = jnp.sin(ang)
    z = jnp.zeros_like(c)
    cos_ref[...] = jnp.concatenate([c, z, c, z], axis=0).T
    sin_ref[...] = jnp.concatenate([-s, z, s, z], axis=0).T


def _rope_tables(positions):
    t = positions.size
    inv_freq = ROPE_BASE ** (-jnp.arange(0, QK_ROPE, 2, dtype=_F32) / QK_ROPE)
    out = pl.BlockSpec((ROPE_TT, ROPE_SLAB), lambda i: (i, 0))
    return pl.pallas_call(
        _rope_body,
        grid=(t // ROPE_TT,),
        in_specs=[pl.BlockSpec((1, ROPE_TT), lambda i: (0, i)), pl.BlockSpec((HALF, 1), lambda i: (0, 0))],
        out_specs=[out, out],
        out_shape=[jax.ShapeDtypeStruct((t, ROPE_SLAB), _F32)] * 2,
        name="rope_tables",
    )(positions.reshape(1, t), inv_freq.reshape(HALF, 1))


def _modulated_norm(x_ref, mod_ref, ng_ref):
    x = x_ref[...]
    xn = x * lax.rsqrt(jnp.mean(x * x, axis=-1, keepdims=True) + EPS)
    a = ng_ref[...] * (1.0 + mod_ref[0, 1:2, :])
    return (xn * a + mod_ref[0, 0:1, :]).astype(_BF16)


def _conv_proj_body(tiles_per_batch, h_ref, win_ref, convw_ref, yconv_ref, carry_scr, vbuf_scr):
    tm = h_ref.shape[0]

    @pl.when(pl.program_id(0) % tiles_per_batch == 0)
    def _():
        carry_scr[...] = jnp.zeros_like(carry_scr)

    def proj(off, width):
        return _dot(h_ref[...], win_ref[:, off:off + width])

    for lo in range(0, D_CONV, CONV_CW):
        xc = proj(OFF_XC + lo, CONV_CW)
        bc = proj(OFF_BC + lo, CONV_CW)
        cc = proj(OFF_CC + lo, CONV_CW)
        zc = proj(OFF_ZC + lo, CONV_CW)
        v0 = cc * xc
        vbuf_scr[0:SUBLANES, :] = carry_scr[:, lo:lo + CONV_CW]
        vbuf_scr[SUBLANES:SUBLANES + tm, :] = v0
        carry_scr[:, lo:lo + CONV_CW] = v0[tm - SUBLANES:, :]
        v1 = vbuf_scr[SUBLANES - 1:SUBLANES - 1 + tm, :]
        v2 = vbuf_scr[SUBLANES - 2:SUBLANES - 2 + tm, :]
        w = convw_ref[:, lo:lo + CONV_CW]
        conv = w[0:1, :] * v2 + w[1:2, :] * v1 + w[2:3, :] * v0
        yconv_ref[:, lo:lo + CONV_CW] = (bc * conv * _silu(zc)).astype(_BF16)


def _resident(shape, index=None):
    index = (0,) * len(shape) if index is None else index
    return pl.BlockSpec(shape, lambda i: index, pipeline_mode=pl.Buffered(1))


def _conv_proj(layer, h2d, win_all, conv_w, seq):
    t = h2d.shape[0]
    tm = CONV_TM
    tpb = seq // tm
    return pl.pallas_call(
        functools.partial(_conv_proj_body, tpb),
        grid=(t // tm,),
        in_specs=[
            pl.BlockSpec((tm, D_MODEL), lambda i: (i, 0)),
            _resident((None, D_MODEL, 4 * D_CONV), (layer, 0, 0)),
            _resident((CONV_WIDTH, D_CONV)),
        ],
        out_specs=pl.BlockSpec((tm, D_CONV), lambda i: (i, 0)),
        out_shape=jax.ShapeDtypeStruct((t, D_CONV), _BF16),
        scratch_shapes=[
            pltpu.VMEM((SUBLANES, D_CONV), _F32),
            pltpu.VMEM((SUBLANES + tm, CONV_CW), _F32),
        ],
        compiler_params=pltpu.CompilerParams(
            dimension_semantics=("arbitrary",), vmem_limit_bytes=VMEM_LIMIT),
        name="conv_proj",
    )(h2d, win_all, conv_w)


def _mla_proj_body(x_ref, mod_ref, ng_ref, wcq_ref, wckv_ref, wkr_ref, wza_ref, qag_ref, wqb_ref, kvag_ref, wkvb_ref,
                   gq_ref, gk_ref, qb_ref, kb_ref, cos_ref, sin_ref,
                   h_scr, sz_ref, q_ref, kt_ref, v_ref, cqn_scr, ckvn_scr):
    h_scr[...] = _modulated_norm(x_ref, mod_ref, ng_ref)
    cos_t = cos_ref[...]
    sin_t = sin_ref[...]

    def rope(slab):
        return slab * cos_t + pltpu.roll(slab, ROPE_SLAB // 2, axis=1) * sin_t

    def rms(parts, width):
        ss = sum(jnp.sum(p * p, axis=-1, keepdims=True) for p in parts)
        return lax.rsqrt(ss * (1.0 / width) + EPS)

    cq = _dot(h_scr[...], wcq_ref[...])
    cqn_scr[...] = (cq * rms([cq], Q_LORA) * qag_ref[...]).astype(_BF16)
    ckv = _dot(h_scr[...], wckv_ref[...])
    ckvn_scr[...] = (ckv * rms([ckv], KV_LORA) * kvag_ref[...]).astype(_BF16)

    kr = _dot(h_scr[...], wkr_ref[...])
    gk = gk_ref[...]
    gq = gq_ref[...]
    kr_rot = rope(kr * gk[:, QK_NOPE:HEAD_SLAB])

    def lane_sum(sq):
        ones = jnp.ones((sq.shape[1], LANES), _BF16)
        return _dot(sq.astype(_BF16), ones)

    ss_r = lane_sum(kr * kr)
    assert sum(HEAD_GROUPS) == N_HEADS
    for g0, group in zip(itertools.accumulate((0,) + HEAD_GROUPS), HEAD_GROUPS):
        heads = range(g0, g0 + group)
        gate_cols = group * V_HEAD
        slabs = [_dot(cqn_scr[...], wqb_ref[:, hd * HEAD_SLAB:(hd + 1) * HEAD_SLAB]) for hd in heads]
        kvs = [_dot(ckvn_scr[...], wkvb_ref[:, hd * KV_HEAD:(hd + 1) * KV_HEAD]) for hd in heads]
        ssq = [lane_sum(s * s) for s in slabs]
        ssk = [lane_sum(kv[:, 0:QK_NOPE] * kv[:, 0:QK_NOPE]) for kv in kvs]
        glo = g0 * V_HEAD
        zas = [(lo, min(2 * MXU_N, glo + gate_cols - lo)) for lo in range(glo, glo + gate_cols, 2 * MXU_N)]
        zas = [(lo, w, _dot(h_scr[...], wza_ref[:, lo:lo + w])) for lo, w in zas]
        for hd, slab, kv, sq, sk in zip(heads, slabs, kvs, ssq, ssk):
            r = lax.rsqrt(sq * (1.0 / QK_HEAD) + EPS)
            q_ref[0, hd, :, 0:QK_NOPE] = (slab[:, 0:QK_NOPE] * r * gq[:, 0:QK_NOPE]).astype(_BF16)
            q_rope = rope(slab[:, QK_NOPE:HEAD_SLAB] * r * gq[:, QK_NOPE:HEAD_SLAB])
            q_ref[0, hd, :, QK_NOPE:HEAD_SLAB] = (q_rope + qb_ref[...]).astype(_BF16)
            r = lax.rsqrt((sk + ss_r) * (1.0 / QK_HEAD) + EPS)
            kt_ref[0, hd, 0:QK_NOPE, :] = (kv[:, 0:QK_NOPE] * r * gk[:, 0:QK_NOPE]).T.astype(_BF16)
            kt_ref[0, hd, QK_NOPE:HEAD_SLAB, :] = (kr_rot * r + kb_ref[...]).T.astype(_BF16)
            v_ref[0, hd, :, :] = kv[:, QK_NOPE:KV_HEAD].astype(_BF16)
        for lo, w, za in zas:
            sz_ref[:, lo:lo + w] = _silu(za).astype(_BF16)


def _mla_proj(layer, x2d, mod3, norm_g, win_all, wkr, wza, q_a_g, wqb, kv_a_g, wkvb, gq, gk, qb, kb, cos_t, sin_t,
              bsz, seq):
    t = x2d.shape[0]
    tm = MLA_TM
    tpb = seq // tm
    tok = lambda width: pl.BlockSpec((tm, width), lambda i: (i, 0))
    head = lambda width: pl.BlockSpec((1, N_HEADS, tm, width), lambda i: (i // tpb, 0, i % tpb, 0))
    return pl.pallas_call(
        _mla_proj_body,
        grid=(t // tm,),
        in_specs=[
            tok(D_MODEL),
            pl.BlockSpec((1, 3, D_MODEL), lambda i: (i // tpb, 0, 0)),
            _resident((1, D_MODEL)),
            _resident((None, D_MODEL, Q_LORA), (layer, 0, OFF_CQ // Q_LORA)),
            _resident((None, D_MODEL, KV_LORA), (layer, 0, OFF_CKV // KV_LORA)),
            _resident((D_MODEL, ROPE_SLAB)),
            _resident((D_MODEL, D_ATTN)),
            _resident((1, Q_LORA)),
            _resident((Q_LORA, N_HEADS * HEAD_SLAB)),
            _resident((1, KV_LORA)),
            _resident((KV_LORA, N_HEADS * (QK_NOPE + V_HEAD))),
            _resident((1, HEAD_SLAB)),
            _resident((1, HEAD_SLAB)),
            _resident((1, ROPE_SLAB)),
            _resident((1, ROPE_SLAB)),
            tok(ROPE_SLAB),
            tok(ROPE_SLAB),
        ],
        out_specs=[
            tok(D_MODEL),
            tok(D_ATTN),
            head(HEAD_SLAB),
            pl.BlockSpec((1, N_HEADS, HEAD_SLAB, tm), lambda i: (i // tpb, 0, 0, i % tpb)),
            head(V_HEAD),
        ],
        out_shape=[
            jax.ShapeDtypeStruct((t, D_MODEL), _BF16),
            jax.ShapeDtypeStruct((t, D_ATTN), _BF16),
            jax.ShapeDtypeStruct((bsz, N_HEADS, seq, HEAD_SLAB), _BF16),
            jax.ShapeDtypeStruct((bsz, N_HEADS, HEAD_SLAB, seq), _BF16),
            jax.ShapeDtypeStruct((bsz, N_HEADS, seq, V_HEAD), _BF16),
        ],
        scratch_shapes=[
            pltpu.VMEM((tm, Q_LORA), _BF16),
            pltpu.VMEM((tm, KV_LORA), _BF16),
        ],
        compiler_params=pltpu.CompilerParams(
            dimension_semantics=("arbitrary",), vmem_limit_bytes=VMEM_LIMIT),
        name="mla_proj",
    )(x2d, mod3, norm_g, win_all, win_all, wkr, wza, q_a_g, wqb, kv_a_g, wkvb, gq, gk, qb, kb, cos_t, sin_t)


def _attn_online_body(q_ref, kt_ref, v_ref, sz_ref, o_ref, sa0_scr, sa1_scr, sb_scr, m_scr, l_scr, acc_scr):
    tq = ATTN_TQ
    tk = ATTN_TK
    half = tq // 2
    assert tq == 2 * tk
    nq = q_ref.shape[2] // tq
    assert nq % 2 == 0

    def scores(qi, j, r0, nr):
        qstart = pl.multiple_of(qi * tq + r0, half)
        kstart = pl.multiple_of(j * tk, tk)
        return _dot(q_ref[0, 0, pl.ds(qstart, nr), :], kt_ref[0, 0, :, pl.ds(kstart, tk)])

    def update(j, s, r0, nr, triangular):
        start = pl.multiple_of(j * tk, tk)
        v = v_ref[0, 0, pl.ds(start, tk), :]
        if triangular:
            row = lax.broadcasted_iota(jnp.int32, (nr, tk), 0)
            col = lax.broadcasted_iota(jnp.int32, (nr, tk), 1)
            s = jnp.where(col <= row, s, -jnp.inf)
        tiles = [s[:, t * LANES:(t + 1) * LANES] for t in range(tk // LANES)]
        mx = functools.reduce(jnp.maximum, tiles)
        m_prev = m_scr[r0:r0 + nr, :]
        m_new = jnp.maximum(m_prev, jnp.max(mx, axis=-1, keepdims=True))
        alpha = jnp.exp2(m_prev - m_new)
        ps = [jnp.exp2(t - m_new) for t in tiles]
        l_scr[r0:r0 + nr, :] = alpha * l_scr[r0:r0 + nr, :] + functools.reduce(jnp.add, ps)
        p = jnp.concatenate(ps, axis=-1).astype(_BF16)
        acc_scr[r0:r0 + nr, :] = alpha * acc_scr[r0:r0 + nr, :] + _dot(p, v)
        m_scr[r0:r0 + nr, :] = m_new

    def query_block(qi, sa_scr, sa_next_scr):
        m_scr[...] = jnp.full_like(m_scr, -jnp.inf)
        l_scr[...] = jnp.zeros_like(l_scr)
        acc_scr[...] = jnp.zeros_like(acc_scr)

        def pair(jj, carry):
            j = 2 * jj
            sb_scr[...] = scores(qi, j + 1, 0, tq)
            update(j, sa_scr[...], 0, tq, False)
            sa_scr[...] = scores(qi, j + 2, 0, tq)
            update(j + 1, sb_scr[...], 0, tq, False)
            return carry

        lax.fori_loop(0, qi, pair, 0)

        jd = 2 * qi
        sb_scr[half:tq, :] = scores(qi, jd + 1, half, half)
        update(jd, sa_scr[0:half, :], 0, half, True)
        sa_next_scr[...] = scores(jnp.minimum(qi + 1, nq - 1), 0, 0, tq)
        update(jd, sa_scr[half:tq, :], half, half, False)
        update(jd + 1, sb_scr[half:tq, :], half, half, True)

        rows = pl.ds(pl.multiple_of(qi * tq, tq), tq)
        o = acc_scr[...] / jnp.sum(l_scr[...], axis=-1, keepdims=True)
        o_ref[rows, :] = (o * sz_ref[rows, :].astype(_F32)).astype(_BF16)

    sa0_scr[...] = scores(0, 0, 0, tq)

    def two_blocks(ib, carry):
        query_block(2 * ib, sa0_scr, sa1_scr)
        query_block(2 * ib + 1, sa1_scr, sa0_scr)
        return carry

    lax.fori_loop(0, nq // 2, two_blocks, 0)


def _attn_shifted_body(q_ref, kt_ref, v_ref, sz_ref, o_ref, pa0_scr, pa1_scr, pb_scr, lnext_scr, l_scr, acc_scr):
    tq = ATTN_TQ
    tk = ATTN_TK
    half = tq // 2
    assert tq == 2 * tk
    nq = q_ref.shape[2] // tq
    assert nq % 2 == 0
    col_minus_row = (lax.broadcasted_iota(jnp.int32, (tq, tk), 1)
                     - lax.broadcasted_iota(jnp.int32, (tq, tk), 0))

    def probs(qi, j, r0, nr, masked):
        qstart = pl.multiple_of(qi * tq + r0, half)
        kstart = pl.multiple_of(j * tk, tk)
        s = _dot(q_ref[0, 0, pl.ds(qstart, nr), :], kt_ref[0, 0, :, pl.ds(kstart, tk)])
        if masked:
            s = jnp.where(col_minus_row[r0:r0 + nr, :] <= qi * tq - j * tk, s, -jnp.inf)
        ps = [jnp.exp2(s[:, t * LANES:(t + 1) * LANES]) for t in range(tk // LANES)]
        return jnp.concatenate(ps, axis=-1).astype(_BF16), functools.reduce(jnp.add, ps)

    def values(j):
        return v_ref[0, 0, pl.ds(pl.multiple_of(j * tk, tk), tk), :]

    def query_block(qi, pa_scr, pa_next_scr):
        l_scr[...] = lnext_scr[...]
        acc_scr[...] = jnp.zeros_like(acc_scr)

        def pair(jj, carry):
            j = 2 * jj
            p, ls = probs(qi, j + 1, 0, tq, False)
            pb_scr[...] = p
            l_scr[...] += ls
            acc_scr[...] += _dot(pa_scr[...], values(j))
            p, ls = probs(qi, j + 2, 0, tq, True)
            pa_scr[...] = p
            l_scr[...] += ls
            acc_scr[...] += _dot(pb_scr[...], values(j + 1))
            return carry

        lax.fori_loop(0, qi, pair, 0)

        jd = 2 * qi
        p, ls = probs(qi, jd + 1, half, half, True)
        pb_scr[half:tq, :] = p
        l_scr[half:tq, :] += ls
        acc_scr[...] += _dot(pa_scr[...], values(jd))
        nxt = jnp.minimum(qi + 1, nq - 1)
        p, ls = probs(nxt, 0, 0, tq, True)
        pa_next_scr[...] = p
        lnext_scr[...] = ls
        acc_scr[half:tq, :] += _dot(pb_scr[half:tq, :], values(jd + 1))

        rows = pl.ds(pl.multiple_of(qi * tq, tq), tq)
        o = acc_scr[...] / jnp.sum(l_scr[...], axis=-1, keepdims=True)
        o_ref[rows, :] = (o * sz_ref[rows, :].astype(_F32)).astype(_BF16)

    p, ls = probs(0, 0, 0, tq, True)
    pa0_scr[...] = p
    lnext_scr[...] = ls

    def two_blocks(ib, carry):
        query_block(2 * ib, pa0_scr, pa1_scr)
        query_block(2 * ib + 1, pa1_scr, pa0_scr)
        return carry

    lax.fori_loop(0, nq // 2, two_blocks, 0)


def _attention(online_max, q, k, v, sz):
    bsz, _, seq, _ = q.shape
    tq = ATTN_TQ
    per_head = lambda width: pl.BlockSpec((1, 1, seq, width), lambda b, h: (b, h, 0, 0))
    stats = [pltpu.VMEM((tq, LANES), _F32), pltpu.VMEM((tq, LANES), _F32), pltpu.VMEM((tq, V_HEAD), _F32)]
    if online_max:
        body = _attn_online_body
        scratch = [pltpu.VMEM((tq, ATTN_TK), _F32)] * 3 + stats
    else:
        body = _attn_shifted_body
        scratch = [pltpu.VMEM((tq, ATTN_TK), _BF16)] * 3 + stats
    return pl.pallas_call(
        body,
        grid=(bsz, N_HEADS),
        in_specs=[
            per_head(HEAD_SLAB),
            pl.BlockSpec((1, 1, HEAD_SLAB, seq), lambda b, h: (b, h, 0, 0)),
            per_head(V_HEAD),
            pl.BlockSpec((seq, V_HEAD), lambda b, h: (b, h)),
        ],
        out_specs=pl.BlockSpec((seq, V_HEAD), lambda b, h: (b, h)),
        out_shape=jax.ShapeDtypeStruct((bsz * seq, D_ATTN), _BF16),
        scratch_shapes=scratch,
        compiler_params=pltpu.CompilerParams(
            dimension_semantics=("arbitrary", "arbitrary"), vmem_limit_bytes=VMEM_LIMIT),
        name="attention_online_max" if online_max else "attention_shifted",
    )(q, k, v, sz)


def _out_proj_body(yc_ref, ya_ref, x_ref, mod_ref, w32_ref, o_ref, w_scr):
    @pl.when(pl.program_id(0) == 0)
    def _():
        w_scr[...] = w32_ref[...].astype(_BF16)

    y = _dot(yc_ref[...], w_scr[0:D_CONV, :]) + _dot(ya_ref[...], w_scr[D_CONV:D_MIX, :])
    o_ref[...] = x_ref[...] + mod_ref[0, 2:3, :] * y


def _out_proj(yconv, yattn, x2d, mod3, wout, seq):
    t = x2d.shape[0]
    tm = OUT_TM
    tpb = seq // tm
    return pl.pallas_call(
        _out_proj_body,
        grid=(t // tm,),
        in_specs=[
            pl.BlockSpec((tm, D_CONV), lambda i: (i, 0)),
            pl.BlockSpec((tm, D_ATTN), lambda i: (i, 0)),
            pl.BlockSpec((tm, D_MODEL), lambda i: (i, 0)),
            pl.BlockSpec((1, 3, D_MODEL), lambda i: (i // tpb, 0, 0)),
            _resident((D_MIX, D_MODEL)),
        ],
        out_specs=pl.BlockSpec((tm, D_MODEL), lambda i: (i, 0)),
        out_shape=jax.ShapeDtypeStruct((t, D_MODEL), _F32),
        scratch_shapes=[pltpu.VMEM((D_MIX, D_MODEL), _BF16)],
        compiler_params=pltpu.CompilerParams(
            dimension_semantics=("arbitrary",), vmem_limit_bytes=VMEM_LIMIT),
        name="out_proj",
    )(yconv, yattn, x2d, mod3, wout)


def _score_shift(q_g, k_g, qscale):
    bound = QK_HEAD * qscale * jnp.max(jnp.abs(q_g)) * jnp.max(jnp.abs(k_g))
    shift = (bound * 1.03 + 1.0).astype(_BF16).astype(_F32)
    use_shift = bound + shift <= EXP2_NORMAL_RANGE
    return jnp.where(use_shift, shift, 0.0), use_shift


def _rope_slab_cols(w):
    z = jnp.zeros(w.shape[:-1] + (HALF,), w.dtype)
    return jnp.concatenate([w[..., :HALF], z, w[..., HALF:], z], axis=-1)


def _layer(layer, x2d, mod, cos_t, sin_t, norm_g, win_all, conv_w, q_a_g, w_q_b, kv_a_g, w_kv_b, q_g, k_g,
           w_out, bsz, seq):
    wkr = _rope_slab_cols(win_all[layer, :, OFF_KR:OFF_ZA])
    wza = win_all[layer, :, OFF_ZA:]
    wq = w_q_b.reshape(Q_LORA, N_HEADS, QK_HEAD)
    wqb = jnp.concatenate([wq[..., :QK_NOPE], _rope_slab_cols(wq[..., QK_NOPE:])], axis=-1)
    wqb = wqb.reshape(Q_LORA, N_HEADS * HEAD_SLAB).astype(_BF16)
    wkvb = w_kv_b.astype(_BF16)
    qscale = math.log2(math.e) / math.sqrt(QK_HEAD)
    gq = jnp.concatenate([q_g[:QK_NOPE], _rope_slab_cols(q_g[QK_NOPE:])]).reshape(1, HEAD_SLAB) * qscale
    gk = jnp.concatenate([k_g[:QK_NOPE], _rope_slab_cols(k_g[QK_NOPE:])]).reshape(1, HEAD_SLAB)
    mod3 = mod.reshape(bsz, 3, D_MODEL)
    shift, use_shift = _score_shift(q_g, k_g, qscale)
    pad_lane = jnp.arange(ROPE_SLAB) == SHIFT_LANE
    qb = jnp.where(pad_lane, -shift, 0.0).reshape(1, ROPE_SLAB)
    kb = jnp.where(pad_lane, 1.0, 0.0).reshape(1, ROPE_SLAB)

    ng = norm_g.reshape(1, D_MODEL)
    h2d, sz, q, k, v = _mla_proj(
        layer, x2d, mod3, ng, win_all, wkr, wza, q_a_g.reshape(1, Q_LORA), wqb,
        kv_a_g.reshape(1, KV_LORA), wkvb, gq, gk, qb, kb, cos_t, sin_t, bsz, seq)
    yconv = _conv_proj(layer, h2d, win_all, conv_w, seq)
    yattn = lax.cond(use_shift, functools.partial(_attention, False), functools.partial(_attention, True),
                     q, k, v, sz)
    return _out_proj(yconv, yattn, x2d, mod3, w_out, seq)


def kernel(x, c, positions, ada_w, ada_b, norm_g, w_in, conv_w, q_a_g, w_q_b, kv_a_g, w_kv_b, q_g, k_g, w_out):
    bsz, seq, _ = x.shape
    depth = ada_w.shape[0]
    cos_t, sin_t = _rope_tables(positions)
    x2d = x.reshape(bsz * seq, D_MODEL)
    win_all = w_in.astype(_BF16)
    for l in range(depth):
        mod = _adaln_mod(c, ada_w[l], ada_b[l])
        x2d = _layer(l, x2d, mod, cos_t, sin_t, norm_g[l], win_all, conv_w[l], q_a_g[l], w_q_b[l],
                     kv_a_g[l], w_kv_b[l], q_g[l], k_g[l], w_out[l], bsz, seq)
    return x2d.reshape(bsz, seq, D_MODEL)
```

```python
import functools
import itertools
import math

import jax
import jax.numpy as jnp
from jax import lax
from jax.experimental import pallas as pl
from jax.experimental.pallas import tpu as pltpu

D_MODEL = 2048
D_CONV = 1024
CONV_WIDTH = 3
N_HEADS = 8
QK_NOPE = 128
QK_ROPE = 64
QK_HEAD = QK_NOPE + QK_ROPE
V_HEAD = 128
D_ATTN = N_HEADS * V_HEAD
Q_LORA = 512
KV_LORA = 256
ROPE_BASE = 10000.0
D_MIX = D_CONV + D_ATTN
EPS = 1e-6

KV_HEAD = QK_NOPE + V_HEAD
MXU_N = 256
LANES = 128
SUBLANES = 8
ROPE_SLAB = LANES
HEAD_SLAB = QK_NOPE + ROPE_SLAB
HALF = QK_ROPE // 2
SHIFT_LANE = HALF
EXP2_NORMAL_RANGE = 120.0

OFF_XC = 0
OFF_BC = D_CONV
OFF_CC = 2 * D_CONV
OFF_ZC = 3 * D_CONV
OFF_CQ = 4 * D_CONV
OFF_CKV = OFF_CQ + Q_LORA
OFF_KR = OFF_CKV + KV_LORA
OFF_ZA = OFF_KR + QK_ROPE
TAIL_COLS = ROPE_SLAB + D_ATTN

ADA_TN = 1024
ROPE_TT = 2048
CONV_TM = 1024
MLA_TM = 512
CONV_CW = 256
HEAD_GROUPS = (4, 4)
ATTN_TQ = 1024
ATTN_TK = 512
OUT_TM = 512
VMEM_LIMIT = 56 * 1024 * 1024

_BF16 = jnp.bfloat16
_F32 = jnp.float32


def _silu(z):
    hz = 0.5 * z
    return hz + hz * jnp.tanh(hz)


def _dot(a, b):
    return jnp.dot(a, b, preferred_element_type=_F32)


def _adaln_body(ct_ref, w_ref, b_ref, o_ref):
    ct = ct_ref[...]
    sc = _silu(ct)
    w = w_ref[...]
    rows = [jnp.sum(w * sc[:, b:b + 1], axis=0, keepdims=True) for b in range(ct.shape[1])]
    o_ref[...] = jnp.concatenate(rows, axis=0) + b_ref[...]


def _adaln_mod(c, ada_w, ada_b):
    bsz = c.shape[0]
    n = ada_w.shape[1]
    return pl.pallas_call(
        _adaln_body,
        grid=(n // ADA_TN,),
        in_specs=[
            pl.BlockSpec((D_MODEL, bsz), lambda j: (0, 0)),
            pl.BlockSpec((D_MODEL, ADA_TN), lambda j: (0, j)),
            pl.BlockSpec((1, ADA_TN), lambda j: (0, j)),
        ],
        out_specs=pl.BlockSpec((bsz, ADA_TN), lambda j: (0, j)),
        out_shape=jax.ShapeDtypeStruct((bsz, n), _F32),
        name="adaln_mod",
    )(c.T, ada_w, ada_b.reshape(1, n))


def _rope_body(pos_ref, f_ref, cos_ref, sin_ref):
    ang = f_ref[...] * pos_ref[...].astype(_F32)
    c = jnp.cos(ang)
    s = jnp.sin(ang)
    z = jnp.zeros_like(c)
    cos_ref[...] = jnp.concatenate([c, z, c, z], axis=0).T
    sin_ref[...] = jnp.concatenate([-s, z, s, z], axis=0).T


def _rope_tables(positions):
    t = positions.size
    inv_freq = ROPE_BASE ** (-jnp.arange(0, QK_ROPE, 2, dtype=_F32) / QK_ROPE)
    out = pl.BlockSpec((ROPE_TT, ROPE_SLAB), lambda i: (i, 0))
    return pl.pallas_call(
        _rope_body,
        grid=(t // ROPE_TT,),
        in_specs=[pl.BlockSpec((1, ROPE_TT), lambda i: (0, i)), pl.BlockSpec((HALF, 1), lambda i: (0, 0))],
        out_specs=[out, out],
        out_shape=[jax.ShapeDtypeStruct((t, ROPE_SLAB), _F32)] * 2,
        name="rope_tables",
    )(positions.reshape(1, t), inv_freq.reshape(HALF, 1))


def _modulated_norm(x_ref, mod_ref, ng_ref):
    x = x_ref[...]
    xn = x * lax.rsqrt(jnp.mean(x * x, axis=-1, keepdims=True) + EPS)
    a = ng_ref[...] * (1.0 + mod_ref[0, 1:2, :])
    return (xn * a + mod_ref[0, 0:1, :]).astype(_BF16)


def _conv_proj_body(tiles_per_batch, h_ref, win_ref, convw_ref, yconv_ref, carry_scr, vbuf_scr):
    tm = h_ref.shape[0]

    @pl.when(pl.program_id(0) % tiles_per_batch == 0)
    def _():
        carry_scr[...] = jnp.zeros_like(carry_scr)

    def proj(off, width):
        return _dot(h_ref[...], win_ref[:, off:off + width])

    for lo in range(0, D_CONV, CONV_CW):
        xc = proj(OFF_XC + lo, CONV_CW)
        bc = proj(OFF_BC + lo, CONV_CW)
        cc = proj(OFF_CC + lo, CONV_CW)
        zc = proj(OFF_ZC + lo, CONV_CW)
        v0 = cc * xc
        vbuf_scr[0:SUBLANES, :] = carry_scr[:, lo:lo + CONV_CW]
        vbuf_scr[SUBLANES:SUBLANES + tm, :] = v0
        carry_scr[:, lo:lo + CONV_CW] = v0[tm - SUBLANES:, :]
        v1 = vbuf_scr[SUBLANES - 1:SUBLANES - 1 + tm, :]
        v2 = vbuf_scr[SUBLANES - 2:SUBLANES - 2 + tm, :]
        w = convw_ref[:, lo:lo + CONV_CW]
        conv = w[0:1, :] * v2 + w[1:2, :] * v1 + w[2:3, :] * v0
        yconv_ref[:, lo:lo + CONV_CW] = (bc * conv * _silu(zc)).astype(_BF16)


def _resident(shape, index=None):
    index = (0,) * len(shape) if index is None else index
    return pl.BlockSpec(shape, lambda i: index, pipeline_mode=pl.Buffered(1))


def _conv_proj(layer, h2d, win_all, conv_w, seq):
    t = h2d.shape[0]
    tm = CONV_TM
    tpb = seq // tm
    return pl.pallas_call(
        functools.partial(_conv_proj_body, tpb),
        grid=(t // tm,),
        in_specs=[
            pl.BlockSpec((tm, D_MODEL), lambda i: (i, 0)),
            _resident((None, D_MODEL, 4 * D_CONV), (layer, 0, 0)),
            _resident((CONV_WIDTH, D_CONV)),
        ],
        out_specs=pl.BlockSpec((tm, D_CONV), lambda i: (i, 0)),
        out_shape=jax.ShapeDtypeStruct((t, D_CONV), _BF16),
        scratch_shapes=[
            pltpu.VMEM((SUBLANES, D_CONV), _F32),
            pltpu.VMEM((SUBLANES + tm, CONV_CW), _F32),
        ],
        compiler_params=pltpu.CompilerParams(
            dimension_semantics=("arbitrary",), vmem_limit_bytes=VMEM_LIMIT),
        name="conv_proj",
    )(h2d, win_all, conv_w)


def _mla_proj_body(x_ref, mod_ref, ng_ref, wcq_ref, wckv_ref, wtail_ref, qag_ref, wqb_ref, kvag_ref, wkvb_ref,
                   gq_ref, gk_ref, qb_ref, kb_ref, cos_ref, sin_ref,
                   h_scr, sz_ref, q_ref, kt_ref, v_ref, cqn_scr, ckvn_scr):
    h_scr[...] = _modulated_norm(x_ref, mod_ref, ng_ref)
    cos_t = cos_ref[...]
    sin_t = sin_ref[...]

    def rope(slab):
        return slab * cos_t + pltpu.roll(slab, ROPE_SLAB // 2, axis=1) * sin_t

    def rms(parts, width):
        ss = sum(jnp.sum(p * p, axis=-1, keepdims=True) for p in parts)
        return lax.rsqrt(ss * (1.0 / width) + EPS)

    cq = _dot(h_scr[...], wcq_ref[...])
    cqn_scr[...] = (cq * rms([cq], Q_LORA) * qag_ref[...]).astype(_BF16)
    ckv = _dot(h_scr[...], wckv_ref[...])
    ckvn_scr[...] = (ckv * rms([ckv], KV_LORA) * kvag_ref[...]).astype(_BF16)

    kr = _dot(h_scr[...], wtail_ref[:, 0:ROPE_SLAB])
    gk = gk_ref[...]
    gq = gq_ref[...]
    kr_rot = rope(kr * gk[:, QK_NOPE:HEAD_SLAB])

    block = lambda axis: lax.broadcasted_iota(jnp.int32, (2 * LANES, 2 * LANES), axis) // LANES
    pair_ones = (block(0) == block(1)).astype(_BF16)

    def lane_sums(sqs):
        sums = []
        for a, b in zip(sqs[0::2], sqs[1::2]):
            both = _dot(jnp.concatenate([a, b], axis=-1).astype(_BF16), pair_ones)
            sums += [both[:, 0:LANES], both[:, LANES:2 * LANES]]
        return sums

    ss_r = lane_sums([kr * kr, jnp.zeros_like(kr)])[0]
    assert sum(HEAD_GROUPS) == N_HEADS and all(g % 2 == 0 for g in HEAD_GROUPS)
    for g0, group in zip(itertools.accumulate((0,) + HEAD_GROUPS), HEAD_GROUPS):
        heads = range(g0, g0 + group)
        gate_cols = group * V_HEAD
        slabs = [_dot(cqn_scr[...], wqb_ref[:, hd * HEAD_SLAB:(hd + 1) * HEAD_SLAB]) for hd in heads]
        kvs = [_dot(ckvn_scr[...], wkvb_ref[:, hd * KV_HEAD:(hd + 1) * KV_HEAD]) for hd in heads]
        sqs = [s * s for s in slabs]
        ssq = lane_sums([sq[:, 0:QK_NOPE] + sq[:, QK_NOPE:HEAD_SLAB] for sq in sqs])
        ssk = lane_sums([kv[:, 0:QK_NOPE] * kv[:, 0:QK_NOPE] for kv in kvs])
        glo = g0 * V_HEAD
        zas = [(lo, min(2 * MXU_N, glo + gate_cols - lo)) for lo in range(glo, glo + gate_cols, 2 * MXU_N)]
        zas = [(lo, w, _dot(h_scr[...], wtail_ref[:, ROPE_SLAB + lo:ROPE_SLAB + lo + w])) for lo, w in zas]
        for hd, slab, kv, sq, sk in zip(heads, slabs, kvs, ssq, ssk):
            r = lax.rsqrt(sq * (1.0 / QK_HEAD) + EPS)
            q_ref[0, hd, :, 0:QK_NOPE] = (slab[:, 0:QK_NOPE] * r * gq[:, 0:QK_NOPE]).astype(_BF16)
            q_rope = rope(slab[:, QK_NOPE:HEAD_SLAB] * r * gq[:, QK_NOPE:HEAD_SLAB])
            q_ref[0, hd, :, QK_NOPE:HEAD_SLAB] = (q_rope + qb_ref[...]).astype(_BF16)
            r = lax.rsqrt((sk + ss_r) * (1.0 / QK_HEAD) + EPS)
            kt_ref[0, hd, 0:QK_NOPE, :] = (kv[:, 0:QK_NOPE] * r * gk[:, 0:QK_NOPE]).T.astype(_BF16)
            kt_ref[0, hd, QK_NOPE:HEAD_SLAB, :] = (kr_rot * r + kb_ref[...]).T.astype(_BF16)
            v_ref[0, hd, :, :] = kv[:, QK_NOPE:KV_HEAD].astype(_BF16)
        for lo, w, za in zas:
            sz_ref[:, lo:lo + w] = _silu(za).astype(_BF16)


def _mla_proj(layer, x2d, mod3, norm_g, win_all, wtail, q_a_g, wqb, kv_a_g, wkvb, gq, gk, qb, kb, cos_t, sin_t,
              bsz, seq):
    t = x2d.shape[0]
    tm = MLA_TM
    tpb = seq // tm
    tok = lambda width: pl.BlockSpec((tm, width), lambda i: (i, 0))
    head = lambda width: pl.BlockSpec((1, N_HEADS, tm, width), lambda i: (i // tpb, 0, i % tpb, 0))
    return pl.pallas_call(
        _mla_proj_body,
        grid=(t // tm,),
        in_specs=[
            tok(D_MODEL),
            pl.BlockSpec((1, 3, D_MODEL), lambda i: (i // tpb, 0, 0)),
            _resident((1, D_MODEL)),
            _resident((None, D_MODEL, Q_LORA), (layer, 0, OFF_CQ // Q_LORA)),
            _resident((None, D_MODEL, KV_LORA), (layer, 0, OFF_CKV // KV_LORA)),
            _resident((D_MODEL, TAIL_COLS)),
            _resident((1, Q_LORA)),
            _resident((Q_LORA, N_HEADS * HEAD_SLAB)),
            _resident((1, KV_LORA)),
            _resident((KV_LORA, N_HEADS * (QK_NOPE + V_HEAD))),
            _resident((1, HEAD_SLAB)),
            _resident((1, HEAD_SLAB)),
            _resident((1, ROPE_SLAB)),
            _resident((1, ROPE_SLAB)),
            tok(ROPE_SLAB),
            tok(ROPE_SLAB),
        ],
        out_specs=[
            tok(D_MODEL),
            tok(D_ATTN),
            head(HEAD_SLAB),
            pl.BlockSpec((1, N_HEADS, HEAD_SLAB, tm), lambda i: (i // tpb, 0, 0, i % tpb)),
            head(V_HEAD),
        ],
        out_shape=[
            jax.ShapeDtypeStruct((t, D_MODEL), _BF16),
            jax.ShapeDtypeStruct((t, D_ATTN), _BF16),
            jax.ShapeDtypeStruct((bsz, N_HEADS, seq, HEAD_SLAB), _BF16),
            jax.ShapeDtypeStruct((bsz, N_HEADS, HEAD_SLAB, seq), _BF16),
            jax.ShapeDtypeStruct((bsz, N_HEADS, seq, V_HEAD), _BF16),
        ],
        scratch_shapes=[
            pltpu.VMEM((tm, Q_LORA), _BF16),
            pltpu.VMEM((tm, KV_LORA), _BF16),
        ],
        compiler_params=pltpu.CompilerParams(
            dimension_semantics=("arbitrary",), vmem_limit_bytes=VMEM_LIMIT),
        name="mla_proj",
    )(x2d, mod3, norm_g, win_all, win_all, wtail, q_a_g, wqb, kv_a_g, wkvb, gq, gk, qb, kb, cos_t, sin_t)


def _attn_online_body(q_ref, kt_ref, v_ref, sz_ref, o_ref, sa0_scr, sa1_scr, sb_scr, m_scr, l_scr, acc_scr):
    tq = ATTN_TQ
    tk = ATTN_TK
    half = tq // 2
    assert tq == 2 * tk
    nq = q_ref.shape[2] // tq
    assert nq % 2 == 0

    def scores(qi, j, r0, nr):
        qstart = pl.multiple_of(qi * tq + r0, half)
        kstart = pl.multiple_of(j * tk, tk)
        return _dot(q_ref[0, 0, pl.ds(qstart, nr), :], kt_ref[0, 0, :, pl.ds(kstart, tk)])

    def update(j, s, r0, nr, triangular):
        start = pl.multiple_of(j * tk, tk)
        v = v_ref[0, 0, pl.ds(start, tk), :]
        if triangular:
            row = lax.broadcasted_iota(jnp.int32, (nr, tk), 0)
            col = lax.broadcasted_iota(jnp.int32, (nr, tk), 1)
            s = jnp.where(col <= row, s, -jnp.inf)
        tiles = [s[:, t * LANES:(t + 1) * LANES] for t in range(tk // LANES)]
        mx = functools.reduce(jnp.maximum, tiles)
        m_prev = m_scr[r0:r0 + nr, :]
        m_new = jnp.maximum(m_prev, jnp.max(mx, axis=-1, keepdims=True))
        alpha = jnp.exp2(m_prev - m_new)
        ps = [jnp.exp2(t - m_new) for t in tiles]
        l_scr[r0:r0 + nr, :] = alpha * l_scr[r0:r0 + nr, :] + functools.reduce(jnp.add, ps)
        p = jnp.concatenate(ps, axis=-1).astype(_BF16)
        acc_scr[r0:r0 + nr, :] = alpha * acc_scr[r0:r0 + nr, :] + _dot(p, v)
        m_scr[r0:r0 + nr, :] = m_new

    def query_block(qi, sa_scr, sa_next_scr):
        m_scr[...] = jnp.full_like(m_scr, -jnp.inf)
        l_scr[...] = jnp.zeros_like(l_scr)
        acc_scr[...] = jnp.zeros_like(acc_scr)

        def pair(jj, carry):
            j = 2 * jj
            sb_scr[...] = scores(qi, j + 1, 0, tq)
            update(j, sa_scr[...], 0, tq, False)
            sa_scr[...] = scores(qi, j + 2, 0, tq)
            update(j + 1, sb_scr[...], 0, tq, False)
            return carry

        lax.fori_loop(0, qi, pair, 0)

        jd = 2 * qi
        sb_scr[half:tq, :] = scores(qi, jd + 1, half, half)
        update(jd, sa_scr[0:half, :], 0, half, True)
        sa_next_scr[...] = scores(jnp.minimum(qi + 1, nq - 1), 0, 0, tq)
        update(jd, sa_scr[half:tq, :], half, half, False)
        update(jd + 1, sb_scr[half:tq, :], half, half, True)

        rows = pl.ds(pl.multiple_of(qi * tq, tq), tq)
        o = acc_scr[...] / jnp.sum(l_scr[...], axis=-1, keepdims=True)
        o_ref[rows, :] = (o * sz_ref[rows, :].astype(_F32)).astype(_BF16)

    sa0_scr[...] = scores(0, 0, 0, tq)

    def two_blocks(ib, carry):
        query_block(2 * ib, sa0_scr, sa1_scr)
        query_block(2 * ib + 1, sa1_scr, sa0_scr)
        return carry

    lax.fori_loop(0, nq // 2, two_blocks, 0)


def _attn_shifted_body(q_ref, kt_ref, v_ref, sz_ref, o_ref, pa0_scr, pa1_scr, pb_scr, lnext_scr, l_scr, acc_scr):
    tq = ATTN_TQ
    tk = ATTN_TK
    half = tq // 2
    assert tq == 2 * tk
    nq = q_ref.shape[2] // tq
    assert nq % 2 == 0
    col_minus_row = (lax.broadcasted_iota(jnp.int32, (tq, tk), 1)
                     - lax.broadcasted_iota(jnp.int32, (tq, tk), 0))

    def probs(qi, j, r0, nr, masked):
        qstart = pl.multiple_of(qi * tq + r0, half)
        kstart = pl.multiple_of(j * tk, tk)
        s = _dot(q_ref[0, 0, pl.ds(qstart, nr), :], kt_ref[0, 0, :, pl.ds(kstart, tk)])
        if masked:
            s = jnp.where(col_minus_row[r0:r0 + nr, :] <= qi * tq - j * tk, s, -jnp.inf)
        ps = [jnp.exp2(s[:, t * LANES:(t + 1) * LANES]) for t in range(tk // LANES)]
        return jnp.concatenate(ps, axis=-1).astype(_BF16), functools.reduce(jnp.add, ps)

    def values(j):
        return v_ref[0, 0, pl.ds(pl.multiple_of(j * tk, tk), tk), :]

    def query_block(qi, pa_scr, pa_next_scr):
        l_scr[...] = lnext_scr[...]
        acc_scr[...] = jnp.zeros_like(acc_scr)

        def pair(jj, carry):
            j = 2 * jj
            p, ls = probs(qi, j + 1, 0, tq, False)
            pb_scr[...] = p
            l_scr[...] += ls
            acc_scr[...] += _dot(pa_scr[...], values(j))
            p, ls = probs(qi, j + 2, 0, tq, True)
            pa_scr[...] = p
            l_scr[...] += ls
            acc_scr[...] += _dot(pb_scr[...], values(j + 1))
            return carry

        lax.fori_loop(0, qi, pair, 0)

        jd = 2 * qi
        p, ls = probs(qi, jd + 1, half, half, True)
        pb_scr[half:tq, :] = p
        l_scr[half:tq, :] += ls
        acc_scr[...] += _dot(pa_scr[...], values(jd))
        nxt = jnp.minimum(qi + 1, nq - 1)
        p, ls = probs(nxt, 0, 0, tq, True)
        pa_next_scr[...] = p
        lnext_scr[...] = ls
        acc_scr[half:tq, :] += _dot(pb_scr[half:tq, :], values(jd + 1))

        rows = pl.ds(pl.multiple_of(qi * tq, tq), tq)
        o = acc_scr[...] / jnp.sum(l_scr[...], axis=-1, keepdims=True)
        o_ref[rows, :] = (o * sz_ref[rows, :].astype(_F32)).astype(_BF16)

    p, ls = probs(0, 0, 0, tq, True)
    pa0_scr[...] = p
    lnext_scr[...] = ls

    def two_blocks(ib, carry):
        query_block(2 * ib, pa0_scr, pa1_scr)
        query_block(2 * ib + 1, pa1_scr, pa0_scr)
        return carry

    lax.fori_loop(0, nq // 2, two_blocks, 0)


def _attention(online_max, q, k, v, sz):
    bsz, _, seq, _ = q.shape
    tq = ATTN_TQ
    per_head = lambda width: pl.BlockSpec((1, 1, seq, width), lambda b, h: (b, h, 0, 0))
    stats = [pltpu.VMEM((tq, LANES), _F32), pltpu.VMEM((tq, LANES), _F32), pltpu.VMEM((tq, V_HEAD), _F32)]
    if online_max:
        body = _attn_online_body
        scratch = [pltpu.VMEM((tq, ATTN_TK), _F32)] * 3 + stats
    else:
        body = _attn_shifted_body
        scratch = [pltpu.VMEM((tq, ATTN_TK), _BF16)] * 3 + stats
    return pl.pallas_call(
        body,
        grid=(bsz, N_HEADS),
        in_specs=[
            per_head(HEAD_SLAB),
            pl.BlockSpec((1, 1, HEAD_SLAB, seq), lambda b, h: (b, h, 0, 0)),
            per_head(V_HEAD),
            pl.BlockSpec((seq, V_HEAD), lambda b, h: (b, h)),
        ],
        out_specs=pl.BlockSpec((seq, V_HEAD), lambda b, h: (b, h)),
        out_shape=jax.ShapeDtypeStruct((bsz * seq, D_ATTN), _BF16),
        scratch_shapes=scratch,
        compiler_params=pltpu.CompilerParams(
            dimension_semantics=("arbitrary", "arbitrary"), vmem_limit_bytes=VMEM_LIMIT),
        name="attention_online_max" if online_max else "attention_shifted",
    )(q, k, v, sz)


def _out_proj_body(yc_ref, ya_ref, x_ref, mod_ref, w32_ref, o_ref, w_scr):
    @pl.when(pl.program_id(0) == 0)
    def _():
        w_scr[...] = w32_ref[...].astype(_BF16)

    y = _dot(yc_ref[...], w_scr[0:D_CONV, :]) + _dot(ya_ref[...], w_scr[D_CONV:D_MIX, :])
    o_ref[...] = x_ref[...] + mod_ref[0, 2:3, :] * y


def _out_proj(yconv, yattn, x2d, mod3, wout, seq):
    t = x2d.shape[0]
    tm = OUT_TM
    tpb = seq // tm
    return pl.pallas_call(
        _out_proj_body,
        grid=(t // tm,),
        in_specs=[
            pl.BlockSpec((tm, D_CONV), lambda i: (i, 0)),
            pl.BlockSpec((tm, D_ATTN), lambda i: (i, 0)),
            pl.BlockSpec((tm, D_MODEL), lambda i: (i, 0)),
            pl.BlockSpec((1, 3, D_MODEL), lambda i: (i // tpb, 0, 0)),
            _resident((D_MIX, D_MODEL)),
        ],
        out_specs=pl.BlockSpec((tm, D_MODEL), lambda i: (i, 0)),
        out_shape=jax.ShapeDtypeStruct((t, D_MODEL), _F32),
        scratch_shapes=[pltpu.VMEM((D_MIX, D_MODEL), _BF16)],
        compiler_params=pltpu.CompilerParams(
            dimension_semantics=("arbitrary",), vmem_limit_bytes=VMEM_LIMIT),
        name="out_proj",
    )(yconv, yattn, x2d, mod3, wout)


def _score_shift(q_g, k_g, qscale):
    bound = QK_HEAD * qscale * jnp.max(jnp.abs(q_g)) * jnp.max(jnp.abs(k_g))
    shift = (bound * 1.03 + 1.0).astype(_BF16).astype(_F32)
    use_shift = bound + shift <= EXP2_NORMAL_RANGE
    return jnp.where(use_shift, shift, 0.0), use_shift


def _rope_slab_cols(w):
    z = jnp.zeros(w.shape[:-1] + (HALF,), w.dtype)
    return jnp.concatenate([w[..., :HALF], z, w[..., HALF:], z], axis=-1)


def _layer(layer, x2d, mod, cos_t, sin_t, norm_g, win_all, conv_w, q_a_g, w_q_b, kv_a_g, w_kv_b, q_g, k_g,
           w_out, bsz, seq):
    wtail = jnp.concatenate(
        [_rope_slab_cols(win_all[layer, :, OFF_KR:OFF_ZA]), win_all[layer, :, OFF_ZA:]], axis=-1)
    wq = w_q_b.reshape(Q_LORA, N_HEADS, QK_HEAD)
    wqb = jnp.concatenate([wq[..., :QK_NOPE], _rope_slab_cols(wq[..., QK_NOPE:])], axis=-1)
    wqb = wqb.reshape(Q_LORA, N_HEADS * HEAD_SLAB).astype(_BF16)
    wkvb = w_kv_b.astype(_BF16)
    qscale = math.log2(math.e) / math.sqrt(QK_HEAD)
    gq = jnp.concatenate([q_g[:QK_NOPE], _rope_slab_cols(q_g[QK_NOPE:])]).reshape(1, HEAD_SLAB) * qscale
    gk = jnp.concatenate([k_g[:QK_NOPE], _rope_slab_cols(k_g[QK_NOPE:])]).reshape(1, HEAD_SLAB)
    mod3 = mod.reshape(bsz, 3, D_MODEL)
    shift, use_shift = _score_shift(q_g, k_g, qscale)
    pad_lane = jnp.arange(ROPE_SLAB) == SHIFT_LANE
    qb = jnp.where(pad_lane, -shift, 0.0).reshape(1, ROPE_SLAB)
    kb = jnp.where(pad_lane, 1.0, 0.0).reshape(1, ROPE_SLAB)

    ng = norm_g.reshape(1, D_MODEL)
    h2d, sz, q, k, v = _mla_proj(
        layer, x2d, mod3, ng, win_all, wtail, q_a_g.reshape(1, Q_LORA), wqb,
        kv_a_g.reshape(1, KV_LORA), wkvb, gq, gk, qb, kb, cos_t, sin_t, bsz, seq)
    yconv = _conv_proj(layer, h2d, win_all, conv_w, seq)
    yattn = lax.cond(use_shift, functools.partial(_attention, False), functools.partial(_attention, True),
                     q, k, v, sz)
    return _out_proj(yconv, yattn, x2d, mod3, w_out, seq)


def kernel(x, c, positions, ada_w, ada_b, norm_g, w_in, conv_w, q_a_g, w_q_b, kv_a_g, w_kv_b, q_g, k_g, w_out):
    bsz, seq, _ = x.shape
    depth = ada_w.shape[0]
    cos_t, sin_t = _rope_tables(positions)
    x2d = x.reshape(bsz * seq, D_MODEL)
    win_all = w_in.astype(_BF16)
    for l in range(depth):
        mod = _adaln_mod(c, ada_w[l], ada_b[l])
        x2d = _layer(l, x2d, mod, cos_t, sin_t, norm_g[l], win_all, conv_w[l], q_a_g[l], w_q_b[l],
                     kv_a_g[l], w_kv_b[l], q_g[l], k_g[l], w_out[l], bsz, seq)
    return x2d.reshape(bsz, seq, D_MODEL)
```
